```python
import jax, jax.numpy as jnp
from jax import lax
import numpy as np

D_MODEL = 2048
BATCH = 4
SEQ = 4096
DEPTH = 1
DEC_BATCH = 32
DEC_SEQ = 16
PAST_LEN = 2048

CHUNK = 64
QBLOCK = 128
D_ATTN = D_MODEL // 2
D_CONV = D_MODEL - D_ATTN
N_HEADS = 8
HEAD_DIM = D_ATTN // N_HEADS
N_KV_HEADS = 2
KV_GROUP = N_HEADS // N_KV_HEADS
KV_DIM = N_KV_HEADS * HEAD_DIM
N_IDX_HEADS = 16
IDX_DIM = 64
TOPK_MAX = 256
CONV_WIDTH = 3
RMS_EPS = 1e-6
SPLIT_SIZES = (D_ATTN, KV_DIM, KV_DIM, D_ATTN, N_IDX_HEADS * IDX_DIM, IDX_DIM, N_IDX_HEADS,
               D_CONV, D_CONV, D_CONV, D_CONV)
D_PROJ = 3 * D_ATTN + 2 * KV_DIM + N_IDX_HEADS * IDX_DIM + IDX_DIM + N_IDX_HEADS + D_CONV * 3

kernel_name = "hybrid_dsa_shortconv_stream_step"


def _offsets():
    offs, acc = [], 0
    for s in SPLIT_SIZES[:-1]:
        acc += s
        offs.append(acc)
    return offs


def _rmsnorm(x, g):
    xf = x.astype(jnp.float32)
    xf = xf * lax.rsqrt(jnp.mean(xf * xf, axis=-1, keepdims=True) + RMS_EPS)
    return (xf * g.astype(jnp.float32)).astype(x.dtype)


def _project(x, g_pre, w_in):
    b, t, _ = x.shape
    h = _rmsnorm(x, g_pre)
    z = jnp.einsum('btd,de->bte', h, w_in)
    q, k, v, ga, qi, ki, wi, bg, cg, hc, gb = jnp.split(z, _offsets(), axis=-1)
    q = q.reshape(b, t, N_KV_HEADS, KV_GROUP, HEAD_DIM)
    k = k.reshape(b, t, N_KV_HEADS, HEAD_DIM)
    v = v.reshape(b, t, N_KV_HEADS, HEAD_DIM)
    qi = qi.reshape(b, t, N_IDX_HEADS, IDX_DIM)
    return q, k, v, ga, qi, ki, wi, bg, cg, hc, gb


def _sparse_attend(q, qi, wi, qpos, k_all, v_all, ki_all, topk):
    b, tq = q.shape[:2]
    s = k_all.shape[1]
    qchunk = qpos // CHUNK
    kchunk = jnp.arange(s, dtype=jnp.int32) // CHUNK
    iscore = jnp.einsum('bthd,bsd->btsh', qi, ki_all, preferred_element_type=jnp.float32) * (IDX_DIM ** -0.5)
    iscore = jnp.einsum('btsh,bth->bts', jax.nn.relu(iscore), wi.astype(jnp.float32)) * (N_IDX_HEADS ** -0.5)
    admissible = kchunk[None, :] <= qchunk[:, None]
    iscore = jnp.where(admissible[None], iscore, -jnp.inf)
    _, sel = lax.top_k(iscore, topk)
    gather = jax.vmap(lambda rows, idx: rows[idx])
    k_sel = gather(k_all, sel)
    v_sel = gather(v_all, sel)
    logits = jnp.einsum('btkgd,btnkd->btkgn', q, k_sel, preferred_element_type=jnp.float32) * (HEAD_DIM ** -0.5)
    valid = (sel // CHUNK) <= qchunk[None, :, None]
    logits = jnp.where(valid[:, :, None, None, :], logits, -jnp.inf)
    probs = jax.nn.softmax(logits, axis=-1).astype(v_all.dtype)
    out = jnp.einsum('btkgn,btnkd->btkgd', probs, v_sel)
    return out.reshape(b, tq, D_ATTN)


def _prompt_attention(q, qi, wi, k, v, ki):
    b, t = q.shape[:2]
    nblk = t // QBLOCK
    topk = min(TOPK_MAX, t // 4)

    def to_blocks(a):
        return jnp.moveaxis(a.reshape((b, nblk, QBLOCK) + a.shape[2:]), 1, 0)

    qpos = jnp.arange(t, dtype=jnp.int32).reshape(nblk, QBLOCK)

    def block(args):
        qb, qib, wib, pb = args
        return _sparse_attend(qb, qib, wib, pb, k, v, ki, topk)

    out = lax.map(block, (to_blocks(q), to_blocks(qi), to_blocks(wi), qpos))
    return jnp.moveaxis(out, 0, 1).reshape(b, t, D_ATTN)


def _short_conv(bg, cg, hc, prev, w_conv):
    t = hc.shape[1]
    u = cg * hc
    u_pad = jnp.concatenate([prev.astype(u.dtype), u], axis=1)
    conv = w_conv[0] * u_pad[:, 0:t]
    for j in range(1, CONV_WIDTH):
        conv = conv + w_conv[j] * u_pad[:, j:j + t]
    return bg * conv, u_pad[:, -(CONV_WIDTH - 1):]


def _merge(x, attn, ga, conv, gb, w_out, g_post):
    z = jnp.concatenate([attn * jax.nn.silu(ga), conv * jax.nn.silu(gb)], axis=-1)
    return x + _rmsnorm(jnp.einsum('bte,ed->btd', z, w_out), g_post)


def _prompt_layer(x, g_pre, w_in, w_conv, w_out, g_post):
    q, k, v, ga, qi, ki, wi, bg, cg, hc, gb = _project(x, g_pre, w_in)
    attn = _prompt_attention(q, qi, wi, k, v, ki)
    zeros = jnp.zeros((x.shape[0], CONV_WIDTH - 1, D_CONV), x.dtype)
    conv, conv_state = _short_conv(bg, cg, hc, zeros, w_conv)
    y = _merge(x, attn, ga, conv, gb, w_out, g_post)
    return y, k, v, ki, conv_state


def _sample_layer(x, past_k, past_v, past_ik, prev_conv, g_pre, w_in, w_conv, w_out, g_post):
    q, k, v, ga, qi, ki, wi, bg, cg, hc, gb = _project(x, g_pre, w_in)
    past = past_k.shape[1]
    ts = x.shape[1]
    k_all = jnp.concatenate([past_k.astype(k.dtype), k], axis=1)
    v_all = jnp.concatenate([past_v.astype(v.dtype), v], axis=1)
    ik_all = jnp.concatenate([past_ik.astype(ki.dtype), ki], axis=1)
    qpos = past + jnp.arange(ts, dtype=jnp.int32)
    topk = min(TOPK_MAX, (past + ts) // 4)
    attn = _sparse_attend(q, qi, wi, qpos, k_all, v_all, ik_all, topk)
    conv, conv_state = _short_conv(bg, cg, hc, prev_conv, w_conv)
    y = _merge(x, attn, ga, conv, gb, w_out, g_post)
    return y, k, v, ki, conv_state


def setup_inputs(seed: int = 0) -> dict:
    key = jax.random.key(seed)
    ks = jax.random.split(key, 12)
    f32 = jnp.float32
    return {
        "x_prompt": jax.random.normal(ks[0], (BATCH, SEQ, D_MODEL), f32),
        "x_sample": jax.random.normal(ks[1], (DEC_BATCH, DEC_SEQ, D_MODEL), f32),
        "cache_k": jax.random.normal(ks[2], (DEPTH, DEC_BATCH, PAST_LEN, N_KV_HEADS, HEAD_DIM), f32),
        "cache_v": jax.random.normal(ks[3], (DEPTH, DEC_BATCH, PAST_LEN, N_KV_HEADS, HEAD_DIM), f32),
        "cache_idx_k": jax.random.normal(ks[4], (DEPTH, DEC_BATCH, PAST_LEN, IDX_DIM), f32),
        "state_conv": jax.random.normal(ks[5], (DEPTH, DEC_BATCH, CONV_WIDTH - 1, D_CONV), f32),
        "g_pre": 1.0 + 0.05 * jax.random.normal(ks[6], (DEPTH, D_MODEL), f32),
        "w_in": jax.random.normal(ks[7], (DEPTH, D_MODEL, D_PROJ), f32) * D_MODEL ** -0.5,
        "w_conv": jax.random.normal(ks[8], (DEPTH, CONV_WIDTH, D_CONV), f32) * CONV_WIDTH ** -0.5,
        "w_out": jax.random.normal(ks[9], (DEPTH, D_MODEL, D_MODEL), f32) * D_MODEL ** -0.5,
        "g_post": 1.0 + 0.05 * jax.random.normal(ks[10], (DEPTH, D_MODEL), f32),
    }


def reference(x_prompt, x_sample, cache_k, cache_v, cache_idx_k, state_conv, g_pre, w_in, w_conv, w_out, g_post):
    xp, xs = x_prompt, x_sample
    kp_l, vp_l, ikp_l, cp_l = [], [], [], []
    ks_l, vs_l, iks_l, cs_l = [], [], [], []
    for l in range(DEPTH):
        xp, kp, vp, ikp, cp = _prompt_layer(xp, g_pre[l], w_in[l], w_conv[l], w_out[l], g_post[l])
        xs, ksm, vsm, iks, csm = _sample_layer(xs, cache_k[l], cache_v[l], cache_idx_k[l], state_conv[l],
                                               g_pre[l], w_in[l], w_conv[l], w_out[l], g_post[l])
        kp_l.append(kp); vp_l.append(vp); ikp_l.append(ikp); cp_l.append(cp)
        ks_l.append(ksm); vs_l.append(vsm); iks_l.append(iks); cs_l.append(csm)
    new_k_prompt = jnp.stack(kp_l)
    new_v_prompt = jnp.stack(vp_l)
    new_ik_prompt = jnp.stack(ikp_l)
    conv_prompt = jnp.stack(cp_l)
    new_k_sample = jnp.stack(ks_l)
    new_v_sample = jnp.stack(vs_l)
    new_ik_sample = jnp.stack(iks_l)
    conv_sample = jnp.stack(cs_l)
    return (xp, xs, new_k_prompt, new_v_prompt, new_ik_prompt, conv_prompt,
            new_k_sample, new_v_sample, new_ik_sample, conv_sample)
```

```python
import functools

import jax
import jax.numpy as jnp
from jax import lax
from jax.experimental import pallas as pl
from jax.experimental.pallas import tpu as pltpu

F32 = jnp.float32
BF16 = jnp.bfloat16
I32 = jnp.int32

D_MODEL = 2048
D_ATTN = 1024
D_CONV = 1024
HEAD_DIM = 128
N_KV_HEADS = 2
KV_GROUP = 4
KV_DIM = N_KV_HEADS * HEAD_DIM
N_IDX_HEADS = 16
IDX_DIM = 64
TOPK_MAX = 256
CHUNK = 64
CONV_WIDTH = 3
RMS_EPS = 1e-6

LANES = 128
SUBLANES = 8
VMEM_LIMIT_BYTES = 60 * 1000 * 1024

OFF_Q = 0
OFF_K = OFF_Q + D_ATTN
OFF_V = OFF_K + KV_DIM
OFF_GA = OFF_V + KV_DIM
OFF_QI = OFF_GA + D_ATTN
OFF_KW = OFF_QI + N_IDX_HEADS * IDX_DIM
OFF_B = OFF_KW + LANES
OFF_C = OFF_B + D_CONV
OFF_HC = OFF_C + D_CONV
OFF_GB = OFF_HC + D_CONV
D_PACK = OFF_GB + D_CONV
N_RAW_KW = IDX_DIM + N_IDX_HEADS

CONV_COLS = 256
PAD_ROWS = SUBLANES
HEADS_PER_QUAD = 4
QUAD_LANES = HEADS_PER_QUAD * IDX_DIM
N_QUADS = N_IDX_HEADS // HEADS_PER_QUAD
IDX_SCALE = (IDX_DIM ** -0.5) * (N_IDX_HEADS ** -0.5)
ATT_SCALE = HEAD_DIM ** -0.5
INT_MIN = -(2 ** 31)
INT_MAX = 2 ** 31 - 1
KEY_BITS = 32
NEG_INF = float("-inf")


def _silu(x):
    return x * jax.nn.sigmoid(x)


def _rmsnorm(x, g):
    ms = jnp.mean(x * x, axis=-1, keepdims=True)
    return x * lax.rsqrt(ms + RMS_EPS) * g


def _project_rows(h, w_ref, q_ref, k_ref, v_ref, kb_ref, vb_ref, ga_ref, qi_ref, ik_ref, aux_ref,
                  kpad_ref):
    def mm(c0, n):
        return jnp.dot(h, w_ref[:, c0:c0 + n], preferred_element_type=F32)

    q_ref[...] = mm(OFF_Q, D_ATTN).astype(BF16)
    kk = mm(OFF_K, KV_DIM)
    k_ref[...] = kk
    kb_ref[...] = kk.astype(BF16)
    vv = mm(OFF_V, KV_DIM)
    v_ref[...] = vv
    vb_ref[...] = vv.astype(BF16)
    ga_ref[...] = mm(OFF_GA, D_ATTN)
    qi_ref[...] = mm(OFF_QI, N_IDX_HEADS * IDX_DIM).astype(BF16)
    kw = mm(OFF_KW, LANES)
    aux_ref[...] = kw
    ik_ref[...] = kw[:, :IDX_DIM]
    lane = lax.broadcasted_iota(I32, kw.shape, 1)
    lo = jnp.where(lane < IDX_DIM, kw, 0.0)
    hi = pltpu.roll(lo, IDX_DIM, axis=1)
    zero = jnp.zeros_like(lo)
    blocks = (lo, zero, hi, zero, zero, lo, zero, hi)
    for i, blk in enumerate(blocks):
        kpad_ref[:, i * LANES:(i + 1) * LANES] = blk.astype(BF16)


def _conv_chunk(h, w_ref, wc_ref, c):
    def mm(off):
        return jnp.dot(h, w_ref[:, off + c:off + c + CONV_COLS], preferred_element_type=F32)

    return mm(OFF_B), mm(OFF_C) * mm(OFF_HC), mm(OFF_GB)


def _conv_out(bg, gb, u, um1, um2, wc_ref, c):
    w0 = wc_ref[0:1, c:c + CONV_COLS]
    w1 = wc_ref[1:2, c:c + CONV_COLS]
    w2 = wc_ref[2:3, c:c + CONV_COLS]
    conv = w0 * um2 + w1 * um1 + w2 * u
    return (bg * conv * _silu(gb)).astype(BF16)


def _project_prompt_kernel(x_ref, g_ref, w_ref, wc_ref, q_ref, k_ref, v_ref, kb_ref, vb_ref, ga_ref,
                           qi_ref, ik_ref, aux_ref, kpad_ref, zc_ref, cs_ref, upad_ref, *, tm):
    @pl.when(pl.program_id(1) == 0)
    def _():
        upad_ref[0:PAD_ROWS, :] = jnp.zeros((PAD_ROWS, D_CONV), F32)

    h = _rmsnorm(x_ref[...], g_ref[...]).astype(BF16)
    _project_rows(h, w_ref, q_ref, k_ref, v_ref, kb_ref, vb_ref, ga_ref, qi_ref, ik_ref, aux_ref,
                  kpad_ref)
    for c in range(0, D_CONV, CONV_COLS):
        bg, u, gb = _conv_chunk(h, w_ref, wc_ref, c)
        upad_ref[PAD_ROWS:PAD_ROWS + tm, c:c + CONV_COLS] = u
        um1 = upad_ref[PAD_ROWS - 1:PAD_ROWS - 1 + tm, c:c + CONV_COLS]
        um2 = upad_ref[PAD_ROWS - 2:PAD_ROWS - 2 + tm, c:c + CONV_COLS]
        zc_ref[:, c:c + CONV_COLS] = _conv_out(bg, gb, u, um1, um2, wc_ref, c)
    last = upad_ref[PAD_ROWS + tm - (CONV_WIDTH - 1):PAD_ROWS + tm, :]
    cs_ref[...] = last
    upad_ref[PAD_ROWS - (CONV_WIDTH - 1):PAD_ROWS, :] = last


def _project_sample_kernel(x_ref, g_ref, w_ref, wc_ref, st_ref, q_ref, k_ref, v_ref, kb_ref, vb_ref,
                           ga_ref, qi_ref, ik_ref, aux_ref, kpad_ref, zc_ref, cs_ref, upad_ref, *,
                           nseq, seqlen):
    rows = nseq * seqlen
    upad_ref[:, PAD_ROWS - (CONV_WIDTH - 1):PAD_ROWS, :] = st_ref[...]
    h = _rmsnorm(x_ref[...], g_ref[...]).astype(BF16)
    _project_rows(h, w_ref, q_ref, k_ref, v_ref, kb_ref, vb_ref, ga_ref, qi_ref, ik_ref, aux_ref,
                  kpad_ref)
    for c in range(0, D_CONV, CONV_COLS):
        bg, u, gb = _conv_chunk(h, w_ref, wc_ref, c)
        upad_ref[:, PAD_ROWS:PAD_ROWS + seqlen, c:c + CONV_COLS] = u.reshape(nseq, seqlen, CONV_COLS)
        um1 = upad_ref[:, PAD_ROWS - 1:PAD_ROWS - 1 + seqlen, c:c + CONV_COLS].reshape(rows, CONV_COLS)
        um2 = upad_ref[:, PAD_ROWS - 2:PAD_ROWS - 2 + seqlen, c:c + CONV_COLS].reshape(rows, CONV_COLS)
        zc_ref[:, c:c + CONV_COLS] = _conv_out(bg, gb, u, um1, um2, wc_ref, c)
    cs_ref[...] = upad_ref[:, PAD_ROWS + seqlen - (CONV_WIDTH - 1):PAD_ROWS + seqlen, :]


def _project_out_shapes(lead, rows):
    def s(n, dt):
        return jax.ShapeDtypeStruct(lead + (rows, n), dt)

    return [s(D_ATTN, BF16), s(KV_DIM, F32), s(KV_DIM, F32), s(KV_DIM, BF16), s(KV_DIM, BF16),
            s(D_ATTN, F32), s(N_IDX_HEADS * IDX_DIM, BF16), s(IDX_DIM, F32), s(LANES, F32),
            s(N_QUADS * QUAD_LANES, BF16), s(D_CONV, BF16)]


_PROJECT_WIDTHS = (D_ATTN, KV_DIM, KV_DIM, KV_DIM, KV_DIM, D_ATTN, N_IDX_HEADS * IDX_DIM, IDX_DIM,
                   LANES, N_QUADS * QUAD_LANES, D_CONV)


def _resident(shape, ngrid):
    zeros = (0,) * len(shape)
    if ngrid == 1:
        return pl.BlockSpec(shape, lambda i: zeros, pipeline_mode=pl.Buffered(1))
    return pl.BlockSpec(shape, lambda b, i: zeros, pipeline_mode=pl.Buffered(1))


def _project_prompt(x, g_pre, w_pack, w_conv, tm):
    nb, t, _ = x.shape
    grid = (nb, t // tm)
    row_spec = lambda n: pl.BlockSpec((None, tm, n), lambda b, i: (b, i, 0))
    out_shapes = _project_out_shapes((nb,), t) + [jax.ShapeDtypeStruct((nb, CONV_WIDTH - 1, D_CONV), F32)]
    out_specs = [row_spec(n) for n in _PROJECT_WIDTHS]
    out_specs.append(pl.BlockSpec((None, CONV_WIDTH - 1, D_CONV), lambda b, i: (b, 0, 0)))
    return pl.pallas_call(
        functools.partial(_project_prompt_kernel, tm=tm),
        grid=grid,
        in_specs=[row_spec(D_MODEL), _resident((1, D_MODEL), 2), _resident((D_MODEL, D_PACK), 2),
                  _resident((CONV_WIDTH, D_CONV), 2)],
        out_specs=out_specs,
        out_shape=out_shapes,
        scratch_shapes=[pltpu.VMEM((PAD_ROWS + tm, D_CONV), F32)],
        compiler_params=pltpu.CompilerParams(dimension_semantics=("arbitrary", "arbitrary"),
                                             vmem_limit_bytes=VMEM_LIMIT_BYTES),
        name="project_prompt",
    )(x, g_pre, w_pack, w_conv)


def _project_sample(x, g_pre, w_pack, w_conv, state):
    nseq, seqlen, _ = x.shape
    rows = nseq * seqlen
    full = lambda shape: pl.BlockSpec(shape, lambda i: (0,) * len(shape))
    out_shapes = _project_out_shapes((), rows) + [jax.ShapeDtypeStruct((nseq, CONV_WIDTH - 1, D_CONV), F32)]
    out_specs = [full((rows, n)) for n in _PROJECT_WIDTHS] + [full((nseq, CONV_WIDTH - 1, D_CONV))]
    return pl.pallas_call(
        functools.partial(_project_sample_kernel, nseq=nseq, seqlen=seqlen),
        grid=(1,),
        in_specs=[full((rows, D_MODEL)), full((1, D_MODEL)), _resident((D_MODEL, D_PACK), 1),
                  full((CONV_WIDTH, D_CONV)), full((nseq, CONV_WIDTH - 1, D_CONV))],
        out_specs=out_specs,
        out_shape=out_shapes,
        scratch_shapes=[pltpu.VMEM((nseq, PAD_ROWS + seqlen, D_CONV), F32)],
        compiler_params=pltpu.CompilerParams(dimension_semantics=("arbitrary",),
                                             vmem_limit_bytes=VMEM_LIMIT_BYTES),
        name="project_sample",
    )(x.reshape(rows, D_MODEL), g_pre, w_pack, w_conv, state)


def _sortable_key(score):
    bits = lax.bitcast_convert_type(score, I32)
    return bits ^ ((bits >> 31) & INT_MAX)


def _attend(qi, aux, q, ga, kpad, kk, vv, z_ref, keys_scr, wb_scr, thr_scr, m_scr, l_scr, acc_scr, *,
            tq, tk, n_tiles, adm_fn, search_pred, n_cols):
    halves = tk // LANES
    qstack = jnp.concatenate([qi[:, u * QUAD_LANES:(u + 1) * QUAD_LANES] for u in range(N_QUADS)], axis=0)
    for h in range(N_IDX_HEADS):
        wb_scr[h] = jnp.broadcast_to(aux[:, IDX_DIM + h:IDX_DIM + h + 1], (tq, LANES))

    def idx_tile(kt, carry):
        start = pl.multiple_of(kt * tk, tk)
        acc = [jnp.zeros((tq, LANES), F32) for _ in range(halves)]
        for c in range(HEADS_PER_QUAD):
            kp = kpad[pl.ds(start, tk), c * QUAD_LANES:(c + 1) * QUAD_LANES]
            s = lax.dot_general(qstack, kp, (((1,), (1,)), ((), ())), preferred_element_type=F32)
            for u in range(N_QUADS):
                wb = wb_scr[HEADS_PER_QUAD * u + c]
                for hf in range(halves):
                    blk = s[u * tq:(u + 1) * tq, hf * LANES:(hf + 1) * LANES]
                    acc[hf] = acc[hf] + jnp.maximum(blk, 0.0) * wb
        key = _sortable_key(jnp.concatenate(acc, axis=1) * IDX_SCALE)
        adm = adm_fn(kt)
        if adm is not None:
            key = jnp.where(adm, key, INT_MIN)
        keys_scr[kt] = key
        return carry

    lax.fori_loop(0, n_tiles, idx_tile, 0)

    def count(pred_fn):
        def body(kt, c):
            p = pred_fn(kt, keys_scr[kt]).astype(I32)
            for hf in range(halves):
                c = c + p[:, hf * LANES:(hf + 1) * LANES]
            return c

        c = lax.fori_loop(0, n_tiles, body, jnp.zeros((tq, LANES), I32))
        return jnp.sum(c, axis=1, keepdims=True)

    thr_scr[...] = jnp.full((tq, LANES), INT_MIN + 1, I32)

    @pl.when(search_pred)
    def _():
        def bisect(_, carry):
            lo, hi = carry
            mid = (lo >> 1) + (hi >> 1) + ((lo | hi) & 1)
            ge = count(lambda kt, t: t >= mid) >= TOPK_MAX
            return jnp.where(ge, mid, lo), jnp.where(ge, hi, mid)

        lo0 = jnp.full((tq, 1), INT_MIN, I32)
        hi0 = jnp.full((tq, 1), INT_MAX, I32)
        lo, _ = lax.fori_loop(0, KEY_BITS, bisect, (lo0, hi0))
        thr_scr[...] = jnp.broadcast_to(lo, (tq, LANES))
        excess = count(lambda kt, t: t >= lo) - TOPK_MAX

        @pl.when(jnp.max(excess.astype(F32)) > 0.0)
        def _():
            need = TOPK_MAX - count(lambda kt, t: t > lo)

            def col(kt):
                return kt * tk + lax.broadcasted_iota(I32, (tq, tk), 1)

            def pos_bisect(_, carry):
                plo, phi = carry
                mid = (plo + phi) >> 1
                ok = count(lambda kt, t: (t == lo) & (col(kt) <= mid)) >= need
                return jnp.where(ok, plo, mid), jnp.where(ok, mid, phi)

            plo0 = jnp.full((tq, 1), -1, I32)
            phi0 = jnp.full((tq, 1), n_cols - 1, I32)
            steps = max(1, (n_cols - 1).bit_length()) + 1
            _, pos = lax.fori_loop(0, steps, pos_bisect, (plo0, phi0))

            def drop(kt, carry):
                t = keys_scr[kt]
                keys_scr[kt] = jnp.where((t == lo) & (col(kt) > pos), INT_MIN, t)
                return carry

            lax.fori_loop(0, n_tiles, drop, 0)

    thr = thr_scr[:, 0:1]

    rows = KV_GROUP * tq
    qn = [jnp.concatenate([q[:, (KV_GROUP * n + g) * HEAD_DIM:(KV_GROUP * n + g + 1) * HEAD_DIM]
                           for g in range(KV_GROUP)], axis=0) for n in range(N_KV_HEADS)]
    m_scr[...] = jnp.full(m_scr.shape, NEG_INF, F32)
    l_scr[...] = jnp.zeros(l_scr.shape, F32)
    acc_scr[...] = jnp.zeros(acc_scr.shape, F32)

    def att_tile(kt, carry):
        start = pl.multiple_of(kt * tk, tk)
        sel = keys_scr[kt] >= thr
        for n in range(N_KV_HEADS):
            kt_n = kk[pl.ds(start, tk), n * HEAD_DIM:(n + 1) * HEAD_DIM]
            lg = lax.dot_general(qn[n], kt_n, (((1,), (1,)), ((), ())), preferred_element_type=F32)
            lg = lg * ATT_SCALE
            lg = jnp.concatenate([jnp.where(sel, lg[g * tq:(g + 1) * tq], NEG_INF)
                                  for g in range(KV_GROUP)], axis=0)
            m_prev = m_scr[n][:, 0:1]
            m_new = jnp.maximum(m_prev, jnp.max(lg, axis=1, keepdims=True))
            m_safe = jnp.where(m_new == NEG_INF, 0.0, m_new)
            alpha = jnp.exp(m_prev - m_safe)
            p = jnp.exp(lg - m_safe)
            l_new = alpha * l_scr[n][:, 0:1] + jnp.sum(p, axis=1, keepdims=True)
            vt_n = vv[pl.ds(start, tk), n * HEAD_DIM:(n + 1) * HEAD_DIM]
            pv = jnp.dot(p.astype(BF16), vt_n, preferred_element_type=F32)
            acc_scr[n] = acc_scr[n] * alpha + pv
            m_scr[n] = jnp.broadcast_to(m_new, (rows, LANES))
            l_scr[n] = jnp.broadcast_to(l_new, (rows, LANES))
        return carry

    lax.fori_loop(0, n_tiles, att_tile, 0)

    outs = []
    for n in range(N_KV_HEADS):
        o = acc_scr[n] / l_scr[n]
        outs.extend(o[g * tq:(g + 1) * tq] for g in range(KV_GROUP))
    z_ref[...] = (jnp.concatenate(outs, axis=1) * _silu(ga)).astype(BF16)


def _attend_prompt_kernel(qi_ref, aux_ref, q_ref, ga_ref, kpad_ref, k_ref, v_ref, z_ref, keys_scr,
                          wb_scr, thr_scr, m_scr, l_scr, acc_scr, *, tq, tk, n_cols):
    j = pl.program_id(1)
    chunks = tq // CHUNK

    def adm_fn(kt):
        r = lax.broadcasted_iota(I32, (tq, tk), 0) // CHUNK
        c = lax.broadcasted_iota(I32, (tq, tk), 1) // CHUNK
        return (c <= r) | (kt != j)

    assert tq == tk and chunks * CHUNK >= TOPK_MAX
    _attend(qi_ref[...], aux_ref[...], q_ref[...], ga_ref[...], kpad_ref, k_ref, v_ref, z_ref,
            keys_scr, wb_scr, thr_scr, m_scr, l_scr, acc_scr, tq=tq, tk=tk, n_tiles=j + 1,
            adm_fn=adm_fn, search_pred=j >= 1, n_cols=n_cols)


def _attend_sample_kernel(qi_ref, aux_ref, q_ref, ga_ref, cik_ref, ck_ref, cv_ref, nk_ref, nv_ref,
                          z_ref, kpad_s, k_s, v_s, keys_scr, wb_scr, thr_scr, m_scr, l_scr, acc_scr,
                          *, tq, tk, past, n_tiles):
    r = lax.broadcasted_iota(I32, (LANES, N_QUADS * QUAD_LANES), 0)
    n = lax.broadcasted_iota(I32, (LANES, N_QUADS * QUAD_LANES), 1)
    place = ((r < IDX_DIM) & ((n % QUAD_LANES) == (n // QUAD_LANES) * IDX_DIM + r)).astype(BF16)
    aux = aux_ref[...]
    kpad_s[0:past, :] = jnp.dot(cik_ref[...].astype(BF16), place[:IDX_DIM],
                                preferred_element_type=F32).astype(BF16)
    kpad_s[past:past + tq, :] = jnp.dot(aux.astype(BF16), place, preferred_element_type=F32).astype(BF16)
    k_s[0:past, :] = ck_ref[...].astype(BF16)
    k_s[past:past + tq, :] = nk_ref[...].astype(BF16)
    v_s[0:past, :] = cv_ref[...].astype(BF16)
    v_s[past:past + tq, :] = nv_ref[...].astype(BF16)
    tail = n_tiles * tk - (past + tq)
    for ref in (kpad_s, k_s, v_s):
        ref[past + tq:, :] = jnp.zeros((tail, ref.shape[1]), BF16)

    n_keys = past + tq
    assert n_keys >= TOPK_MAX and past % CHUNK == 0 and tq <= CHUNK

    def adm_fn(kt):
        return kt * tk + lax.broadcasted_iota(I32, (tq, tk), 1) < n_keys

    _attend(qi_ref[...], aux, q_ref[...], ga_ref[...], kpad_s, k_s, v_s, z_ref, keys_scr, wb_scr,
            thr_scr, m_scr, l_scr, acc_scr, tq=tq, tk=tk, n_tiles=n_tiles, adm_fn=adm_fn,
            search_pred=True, n_cols=n_tiles * tk)


def _attend_scratch(tq, tk, n_tiles):
    rows = KV_GROUP * tq
    return [pltpu.VMEM((n_tiles, tq, tk), I32), pltpu.VMEM((N_IDX_HEADS, tq, LANES), F32),
            pltpu.VMEM((tq, LANES), I32), pltpu.VMEM((N_KV_HEADS, rows, LANES), F32),
            pltpu.VMEM((N_KV_HEADS, rows, LANES), F32), pltpu.VMEM((N_KV_HEADS, rows, HEAD_DIM), F32)]


def _attend_prompt(qi, aux, q, ga, kpad, kb, vb, tq):
    nb, t, _ = q.shape
    tk = tq
    blk = lambda n: pl.BlockSpec((None, tq, n), lambda b, j: (b, j, 0))
    seq = lambda n: pl.BlockSpec((None, t, n), lambda b, j: (b, 0, 0))
    return pl.pallas_call(
        functools.partial(_attend_prompt_kernel, tq=tq, tk=tk, n_cols=t),
        grid=(nb, t // tq),
        in_specs=[blk(N_IDX_HEADS * IDX_DIM), blk(LANES), blk(D_ATTN), blk(D_ATTN),
                  seq(N_QUADS * QUAD_LANES), seq(KV_DIM), seq(KV_DIM)],
        out_specs=blk(D_ATTN),
        out_shape=jax.ShapeDtypeStruct((nb, t, D_ATTN), BF16),
        scratch_shapes=_attend_scratch(tq, tk, t // tk),
        compiler_params=pltpu.CompilerParams(dimension_semantics=("arbitrary", "arbitrary"),
                                             vmem_limit_bytes=VMEM_LIMIT_BYTES),
        name="attend_prompt",
    )(qi, aux, q, ga, kpad, kb, vb)


def _attend_sample(qi, aux, q, ga, cache_ik, cache_k, cache_v, new_k, new_v, tk):
    nseq, tq, _ = q.shape
    past = cache_k.shape[1]
    n_tiles = -(-(past + tq) // tk)
    blk = lambda n: pl.BlockSpec((None, tq, n), lambda b: (b, 0, 0))
    cache = lambda n: pl.BlockSpec((None, past, n), lambda b: (b, 0, 0))
    scratch = [pltpu.VMEM((n_tiles * tk, N_QUADS * QUAD_LANES), BF16),
               pltpu.VMEM((n_tiles * tk, KV_DIM), BF16), pltpu.VMEM((n_tiles * tk, KV_DIM), BF16)]
    return pl.pallas_call(
        functools.partial(_attend_sample_kernel, tq=tq, tk=tk, past=past, n_tiles=n_tiles),
        grid=(nseq,),
        in_specs=[blk(N_IDX_HEADS * IDX_DIM), blk(LANES), blk(D_ATTN), blk(D_ATTN), cache(IDX_DIM),
                  cache(KV_DIM), cache(KV_DIM), blk(KV_DIM), blk(KV_DIM)],
        out_specs=blk(D_ATTN),
        out_shape=jax.ShapeDtypeStruct((nseq, tq, D_ATTN), BF16),
        scratch_shapes=scratch + _attend_scratch(tq, tk, n_tiles),
        compiler_params=pltpu.CompilerParams(dimension_semantics=("arbitrary",),
                                             vmem_limit_bytes=VMEM_LIMIT_BYTES),
        name="attend_sample",
    )(qi, aux, q, ga, cache_ik, cache_k, cache_v, new_k, new_v)


def _merge_kernel(x_ref, za_ref, zc_ref, w_ref, g_ref, y_ref):
    z = jnp.concatenate([za_ref[...], zc_ref[...]], axis=1)
    y = jnp.dot(z, w_ref[...], preferred_element_type=F32)
    y_ref[...] = x_ref[...] + _rmsnorm(y, g_ref[...])


def _merge(x, za, zc, w_out, g_post, tm):
    rows = x.shape[0]
    row_spec = lambda n: pl.BlockSpec((tm, n), lambda i: (i, 0))
    return pl.pallas_call(
        _merge_kernel,
        grid=(rows // tm,),
        in_specs=[row_spec(D_MODEL), row_spec(D_ATTN), row_spec(D_CONV), _resident((D_MODEL, D_MODEL), 1),
                  _resident((1, D_MODEL), 1)],
        out_specs=row_spec(D_MODEL),
        out_shape=jax.ShapeDtypeStruct((rows, D_MODEL), F32),
        compiler_params=pltpu.CompilerParams(dimension_semantics=("arbitrary",),
                                             vmem_limit_bytes=VMEM_LIMIT_BYTES),
        name="merge",
    )(x, za, zc, w_out, g_post)


PROJECT_ROWS = 256
ATTEND_ROWS = 256
MERGE_ROWS = 512


def _pack_w_in(w):
    pad = jnp.zeros((D_MODEL, LANES - N_RAW_KW), w.dtype)
    split = OFF_KW + N_RAW_KW
    return jnp.concatenate([w[:, :split], pad, w[:, split:]], axis=1).astype(BF16)


def _layer(xp, xs, cache_k, cache_v, cache_ik, state, g_pre, w_in, w_conv, w_out, g_post):
    nb, t, _ = xp.shape
    nseq, seqlen, _ = xs.shape
    g_pre = g_pre.reshape(1, D_MODEL)
    g_post = g_post.reshape(1, D_MODEL)
    w_pack = _pack_w_in(w_in)
    w_out = w_out.astype(BF16)

    q, k, v, kb, vb, ga, qi, ik, aux, kpad, zc, cs = _project_prompt(xp, g_pre, w_pack, w_conv, PROJECT_ROWS)
    za = _attend_prompt(qi, aux, q, ga, kpad, kb, vb, ATTEND_ROWS)
    yp = _merge(xp.reshape(nb * t, D_MODEL), za.reshape(nb * t, D_ATTN), zc.reshape(nb * t, D_CONV),
                w_out, g_post, MERGE_ROWS).reshape(nb, t, D_MODEL)

    sq, sk, sv, _, _, sga, sqi, sik, saux, _, szc, scs = _project_sample(xs, g_pre, w_pack, w_conv, state)
    per_seq = lambda a: a.reshape(nseq, seqlen, a.shape[-1])
    sza = _attend_sample(per_seq(sqi), per_seq(saux), per_seq(sq), per_seq(sga), cache_ik,
                         cache_k.reshape(nseq, -1, KV_DIM), cache_v.reshape(nseq, -1, KV_DIM),
                         per_seq(sk), per_seq(sv), ATTEND_ROWS)
    ys = _merge(xs.reshape(nseq * seqlen, D_MODEL), sza.reshape(nseq * seqlen, D_ATTN), szc, w_out,
                g_post, MERGE_ROWS).reshape(nseq, seqlen, D_MODEL)

    heads = lambda a, lead: a.reshape(lead + (N_KV_HEADS, HEAD_DIM))
    return (yp, ys, heads(k, (nb, t)), heads(v, (nb, t)), ik, cs,
            heads(sk, (nseq, seqlen)), heads(sv, (nseq, seqlen)), per_seq(sik), scs)


def kernel(x_prompt, x_sample, cache_k, cache_v, cache_idx_k, state_conv, g_pre, w_in, w_conv, w_out,
           g_post):
    depth = g_pre.shape[0]
    xp, xs = x_prompt, x_sample
    outs = []
    for l in range(depth):
        res = _layer(xp, xs, cache_k[l], cache_v[l], cache_idx_k[l], state_conv[l], g_pre[l], w_in[l],
                     w_conv[l], w_out[l], g_post[l])
        xp, xs = res[0], res[1]
        outs.append(res[2:])
    stacked = [jnp.stack([o[i] for o in outs]) for i in range(8)]
    return (xp, xs) + tuple(stacked)
```

```python
import functools

import jax
import jax.numpy as jnp
from jax import lax
from jax.experimental import pallas as pl
from jax.experimental.pallas import tpu as pltpu

F32 = jnp.float32
BF16 = jnp.bfloat16
I32 = jnp.int32

D_MODEL = 2048
D_ATTN = 1024
D_CONV = 1024
HEAD_DIM = 128
N_KV_HEADS = 2
KV_GROUP = 4
KV_DIM = N_KV_HEADS * HEAD_DIM
N_IDX_HEADS = 16
IDX_DIM = 64
TOPK_MAX = 256
CHUNK = 64
CONV_WIDTH = 3
RMS_EPS = 1e-6

LANES = 128
SUBLANES = 8
VMEM_LIMIT_BYTES = 60 * 1000 * 1024

OFF_Q = 0
OFF_K = OFF_Q + D_ATTN
OFF_V = OFF_K + KV_DIM
OFF_GA = OFF_V + KV_DIM
OFF_QI = OFF_GA + D_ATTN
OFF_KW = OFF_QI + N_IDX_HEADS * IDX_DIM
OFF_B = OFF_KW + IDX_DIM + N_IDX_HEADS
OFF_C = OFF_B + D_CONV
OFF_HC = OFF_C + D_CONV
OFF_GB = OFF_HC + D_CONV
D_PROJ = OFF_GB + D_CONV
BF16_ROWS = 16
assert all(o % BF16_ROWS == 0 for o in (OFF_K, OFF_V, OFF_GA, OFF_QI, OFF_KW, OFF_B, OFF_C, OFF_HC, OFF_GB))

CONV_COLS = 256
PAD_ROWS = SUBLANES
HEADS_PER_QUAD = 4
QUAD_LANES = HEADS_PER_QUAD * IDX_DIM
N_QUADS = N_IDX_HEADS // HEADS_PER_QUAD
IDX_SCALE = (IDX_DIM ** -0.5) * (N_IDX_HEADS ** -0.5)
ATT_SCALE = HEAD_DIM ** -0.5
INT_MIN = -(2 ** 31)
INT_MAX = 2 ** 31 - 1
KEY_BITS = 32
NEG_INF = float("-inf")


def _silu(x):
    return x * jax.nn.sigmoid(x)


def _dot_nt(a, b):
    return lax.dot_general(a, b, (((1,), (1,)), ((), ())), preferred_element_type=F32)


def _rmsnorm(x, g):
    ms = jnp.mean(x * x, axis=-1, keepdims=True)
    return x * lax.rsqrt(ms + RMS_EPS) * g


def _project_rows(h, w_ref, q_ref, k_ref, v_ref, ga_ref, qi_ref, ik_ref, aux_ref, kpad_ref,
                  kb_ref=None, vt_ref=None):
    rows = h.shape[0]

    def mm(r0, n):
        return _dot_nt(h, w_ref[r0:r0 + n, :])

    q_ref[...] = mm(OFF_Q, D_ATTN).astype(BF16)
    kk = mm(OFF_K, KV_DIM)
    vv = mm(OFF_V, KV_DIM)
    for n in range(N_KV_HEADS):
        k_ref[pl.ds(n, rows, stride=N_KV_HEADS), :] = kk[:, n * HEAD_DIM:(n + 1) * HEAD_DIM]
        v_ref[pl.ds(n, rows, stride=N_KV_HEADS), :] = vv[:, n * HEAD_DIM:(n + 1) * HEAD_DIM]
    if kb_ref is not None:
        kb_ref[...] = kk.astype(BF16)
        vt_ref[...] = vv.T.astype(BF16)
    ga_ref[...] = mm(OFF_GA, D_ATTN)
    qi_ref[...] = mm(OFF_QI, N_IDX_HEADS * IDX_DIM).astype(BF16)
    kw = mm(OFF_KW, LANES)
    aux_ref[...] = kw
    ik_ref[...] = kw[:, :IDX_DIM]
    lane = lax.broadcasted_iota(I32, kw.shape, 1)
    lo = jnp.where(lane < IDX_DIM, kw, 0.0)
    hi = pltpu.roll(lo, IDX_DIM, axis=1)
    zero = jnp.zeros_like(lo)
    blocks = (lo, zero, hi, zero, zero, lo, zero, hi)
    for i, blk in enumerate(blocks):
        kpad_ref[:, i * LANES:(i + 1) * LANES] = blk.astype(BF16)


def _conv_chunk(h, w_ref, wc_ref, c):
    def mm(off):
        return _dot_nt(h, w_ref[off + c:off + c + CONV_COLS, :])

    return mm(OFF_B), mm(OFF_C) * mm(OFF_HC), mm(OFF_GB)


def _conv_out(bg, gb, u, um1, um2, wc_ref, c):
    w0 = wc_ref[0:1, c:c + CONV_COLS]
    w1 = wc_ref[1:2, c:c + CONV_COLS]
    w2 = wc_ref[2:3, c:c + CONV_COLS]
    conv = w0 * um2 + w1 * um1 + w2 * u
    return (bg * conv * _silu(gb)).astype(BF16)


def _project_prompt_kernel(x_ref, g_ref, w_ref, wc_ref, q_ref, k_ref, v_ref, ga_ref, qi_ref, ik_ref,
                           aux_ref, kpad_ref, zc_ref, kb_ref, vt_ref, cs_ref, upad_ref, *, tm):
    @pl.when(pl.program_id(1) == 0)
    def _():
        upad_ref[0:PAD_ROWS, :] = jnp.zeros((PAD_ROWS, D_CONV), F32)

    h = _rmsnorm(x_ref[...], g_ref[...]).astype(BF16)
    _project_rows(h, w_ref, q_ref, k_ref, v_ref, ga_ref, qi_ref, ik_ref, aux_ref, kpad_ref, kb_ref,
                  vt_ref)
    for c in range(0, D_CONV, CONV_COLS):
        bg, u, gb = _conv_chunk(h, w_ref, wc_ref, c)
        upad_ref[PAD_ROWS:PAD_ROWS + tm, c:c + CONV_COLS] = u
        um1 = upad_ref[PAD_ROWS - 1:PAD_ROWS - 1 + tm, c:c + CONV_COLS]
        um2 = upad_ref[PAD_ROWS - 2:PAD_ROWS - 2 + tm, c:c + CONV_COLS]
        zc_ref[:, c:c + CONV_COLS] = _conv_out(bg, gb, u, um1, um2, wc_ref, c)
    last = upad_ref[PAD_ROWS + tm - (CONV_WIDTH - 1):PAD_ROWS + tm, :]
    cs_ref[...] = last
    upad_ref[PAD_ROWS - (CONV_WIDTH - 1):PAD_ROWS, :] = last


def _project_sample_kernel(x_ref, g_ref, w_ref, wc_ref, st_ref, q_ref, k_ref, v_ref, ga_ref, qi_ref,
                           ik_ref, aux_ref, kpad_ref, zc_ref, cs_ref, upad_ref, *, nseq, seqlen):
    rows = nseq * seqlen
    upad_ref[:, PAD_ROWS - (CONV_WIDTH - 1):PAD_ROWS, :] = st_ref[...]
    h = _rmsnorm(x_ref[...], g_ref[...]).astype(BF16)
    _project_rows(h, w_ref, q_ref, k_ref, v_ref, ga_ref, qi_ref, ik_ref, aux_ref, kpad_ref)
    for c in range(0, D_CONV, CONV_COLS):
        bg, u, gb = _conv_chunk(h, w_ref, wc_ref, c)
        upad_ref[:, PAD_ROWS:PAD_ROWS + seqlen, c:c + CONV_COLS] = u.reshape(nseq, seqlen, CONV_COLS)
        um1 = upad_ref[:, PAD_ROWS - 1:PAD_ROWS - 1 + seqlen, c:c + CONV_COLS].reshape(rows, CONV_COLS)
        um2 = upad_ref[:, PAD_ROWS - 2:PAD_ROWS - 2 + seqlen, c:c + CONV_COLS].reshape(rows, CONV_COLS)
        zc_ref[:, c:c + CONV_COLS] = _conv_out(bg, gb, u, um1, um2, wc_ref, c)
    cs_ref[...] = upad_ref[:, PAD_ROWS + seqlen - (CONV_WIDTH - 1):PAD_ROWS + seqlen, :]


_PROJECT_OUTS = ((1, D_ATTN, BF16), (N_KV_HEADS, HEAD_DIM, F32), (N_KV_HEADS, HEAD_DIM, F32),
                 (1, D_ATTN, F32), (1, N_IDX_HEADS * IDX_DIM, BF16), (1, IDX_DIM, F32), (1, LANES, F32),
                 (1, N_QUADS * QUAD_LANES, BF16), (1, D_CONV, BF16))


def _project_out_shapes(lead, rows):
    return [jax.ShapeDtypeStruct(lead + (m * rows, n), dt) for m, n, dt in _PROJECT_OUTS]


def _resident(shape, ngrid):
    zeros = (0,) * len(shape)
    if ngrid == 1:
        return pl.BlockSpec(shape, lambda i: zeros, pipeline_mode=pl.Buffered(1))
    return pl.BlockSpec(shape, lambda b, i: zeros, pipeline_mode=pl.Buffered(1))


def _project_prompt(x, g_pre, w_pack, w_conv, tm):
    nb, t, _ = x.shape
    grid = (nb, t // tm)
    row_spec = lambda n, m=1: pl.BlockSpec((None, m * tm, n), lambda b, i: (b, i, 0))
    out_shapes = _project_out_shapes((nb,), t) + [
        jax.ShapeDtypeStruct((nb, t, KV_DIM), BF16),
        jax.ShapeDtypeStruct((nb, t // tm, KV_DIM, tm), BF16),
        jax.ShapeDtypeStruct((nb, CONV_WIDTH - 1, D_CONV), F32)]
    out_specs = [row_spec(n, m) for m, n, _ in _PROJECT_OUTS]
    out_specs.append(row_spec(KV_DIM))
    out_specs.append(pl.BlockSpec((None, None, KV_DIM, tm), lambda b, i: (b, i, 0, 0)))
    out_specs.append(pl.BlockSpec((None, CONV_WIDTH - 1, D_CONV), lambda b, i: (b, 0, 0)))
    return pl.pallas_call(
        functools.partial(_project_prompt_kernel, tm=tm),
        grid=grid,
        in_specs=[row_spec(D_MODEL), _resident((1, D_MODEL), 2), _resident((D_PROJ, D_MODEL), 2),
                  _resident((CONV_WIDTH, D_CONV), 2)],
        out_specs=out_specs,
        out_shape=out_shapes,
        scratch_shapes=[pltpu.VMEM((PAD_ROWS + tm, D_CONV), F32)],
        compiler_params=pltpu.CompilerParams(dimension_semantics=("arbitrary", "arbitrary"),
                                             vmem_limit_bytes=VMEM_LIMIT_BYTES),
        name="project_prompt",
    )(x, g_pre, w_pack, w_conv)


def _project_sample(x, g_pre, w_pack, w_conv, state):
    nseq, seqlen, _ = x.shape
    rows = nseq * seqlen
    full = lambda shape: pl.BlockSpec(shape, lambda i: (0,) * len(shape))
    out_shapes = _project_out_shapes((), rows) + [jax.ShapeDtypeStruct((nseq, CONV_WIDTH - 1, D_CONV), F32)]
    out_specs = [full((m * rows, n)) for m, n, _ in _PROJECT_OUTS] + [full((nseq, CONV_WIDTH - 1, D_CONV))]
    return pl.pallas_call(
        functools.partial(_project_sample_kernel, nseq=nseq, seqlen=seqlen),
        grid=(1,),
        in_specs=[full((rows, D_MODEL)), full((1, D_MODEL)), _resident((D_PROJ, D_MODEL), 1),
                  full((CONV_WIDTH, D_CONV)), full((nseq, CONV_WIDTH - 1, D_CONV))],
        out_specs=out_specs,
        out_shape=out_shapes,
        scratch_shapes=[pltpu.VMEM((nseq, PAD_ROWS + seqlen, D_CONV), F32)],
        compiler_params=pltpu.CompilerParams(dimension_semantics=("arbitrary",),
                                             vmem_limit_bytes=VMEM_LIMIT_BYTES),
        name="project_sample",
    )(x.reshape(rows, D_MODEL), g_pre, w_pack, w_conv, state)


def _sortable_key(score):
    bits = lax.bitcast_convert_type(score, I32)
    return bits ^ ((bits >> 31) & INT_MAX)


def _attend(qi, aux, q, ga, kpad, kk, vv, z_ref, keys_scr, wb_scr, thr_scr, m_scr, l_scr, acc_scr, *,
            tq, tk, n_tiles, adm_fn, search_pred, n_cols):
    halves = tk // LANES
    qstack = jnp.concatenate([qi[:, u * QUAD_LANES:(u + 1) * QUAD_LANES] for u in range(N_QUADS)], axis=0)
    for h in range(N_IDX_HEADS):
        wb_scr[h] = jnp.broadcast_to(aux[:, IDX_DIM + h:IDX_DIM + h + 1], (tq, LANES))

    def idx_tile(kt, carry):
        start = pl.multiple_of(kt * tk, tk)
        acc = [jnp.zeros((tq, LANES), F32) for _ in range(halves)]
        for c in range(HEADS_PER_QUAD):
            kp = kpad[pl.ds(start, tk), c * QUAD_LANES:(c + 1) * QUAD_LANES]
            s = lax.dot_general(qstack, kp, (((1,), (1,)), ((), ())), preferred_element_type=F32)
            for u in range(N_QUADS):
                wb = wb_scr[HEADS_PER_QUAD * u + c]
                for hf in range(halves):
                    blk = s[u * tq:(u + 1) * tq, hf * LANES:(hf + 1) * LANES]
                    acc[hf] = acc[hf] + jnp.maximum(blk, 0.0) * wb
        key = _sortable_key(jnp.concatenate(acc, axis=1) * IDX_SCALE)
        adm = adm_fn(kt)
        if adm is not None:
            key = jnp.where(adm, key, INT_MIN)
        keys_scr[kt] = key
        return carry

    lax.fori_loop(0, n_tiles, idx_tile, 0)

    def count(pred_fn):
        def body(kt, c):
            p = pred_fn(kt, keys_scr[kt]).astype(I32)
            for hf in range(halves):
                c = c + p[:, hf * LANES:(hf + 1) * LANES]
            return c

        c = lax.fori_loop(0, n_tiles, body, jnp.zeros((tq, LANES), I32))
        return jnp.sum(c, axis=1, keepdims=True)

    thr_scr[...] = jnp.full((tq, LANES), INT_MIN + 1, I32)

    @pl.when(search_pred)
    def _():
        def bisect(_, carry):
            lo, hi = carry
            mid = (lo >> 1) + (hi >> 1) + ((lo | hi) & 1)
            ge = count(lambda kt, t: t >= mid) >= TOPK_MAX
            return jnp.where(ge, mid, lo), jnp.where(ge, hi, mid)

        lo0 = jnp.full((tq, 1), INT_MIN, I32)
        hi0 = jnp.full((tq, 1), INT_MAX, I32)
        lo, _ = lax.fori_loop(0, KEY_BITS, bisect, (lo0, hi0))
        thr_scr[...] = jnp.broadcast_to(lo, (tq, LANES))
        excess = count(lambda kt, t: t >= lo) - TOPK_MAX

        @pl.when(jnp.max(excess.astype(F32)) > 0.0)
        def _():
            need = TOPK_MAX - count(lambda kt, t: t > lo)

            def col(kt):
                return kt * tk + lax.broadcasted_iota(I32, (tq, tk), 1)

            def pos_bisect(_, carry):
                plo, phi = carry
                mid = (plo + phi) >> 1
                ok = count(lambda kt, t: (t == lo) & (col(kt) <= mid)) >= need
                return jnp.where(ok, plo, mid), jnp.where(ok, mid, phi)

            plo0 = jnp.full((tq, 1), -1, I32)
            phi0 = jnp.full((tq, 1), n_cols - 1, I32)
            steps = max(1, (n_cols - 1).bit_length()) + 1
            _, pos = lax.fori_loop(0, steps, pos_bisect, (plo0, phi0))

            def drop(kt, carry):
                t = keys_scr[kt]
                keys_scr[kt] = jnp.where((t == lo) & (col(kt) > pos), INT_MIN, t)
                return carry

            lax.fori_loop(0, n_tiles, drop, 0)

    thr = thr_scr[:, 0:1]

    rows = KV_GROUP * tq
    qn = [jnp.concatenate([q[:, (KV_GROUP * n + g) * HEAD_DIM:(KV_GROUP * n + g + 1) * HEAD_DIM]
                           for g in range(KV_GROUP)], axis=0) for n in range(N_KV_HEADS)]
    m_scr[...] = jnp.full(m_scr.shape, NEG_INF, F32)
    l_scr[...] = jnp.zeros(l_scr.shape, F32)
    acc_scr[...] = jnp.zeros(acc_scr.shape, F32)

    def att_tile(kt, carry):
        start = pl.multiple_of(kt * tk, tk)
        sel = keys_scr[kt] >= thr
        for n in range(N_KV_HEADS):
            kt_n = kk[pl.ds(start, tk), n * HEAD_DIM:(n + 1) * HEAD_DIM]
            lg = lax.dot_general(qn[n], kt_n, (((1,), (1,)), ((), ())), preferred_element_type=F32)
            lg = lg * ATT_SCALE
            lg = jnp.concatenate([jnp.where(sel, lg[g * tq:(g + 1) * tq], NEG_INF)
                                  for g in range(KV_GROUP)], axis=0)
            m_prev = m_scr[n][:, 0:1]
            m_new = jnp.maximum(m_prev, jnp.max(lg, axis=1, keepdims=True))
            m_safe = jnp.where(m_new == NEG_INF, 0.0, m_new)
            alpha = jnp.exp(m_prev - m_safe)
            p = jnp.exp(lg - m_safe)
            l_new = alpha * l_scr[n][:, 0:1] + jnp.sum(p, axis=1, keepdims=True)
            vt_n = vv[pl.ds(start, tk), n * HEAD_DIM:(n + 1) * HEAD_DIM]
            pv = jnp.dot(p.astype(BF16), vt_n, preferred_element_type=F32)
            acc_scr[n] = acc_scr[n] * alpha + pv
            m_scr[n] = jnp.broadcast_to(m_new, (rows, LANES))
            l_scr[n] = jnp.broadcast_to(l_new, (rows, LANES))
        return carry

    lax.fori_loop(0, n_tiles, att_tile, 0)

    outs = []
    for n in range(N_KV_HEADS):
        o = acc_scr[n] / l_scr[n]
        outs.extend(o[g * tq:(g + 1) * tq] for g in range(KV_GROUP))
    z_ref[...] = (jnp.concatenate(outs, axis=1) * _silu(ga)).astype(BF16)


def _attend_prompt_kernel(qi_ref, aux_ref, q_ref, ga_ref, kpad_ref, k_ref, vt_ref, z_ref, keys_scr,
                          thr_scr, m_scr, l_scr, acc_scr, *, tq, tk, n_cols):
    j = pl.program_id(1)
    n_tiles = j + 1
    assert tq == tk == TOPK_MAX and tq % CHUNK == 0
    lanes_q = KV_GROUP * tq
    qi = qi_ref[...]
    qstack = jnp.concatenate([qi[:, u * QUAD_LANES:(u + 1) * QUAD_LANES] for u in range(N_QUADS)], axis=0)
    w_t = aux_ref[...].T
    w_rows = [w_t[IDX_DIM + h:IDX_DIM + h + 1, :] for h in range(N_IDX_HEADS)]
    key_chunk = lax.broadcasted_iota(I32, (tk, tq), 0) // CHUNK
    qry_chunk = lax.broadcasted_iota(I32, (tk, tq), 1) // CHUNK
    diag_adm = key_chunk <= qry_chunk

    def idx_tile(kt, carry):
        start = pl.multiple_of(kt * tk, tk)
        acc = jnp.zeros((tk, tq), F32)
        for c in range(HEADS_PER_QUAD):
            kp = kpad_ref[pl.ds(start, tk), c * QUAD_LANES:(c + 1) * QUAD_LANES]
            s = lax.dot_general(kp, qstack, (((1,), (1,)), ((), ())), preferred_element_type=F32)
            for u in range(N_QUADS):
                acc = acc + jnp.maximum(s[:, u * tq:(u + 1) * tq], 0.0) * w_rows[HEADS_PER_QUAD * u + c]
        key = _sortable_key(acc * IDX_SCALE)
        keys_scr[kt] = jnp.where(diag_adm | (kt != j), key, INT_MIN)
        return carry

    lax.fori_loop(0, n_tiles, idx_tile, 0)

    def count(pred_fn):
        def body(kt, c8):
            p = pred_fn(kt, keys_scr[kt]).astype(I32)
            for r in range(0, tk, SUBLANES):
                c8 = c8 + p[r:r + SUBLANES, :]
            return c8

        c8 = lax.fori_loop(0, n_tiles, body, jnp.zeros((SUBLANES, tq), I32))
        return jnp.sum(c8, axis=0, keepdims=True)

    thr_scr[...] = jnp.full((SUBLANES, tq), INT_MIN + 1, I32)

    @pl.when(j >= 1)
    def _():
        def bisect(_, carry):
            lo, hi = carry
            mid = (lo >> 1) + (hi >> 1) + ((lo | hi) & 1)
            ge = count(lambda kt, t: t >= mid) >= TOPK_MAX
            return jnp.where(ge, mid, lo), jnp.where(ge, hi, mid)

        lo0 = jnp.full((1, tq), INT_MIN, I32)
        hi0 = jnp.full((1, tq), INT_MAX, I32)
        lo, _ = lax.fori_loop(0, KEY_BITS, bisect, (lo0, hi0))
        thr_scr[...] = jnp.broadcast_to(lo, (SUBLANES, tq))
        excess = count(lambda kt, t: t >= lo) - TOPK_MAX

        @pl.when(jnp.max(excess.astype(F32)) > 0.0)
        def _():
            need = TOPK_MAX - count(lambda kt, t: t > lo)

            def pos_of(kt):
                return kt * tk + lax.broadcasted_iota(I32, (tk, tq), 0)

            def pos_bisect(_, carry):
                plo, phi = carry
                mid = (plo + phi) >> 1
                ok = count(lambda kt, t: (t == lo) & (pos_of(kt) <= mid)) >= need
                return jnp.where(ok, plo, mid), jnp.where(ok, mid, phi)

            plo0 = jnp.full((1, tq), -1, I32)
            phi0 = jnp.full((1, tq), n_cols - 1, I32)
            steps = max(1, (n_cols - 1).bit_length()) + 1
            _, pos = lax.fori_loop(0, steps, pos_bisect, (plo0, phi0))

            def drop(kt, carry):
                t = keys_scr[kt]
                keys_scr[kt] = jnp.where((t == lo) & (pos_of(kt) > pos), INT_MIN, t)
                return carry

            lax.fori_loop(0, n_tiles, drop, 0)

    thr = thr_scr[0:1, :]

    q = q_ref[...]
    qn = [jnp.concatenate([q[:, (KV_GROUP * n + g) * HEAD_DIM:(KV_GROUP * n + g + 1) * HEAD_DIM]
                           for g in range(KV_GROUP)], axis=0) for n in range(N_KV_HEADS)]
    m_scr[...] = jnp.full(m_scr.shape, NEG_INF, F32)
    l_scr[...] = jnp.zeros(l_scr.shape, F32)
    acc_scr[...] = jnp.zeros(acc_scr.shape, F32)

    def att_tile(kt, carry):
        start = pl.multiple_of(kt * tk, tk)
        sel = keys_scr[kt] >= thr
        for n in range(N_KV_HEADS):
            kt_n = k_ref[pl.ds(start, tk), n * HEAD_DIM:(n + 1) * HEAD_DIM]
            lg = lax.dot_general(kt_n, qn[n], (((1,), (1,)), ((), ())), preferred_element_type=F32)
            lg = lg * ATT_SCALE
            lg = jnp.concatenate([jnp.where(sel, lg[:, g * tq:(g + 1) * tq], NEG_INF)
                                  for g in range(KV_GROUP)], axis=1)
            m_prev = m_scr[n][0:1, :]
            m_new = jnp.maximum(m_prev, jnp.max(lg, axis=0, keepdims=True))
            m_safe = jnp.where(m_new == NEG_INF, 0.0, m_new)
            alpha = jnp.exp(m_prev - m_safe)
            p = jnp.exp(lg - m_safe)
            l_new = alpha * l_scr[n][0:1, :] + jnp.sum(p, axis=0, keepdims=True)
            vt_n = vt_ref[kt, n * HEAD_DIM:(n + 1) * HEAD_DIM, :]
            pv = jnp.dot(vt_n, p.astype(BF16), preferred_element_type=F32)
            acc_scr[n] = acc_scr[n] * alpha + pv
            m_scr[n] = jnp.broadcast_to(m_new, (SUBLANES, lanes_q))
            l_scr[n] = jnp.broadcast_to(l_new, (SUBLANES, lanes_q))
        return carry

    lax.fori_loop(0, n_tiles, att_tile, 0)

    outs = []
    for n in range(N_KV_HEADS):
        o = acc_scr[n] / l_scr[n][0:1, :]
        outs.extend(o[:, g * tq:(g + 1) * tq].T for g in range(KV_GROUP))
    z_ref[...] = (jnp.concatenate(outs, axis=1) * _silu(ga_ref[...])).astype(BF16)


def _attend_sample_kernel(qi_ref, aux_ref, q_ref, ga_ref, cik_ref, ck_ref, cv_ref, nk_ref, nv_ref,
                          z_ref, kpad_s, k_s, v_s, keys_scr, wb_scr, thr_scr, m_scr, l_scr, acc_scr,
                          *, tq, tk, past, n_tiles):
    r = lax.broadcasted_iota(I32, (LANES, N_QUADS * QUAD_LANES), 0)
    n = lax.broadcasted_iota(I32, (LANES, N_QUADS * QUAD_LANES), 1)
    place = ((r < IDX_DIM) & ((n % QUAD_LANES) == (n // QUAD_LANES) * IDX_DIM + r)).astype(BF16)
    aux = aux_ref[...]
    kpad_s[0:past, :] = jnp.dot(cik_ref[...].astype(BF16), place[:IDX_DIM],
                                preferred_element_type=F32).astype(BF16)
    kpad_s[past:past + tq, :] = jnp.dot(aux.astype(BF16), place, preferred_element_type=F32).astype(BF16)
    for n in range(N_KV_HEADS):
        cols = slice(n * HEAD_DIM, (n + 1) * HEAD_DIM)
        k_s[0:past, cols] = ck_ref[pl.ds(n, past, stride=N_KV_HEADS), :].astype(BF16)
        k_s[past:past + tq, cols] = nk_ref[pl.ds(n, tq, stride=N_KV_HEADS), :].astype(BF16)
        v_s[0:past, cols] = cv_ref[pl.ds(n, past, stride=N_KV_HEADS), :].astype(BF16)
        v_s[past:past + tq, cols] = nv_ref[pl.ds(n, tq, stride=N_KV_HEADS), :].astype(BF16)
    tail = n_tiles * tk - (past + tq)
    for ref in (kpad_s, k_s, v_s):
        ref[past + tq:, :] = jnp.zeros((tail, ref.shape[1]), BF16)

    n_keys = past + tq
    assert n_keys >= TOPK_MAX and past % CHUNK == 0 and tq <= CHUNK

    def adm_fn(kt):
        return kt * tk + lax.broadcasted_iota(I32, (tq, tk), 1) < n_keys

    _attend(qi_ref[...], aux, q_ref[...], ga_ref[...], kpad_s, k_s, v_s, z_ref, keys_scr, wb_scr,
            thr_scr, m_scr, l_scr, acc_scr, tq=tq, tk=tk, n_tiles=n_tiles, adm_fn=adm_fn,
            search_pred=True, n_cols=n_tiles * tk)


def _attend_scratch(tq, tk, n_tiles):
    rows = KV_GROUP * tq
    return [pltpu.VMEM((n_tiles, tq, tk), I32), pltpu.VMEM((N_IDX_HEADS, tq, LANES), F32),
            pltpu.VMEM((tq, LANES), I32), pltpu.VMEM((N_KV_HEADS, rows, LANES), F32),
            pltpu.VMEM((N_KV_HEADS, rows, LANES), F32), pltpu.VMEM((N_KV_HEADS, rows, HEAD_DIM), F32)]


def _attend_prompt(qi, aux, q, ga, kpad, kb, vt, tq):
    nb, t, _ = q.shape
    tk = tq
    n_tiles = t // tk
    lanes_q = KV_GROUP * tq
    blk = lambda n: pl.BlockSpec((None, tq, n), lambda b, j: (b, j, 0))
    seq = lambda n: pl.BlockSpec((None, t, n), lambda b, j: (b, 0, 0))
    scratch = [pltpu.VMEM((n_tiles, tk, tq), I32), pltpu.VMEM((SUBLANES, tq), I32),
               pltpu.VMEM((N_KV_HEADS, SUBLANES, lanes_q), F32),
               pltpu.VMEM((N_KV_HEADS, SUBLANES, lanes_q), F32),
               pltpu.VMEM((N_KV_HEADS, HEAD_DIM, lanes_q), F32)]
    return pl.pallas_call(
        functools.partial(_attend_prompt_kernel, tq=tq, tk=tk, n_cols=t),
        grid=(nb, t // tq),
        in_specs=[blk(N_IDX_HEADS * IDX_DIM), blk(LANES), blk(D_ATTN), blk(D_ATTN),
                  seq(N_QUADS * QUAD_LANES), seq(KV_DIM),
                  pl.BlockSpec((None, n_tiles, KV_DIM, tk), lambda b, j: (b, 0, 0, 0))],
        out_specs=blk(D_ATTN),
        out_shape=jax.ShapeDtypeStruct((nb, t, D_ATTN), BF16),
        scratch_shapes=scratch,
        compiler_params=pltpu.CompilerParams(dimension_semantics=("arbitrary", "arbitrary"),
                                             vmem_limit_bytes=VMEM_LIMIT_BYTES),
        name="attend_prompt",
    )(qi, aux, q, ga, kpad, kb, vt)


def _attend_sample(qi, aux, q, ga, cache_ik, cache_k, cache_v, new_k, new_v, tk):
    nseq, tq, _ = q.shape
    past = cache_ik.shape[1]
    n_tiles = -(-(past + tq) // tk)
    blk = lambda n, m=1: pl.BlockSpec((None, m * tq, n), lambda b: (b, 0, 0))
    cache = lambda n, m=1: pl.BlockSpec((None, m * past, n), lambda b: (b, 0, 0))
    scratch = [pltpu.VMEM((n_tiles * tk, N_QUADS * QUAD_LANES), BF16),
               pltpu.VMEM((n_tiles * tk, KV_DIM), BF16), pltpu.VMEM((n_tiles * tk, KV_DIM), BF16)]
    return pl.pallas_call(
        functools.partial(_attend_sample_kernel, tq=tq, tk=tk, past=past, n_tiles=n_tiles),
        grid=(nseq,),
        in_specs=[blk(N_IDX_HEADS * IDX_DIM), blk(LANES), blk(D_ATTN), blk(D_ATTN), cache(IDX_DIM),
                  cache(HEAD_DIM, N_KV_HEADS), cache(HEAD_DIM, N_KV_HEADS), blk(HEAD_DIM, N_KV_HEADS),
                  blk(HEAD_DIM, N_KV_HEADS)],
        out_specs=blk(D_ATTN),
        out_shape=jax.ShapeDtypeStruct((nseq, tq, D_ATTN), BF16),
        scratch_shapes=scratch + _attend_scratch(tq, tk, n_tiles),
        compiler_params=pltpu.CompilerParams(dimension_semantics=("arbitrary",),
                                             vmem_limit_bytes=VMEM_LIMIT_BYTES),
        name="attend_sample",
    )(qi, aux, q, ga, cache_ik, cache_k, cache_v, new_k, new_v)


def _merge_kernel(x_ref, za_ref, zc_ref, w_ref, g_ref, y_ref):
    z = jnp.concatenate([za_ref[...], zc_ref[...]], axis=1)
    y = jnp.dot(z, w_ref[...], preferred_element_type=F32)
    y_ref[...] = x_ref[...] + _rmsnorm(y, g_ref[...])


def _merge(x, za, zc, w_out, g_post, tm):
    rows = x.shape[0]
    row_spec = lambda n: pl.BlockSpec((tm, n), lambda i: (i, 0))
    return pl.pallas_call(
        _merge_kernel,
        grid=(rows // tm,),
        in_specs=[row_spec(D_MODEL), row_spec(D_ATTN), row_spec(D_CONV), _resident((D_MODEL, D_MODEL), 1),
                  _resident((1, D_MODEL), 1)],
        out_specs=row_spec(D_MODEL),
        out_shape=jax.ShapeDtypeStruct((rows, D_MODEL), F32),
        compiler_params=pltpu.CompilerParams(dimension_semantics=("arbitrary",),
                                             vmem_limit_bytes=VMEM_LIMIT_BYTES),
        name="merge",
    )(x, za, zc, w_out, g_post)


PROJECT_ROWS = 256
ATTEND_ROWS = 256
MERGE_ROWS = 512


def _layer(xp, xs, cache_k, cache_v, cache_ik, state, g_pre, w_in, w_conv, w_out, g_post):
    nb, t, _ = xp.shape
    nseq, seqlen, _ = xs.shape
    g_pre = g_pre.reshape(1, D_MODEL)
    g_post = g_post.reshape(1, D_MODEL)
    w_pack = w_in.T.astype(BF16)
    w_out = w_out.astype(BF16)

    assert PROJECT_ROWS == ATTEND_ROWS
    q, k, v, ga, qi, ik, aux, kpad, zc, kb, vt, cs = _project_prompt(xp, g_pre, w_pack, w_conv, PROJECT_ROWS)
    za = _attend_prompt(qi, aux, q, ga, kpad, kb, vt, ATTEND_ROWS)
    yp = _merge(xp.reshape(nb * t, D_MODEL), za.reshape(nb * t, D_ATTN), zc.reshape(nb * t, D_CONV),
                w_out, g_post, MERGE_ROWS).reshape(nb, t, D_MODEL)

    sq, sk, sv, sga, sqi, sik, saux, _, szc, scs = _project_sample(xs, g_pre, w_pack, w_conv, state)
    per_seq = lambda a: a.reshape(nseq, -1, a.shape[-1])
    kv_rows = lambda a: a.reshape(nseq, -1, HEAD_DIM)
    sza = _attend_sample(per_seq(sqi), per_seq(saux), per_seq(sq), per_seq(sga), cache_ik,
                         kv_rows(cache_k), kv_rows(cache_v), per_seq(sk), per_seq(sv), ATTEND_ROWS)
    ys = _merge(xs.reshape(nseq * seqlen, D_MODEL), sza.reshape(nseq * seqlen, D_ATTN), szc, w_out,
                g_post, MERGE_ROWS).reshape(nseq, seqlen, D_MODEL)

    heads = lambda a, lead: a.reshape(lead + (N_KV_HEADS, HEAD_DIM))
    return (yp, ys, heads(k, (nb, t)), heads(v, (nb, t)), ik, cs,
            heads(sk, (nseq, seqlen)), heads(sv, (nseq, seqlen)), per_seq(sik), scs)


def kernel(x_prompt, x_sample, cache_k, cache_v, cache_idx_k, state_conv, g_pre, w_in, w_conv, w_out,
           g_post):
    depth = g_pre.shape[0]
    xp, xs = x_prompt, x_sample
    outs = []
    for l in range(depth):
        res = _layer(xp, xs, cache_k[l], cache_v[l], cache_idx_k[l], state_conv[l], g_pre[l], w_in[l],
                     w_conv[l], w_out[l], g_post[l])
        xp, xs = res[0], res[1]
        outs.append(res[2:])
    stacked = [jnp.stack([o[i] for o in outs]) for i in range(8)]
    return (xp, xs) + tuple(stacked)
```

```python
import functools

import jax
import jax.numpy as jnp
from jax import lax
from jax.experimental import pallas as pl
from jax.experimental.pallas import tpu as pltpu

F32 = jnp.float32
BF16 = jnp.bfloat16
I32 = jnp.int32

D_MODEL = 2048
D_ATTN = 1024
D_CONV = 1024
HEAD_DIM = 128
N_KV_HEADS = 2
KV_GROUP = 4
KV_DIM = N_KV_HEADS * HEAD_DIM
N_IDX_HEADS = 16
IDX_DIM = 64
TOPK_MAX = 256
CHUNK = 64
CONV_WIDTH = 3
RMS_EPS = 1e-6

LANES = 128
SUBLANES = 8
VMEM_LIMIT_BYTES = 60 * 1000 * 1024

OFF_Q = 0
OFF_K = OFF_Q + D_ATTN
OFF_V = OFF_K + KV_DIM
OFF_GA = OFF_V + KV_DIM
OFF_QI = OFF_GA + D_ATTN
OFF_KW = OFF_QI + N_IDX_HEADS * IDX_DIM
OFF_B = OFF_KW + IDX_DIM + N_IDX_HEADS
OFF_C = OFF_B + D_CONV
OFF_HC = OFF_C + D_CONV
OFF_GB = OFF_HC + D_CONV
D_PROJ = OFF_GB + D_CONV
BF16_ROWS = 16
assert all(o % BF16_ROWS == 0 for o in (OFF_K, OFF_V, OFF_GA, OFF_QI, OFF_KW, OFF_B, OFF_C, OFF_HC, OFF_GB))

CONV_COLS = 256
PAD_ROWS = SUBLANES
HEADS_PER_QUAD = 4
QUAD_LANES = HEADS_PER_QUAD * IDX_DIM
N_QUADS = N_IDX_HEADS // HEADS_PER_QUAD
IDX_SCALE = (IDX_DIM ** -0.5) * (N_IDX_HEADS ** -0.5)
ATT_SCALE = HEAD_DIM ** -0.5
INT_MIN = -(2 ** 31)
INT_MAX = 2 ** 31 - 1
KEY_BITS = 32
I16 = jnp.int16
I16_ROWS = 2 * SUBLANES
HALF_BITS = KEY_BITS // 2
HALF_MASK = 2 ** HALF_BITS - 1
HALF_MIN = -(2 ** (HALF_BITS - 1))
HALF_MAX = 2 ** (HALF_BITS - 1) - 1
LOG2E = 1.4426950408889634
NEG_INF = float("-inf")


def _silu(x):
    return x * jax.nn.sigmoid(x)


def _dot_nt(a, b):
    return lax.dot_general(a, b, (((1,), (1,)), ((), ())), preferred_element_type=F32)


def _rmsnorm(x, g):
    ms = jnp.mean(x * x, axis=-1, keepdims=True)
    return x * lax.rsqrt(ms + RMS_EPS) * g


def _project_rows(h, w_ref, q_ref, k_ref, v_ref, ga_ref, qi_ref, ik_ref, aux_ref, kpad_ref,
                  kb_ref=None, vt_ref=None):
    rows = h.shape[0]

    def mm(r0, n):
        return _dot_nt(h, w_ref[r0:r0 + n, :])

    q_ref[...] = mm(OFF_Q, D_ATTN).astype(BF16)
    kk = mm(OFF_K, KV_DIM)
    vv = mm(OFF_V, KV_DIM)
    for n in range(N_KV_HEADS):
        k_ref[pl.ds(n, rows, stride=N_KV_HEADS), :] = kk[:, n * HEAD_DIM:(n + 1) * HEAD_DIM]
        v_ref[pl.ds(n, rows, stride=N_KV_HEADS), :] = vv[:, n * HEAD_DIM:(n + 1) * HEAD_DIM]
    if kb_ref is not None:
        kb_ref[...] = kk.astype(BF16)
        vt_ref[...] = vv.T.astype(BF16)
    ga_ref[...] = mm(OFF_GA, D_ATTN)
    qi_ref[...] = mm(OFF_QI, N_IDX_HEADS * IDX_DIM).astype(BF16)
    kw = mm(OFF_KW, LANES)
    aux_ref[...] = kw
    ik_ref[...] = kw[:, :IDX_DIM]
    lane = lax.broadcasted_iota(I32, kw.shape, 1)
    lo = jnp.where(lane < IDX_DIM, kw, 0.0)
    hi = pltpu.roll(lo, IDX_DIM, axis=1)
    zero = jnp.zeros_like(lo)
    blocks = (lo, zero, hi, zero, zero, lo, zero, hi)
    for i, blk in enumerate(blocks):
        kpad_ref[:, i * LANES:(i + 1) * LANES] = blk.astype(BF16)


def _conv_chunk(h, w_ref, wc_ref, c):
    def mm(off):
        return _dot_nt(h, w_ref[off + c:off + c + CONV_COLS, :])

    return mm(OFF_B), mm(OFF_C) * mm(OFF_HC), mm(OFF_GB)


def _conv_out(bg, gb, u, um1, um2, wc_ref, c):
    w0 = wc_ref[0:1, c:c + CONV_COLS]
    w1 = wc_ref[1:2, c:c + CONV_COLS]
    w2 = wc_ref[2:3, c:c + CONV_COLS]
    conv = w0 * um2 + w1 * um1 + w2 * u
    return (bg * conv * _silu(gb)).astype(BF16)


def _project_prompt_kernel(x_ref, g_ref, w_ref, wc_ref, q_ref, k_ref, v_ref, ga_ref, qi_ref, ik_ref,
                           aux_ref, kpad_ref, zc_ref, kb_ref, vt_ref, cs_ref, upad_ref, *, tm):
    @pl.when(pl.program_id(1) == 0)
    def _():
        upad_ref[0:PAD_ROWS, :] = jnp.zeros((PAD_ROWS, D_CONV), F32)

    h = _rmsnorm(x_ref[...], g_ref[...]).astype(BF16)
    _project_rows(h, w_ref, q_ref, k_ref, v_ref, ga_ref, qi_ref, ik_ref, aux_ref, kpad_ref, kb_ref,
                  vt_ref)
    for c in range(0, D_CONV, CONV_COLS):
        bg, u, gb = _conv_chunk(h, w_ref, wc_ref, c)
        upad_ref[PAD_ROWS:PAD_ROWS + tm, c:c + CONV_COLS] = u
        um1 = upad_ref[PAD_ROWS - 1:PAD_ROWS - 1 + tm, c:c + CONV_COLS]
        um2 = upad_ref[PAD_ROWS - 2:PAD_ROWS - 2 + tm, c:c + CONV_COLS]
        zc_ref[:, c:c + CONV_COLS] = _conv_out(bg, gb, u, um1, um2, wc_ref, c)
    last = upad_ref[PAD_ROWS + tm - (CONV_WIDTH - 1):PAD_ROWS + tm, :]
    cs_ref[...] = last
    upad_ref[PAD_ROWS - (CONV_WIDTH - 1):PAD_ROWS, :] = last


def _project_sample_kernel(x_ref, g_ref, w_ref, wc_ref, st_ref, q_ref, k_ref, v_ref, ga_ref, qi_ref,
                           ik_ref, aux_ref, kpad_ref, zc_ref, cs_ref, upad_ref, *, nseq, seqlen):
    rows = nseq * seqlen
    upad_ref[:, PAD_ROWS - (CONV_WIDTH - 1):PAD_ROWS, :] = st_ref[...]
    h = _rmsnorm(x_ref[...], g_ref[...]).astype(BF16)
    _project_rows(h, w_ref, q_ref, k_ref, v_ref, ga_ref, qi_ref, ik_ref, aux_ref, kpad_ref)
    for c in range(0, D_CONV, CONV_COLS):
        bg, u, gb = _conv_chunk(h, w_ref, wc_ref, c)
        upad_ref[:, PAD_ROWS:PAD_ROWS + seqlen, c:c + CONV_COLS] = u.reshape(nseq, seqlen, CONV_COLS)
        um1 = upad_ref[:, PAD_ROWS - 1:PAD_ROWS - 1 + seqlen, c:c + CONV_COLS].reshape(rows, CONV_COLS)
        um2 = upad_ref[:, PAD_ROWS - 2:PAD_ROWS - 2 + seqlen, c:c + CONV_COLS].reshape(rows, CONV_COLS)
        zc_ref[:, c:c + CONV_COLS] = _conv_out(bg, gb, u, um1, um2, wc_ref, c)
    cs_ref[...] = upad_ref[:, PAD_ROWS + seqlen - (CONV_WIDTH - 1):PAD_ROWS + seqlen, :]


_PROJECT_OUTS = ((1, D_ATTN, BF16), (N_KV_HEADS, HEAD_DIM, F32), (N_KV_HEADS, HEAD_DIM, F32),
                 (1, D_ATTN, F32), (1, N_IDX_HEADS * IDX_DIM, BF16), (1, IDX_DIM, F32), (1, LANES, F32),
                 (1, N_QUADS * QUAD_LANES, BF16), (1, D_CONV, BF16))


def _project_out_shapes(lead, rows):
    return [jax.ShapeDtypeStruct(lead + (m * rows, n), dt) for m, n, dt in _PROJECT_OUTS]


def _resident(shape, ngrid):
    zeros = (0,) * len(shape)
    if ngrid == 1:
        return pl.BlockSpec(shape, lambda i: zeros, pipeline_mode=pl.Buffered(1))
    return pl.BlockSpec(shape, lambda b, i: zeros, pipeline_mode=pl.Buffered(1))


def _project_prompt(x, g_pre, w_pack, w_conv, tm):
    nb, t, _ = x.shape
    grid = (nb, t // tm)
    row_spec = lambda n, m=1: pl.BlockSpec((None, m * tm, n), lambda b, i: (b, i, 0))
    out_shapes = _project_out_shapes((nb,), t) + [
        jax.ShapeDtypeStruct((nb, t, KV_DIM), BF16),
        jax.ShapeDtypeStruct((nb, t // tm, KV_DIM, tm), BF16),
        jax.ShapeDtypeStruct((nb, CONV_WIDTH - 1, D_CONV), F32)]
    out_specs = [row_spec(n, m) for m, n, _ in _PROJECT_OUTS]
    out_specs.append(row_spec(KV_DIM))
    out_specs.append(pl.BlockSpec((None, None, KV_DIM, tm), lambda b, i: (b, i, 0, 0)))
    out_specs.append(pl.BlockSpec((None, CONV_WIDTH - 1, D_CONV), lambda b, i: (b, 0, 0)))
    return pl.pallas_call(
        functools.partial(_project_prompt_kernel, tm=tm),
        grid=grid,
        in_specs=[row_spec(D_MODEL), _resident((1, D_MODEL), 2), _resident((D_PROJ, D_MODEL), 2),
                  _resident((CONV_WIDTH, D_CONV), 2)],
        out_specs=out_specs,
        out_shape=out_shapes,
        scratch_shapes=[pltpu.VMEM((PAD_ROWS + tm, D_CONV), F32)],
        compiler_params=pltpu.CompilerParams(dimension_semantics=("arbitrary", "arbitrary"),
                                             vmem_limit_bytes=VMEM_LIMIT_BYTES),
        name="project_prompt",
    )(x, g_pre, w_pack, w_conv)


def _project_sample(x, g_pre, w_pack, w_conv, state):
    nseq, seqlen, _ = x.shape
    rows = nseq * seqlen
    full = lambda shape: pl.BlockSpec(shape, lambda i: (0,) * len(shape))
    out_shapes = _project_out_shapes((), rows) + [jax.ShapeDtypeStruct((nseq, CONV_WIDTH - 1, D_CONV), F32)]
    out_specs = [full((m * rows, n)) for m, n, _ in _PROJECT_OUTS] + [full((nseq, CONV_WIDTH - 1, D_CONV))]
    return pl.pallas_call(
        functools.partial(_project_sample_kernel, nseq=nseq, seqlen=seqlen),
        grid=(1,),
        in_specs=[full((rows, D_MODEL)), full((1, D_MODEL)), _resident((D_PROJ, D_MODEL), 1),
                  full((CONV_WIDTH, D_CONV)), full((nseq, CONV_WIDTH - 1, D_CONV))],
        out_specs=out_specs,
        out_shape=out_shapes,
        scratch_shapes=[pltpu.VMEM((nseq, PAD_ROWS + seqlen, D_CONV), F32)],
        compiler_params=pltpu.CompilerParams(dimension_semantics=("arbitrary",),
                                             vmem_limit_bytes=VMEM_LIMIT_BYTES),
        name="project_sample",
    )(x.reshape(rows, D_MODEL), g_pre, w_pack, w_conv, state)


def _sortable_key(score):
    bits = lax.bitcast_convert_type(score, I32)
    return bits ^ ((bits >> 31) & INT_MAX)


def _attend(qi, aux, q, ga, kpad, kk, vv, z_ref, keys_scr, wb_scr, thr_scr, m_scr, l_scr, acc_scr, *,
            tq, tk, n_tiles, adm_fn, search_pred, n_cols):
    halves = tk // LANES
    qstack = jnp.concatenate([qi[:, u * QUAD_LANES:(u + 1) * QUAD_LANES] for u in range(N_QUADS)], axis=0)
    for h in range(N_IDX_HEADS):
        wb_scr[h] = jnp.broadcast_to(aux[:, IDX_DIM + h:IDX_DIM + h + 1], (tq, LANES))

    def idx_tile(kt, carry):
        start = pl.multiple_of(kt * tk, tk)
        acc = [jnp.zeros((tq, LANES), F32) for _ in range(halves)]
        for c in range(HEADS_PER_QUAD):
            kp = kpad[pl.ds(start, tk), c * QUAD_LANES:(c + 1) * QUAD_LANES]
            s = lax.dot_general(qstack, kp, (((1,), (1,)), ((), ())), preferred_element_type=F32)
            for u in range(N_QUADS):
                wb = wb_scr[HEADS_PER_QUAD * u + c]
                for hf in range(halves):
                    blk = s[u * tq:(u + 1) * tq, hf * LANES:(hf + 1) * LANES]
                    acc[hf] = acc[hf] + jnp.maximum(blk, 0.0) * wb
        key = _sortable_key(jnp.concatenate(acc, axis=1) * IDX_SCALE)
        adm = adm_fn(kt)
        if adm is not None:
            key = jnp.where(adm, key, INT_MIN)
        keys_scr[kt] = key
        return carry

    lax.fori_loop(0, n_tiles, idx_tile, 0)

    def count(pred_fn):
        def body(kt, c):
            p = pred_fn(kt, keys_scr[kt]).astype(I32)
            for hf in range(halves):
                c = c + p[:, hf * LANES:(hf + 1) * LANES]
            return c

        c = lax.fori_loop(0, n_tiles, body, jnp.zeros((tq, LANES), I32))
        return jnp.sum(c, axis=1, keepdims=True)

    thr_scr[...] = jnp.full((tq, LANES), INT_MIN + 1, I32)

    @pl.when(search_pred)
    def _():
        def bisect(_, carry):
            lo, hi = carry
            mid = (lo >> 1) + (hi >> 1) + ((lo | hi) & 1)
            ge = count(lambda kt, t: t >= mid) >= TOPK_MAX
            return jnp.where(ge, mid, lo), jnp.where(ge, hi, mid)

        lo0 = jnp.full((tq, 1), INT_MIN, I32)
        hi0 = jnp.full((tq, 1), INT_MAX, I32)
        lo, _ = lax.fori_loop(0, KEY_BITS, bisect, (lo0, hi0))
        thr_scr[...] = jnp.broadcast_to(lo, (tq, LANES))
        excess = count(lambda kt, t: t >= lo) - TOPK_MAX

        @pl.when(jnp.max(excess.astype(F32)) > 0.0)
        def _():
            need = TOPK_MAX - count(lambda kt, t: t > lo)

            def col(kt):
                return kt * tk + lax.broadcasted_iota(I32, (tq, tk), 1)

            def pos_bisect(_, carry):
                plo, phi = carry
                mid = (plo + phi) >> 1
                ok = count(lambda kt, t: (t == lo) & (col(kt) <= mid)) >= need
                return jnp.where(ok, plo, mid), jnp.where(ok, mid, phi)

            plo0 = jnp.full((tq, 1), -1, I32)
            phi0 = jnp.full((tq, 1), n_cols - 1, I32)
            steps = max(1, (n_cols - 1).bit_length()) + 1
            _, pos = lax.fori_loop(0, steps, pos_bisect, (plo0, phi0))

            def drop(kt, carry):
                t = keys_scr[kt]
                keys_scr[kt] = jnp.where((t == lo) & (col(kt) > pos), INT_MIN, t)
                return carry

            lax.fori_loop(0, n_tiles, drop, 0)

    thr = thr_scr[:, 0:1]

    rows = KV_GROUP * tq
    qn = [jnp.concatenate([q[:, (KV_GROUP * n + g) * HEAD_DIM:(KV_GROUP * n + g + 1) * HEAD_DIM]
                           for g in range(KV_GROUP)], axis=0) for n in range(N_KV_HEADS)]
    m_scr[...] = jnp.full(m_scr.shape, NEG_INF, F32)
    l_scr[...] = jnp.zeros(l_scr.shape, F32)
    acc_scr[...] = jnp.zeros(acc_scr.shape, F32)

    def att_tile(kt, carry):
        start = pl.multiple_of(kt * tk, tk)
        sel = keys_scr[kt] >= thr
        for n in range(N_KV_HEADS):
            kt_n = kk[pl.ds(start, tk), n * HEAD_DIM:(n + 1) * HEAD_DIM]
            lg = lax.dot_general(qn[n], kt_n, (((1,), (1,)), ((), ())), preferred_element_type=F32)
            lg = lg * ATT_SCALE
            lg = jnp.concatenate([jnp.where(sel, lg[g * tq:(g + 1) * tq], NEG_INF)
                                  for g in range(KV_GROUP)], axis=0)
            m_prev = m_scr[n][:, 0:1]
            m_new = jnp.maximum(m_prev, jnp.max(lg, axis=1, keepdims=True))
            m_safe = jnp.where(m_new == NEG_INF, 0.0, m_new)
            alpha = jnp.exp(m_prev - m_safe)
            p = jnp.exp(lg - m_safe)
            l_new = alpha * l_scr[n][:, 0:1] + jnp.sum(p, axis=1, keepdims=True)
            vt_n = vv[pl.ds(start, tk), n * HEAD_DIM:(n + 1) * HEAD_DIM]
            pv = jnp.dot(p.astype(BF16), vt_n, preferred_element_type=F32)
            acc_scr[n] = acc_scr[n] * alpha + pv
            m_scr[n] = jnp.broadcast_to(m_new, (rows, LANES))
            l_scr[n] = jnp.broadcast_to(l_new, (rows, LANES))
        return carry

    lax.fori_loop(0, n_tiles, att_tile, 0)

    outs = []
    for n in range(N_KV_HEADS):
        o = acc_scr[n] / l_scr[n]
        outs.extend(o[g * tq:(g + 1) * tq] for g in range(KV_GROUP))
    z_ref[...] = (jnp.concatenate(outs, axis=1) * _silu(ga)).astype(BF16)


def _attend_prompt_kernel(qi_ref, aux_ref, q_ref, ga_ref, kpad_ref, k_ref, vt_ref, z_ref, keys_scr,
                          hi_scr, lo_scr, thr_scr, m_scr, l_scr, acc_scr, *, tq, tk, n_cols):
    j = pl.program_id(1)
    n_tiles = j + 1

    def for_tiles(fn, init):
        return lax.fori_loop(0, n_tiles, fn, init)

    assert tq == tk == TOPK_MAX and tq % CHUNK == 0
    lanes_q = KV_GROUP * tq
    qi = qi_ref[...]
    qstack = jnp.concatenate([qi[:, u * QUAD_LANES:(u + 1) * QUAD_LANES] for u in range(N_QUADS)], axis=0)
    w_t = aux_ref[...].T
    w_rows = [w_t[IDX_DIM + h:IDX_DIM + h + 1, :] for h in range(N_IDX_HEADS)]
    key_chunk = lax.broadcasted_iota(I32, (tk, tq), 0) // CHUNK
    qry_chunk = lax.broadcasted_iota(I32, (tk, tq), 1) // CHUNK
    diag_adm = key_chunk <= qry_chunk

    def idx_tile(kt, carry):
        start = pl.multiple_of(kt * tk, tk)
        acc = jnp.zeros((tk, tq), F32)
        for c in range(HEADS_PER_QUAD):
            kp = kpad_ref[pl.ds(start, tk), c * QUAD_LANES:(c + 1) * QUAD_LANES]
            s = lax.dot_general(kp, qstack, (((1,), (1,)), ((), ())), preferred_element_type=F32)
            for u in range(N_QUADS):
                acc = acc + jnp.maximum(s[:, u * tq:(u + 1) * tq], 0.0) * w_rows[HEADS_PER_QUAD * u + c]
        key = _sortable_key(acc * IDX_SCALE)
        key = jnp.where((diag_adm & (kt == j)) | (kt < j), key, INT_MIN)
        keys_scr[kt] = key
        hi_scr[kt] = (key >> HALF_BITS).astype(I16)
        lo_scr[kt] = ((key & HALF_MASK) + HALF_MIN).astype(I16)
        return carry

    for_tiles(idx_tile, 0)

    def count(pred_fn):
        def body(kt, c8):
            p = pred_fn(kt, keys_scr[kt]).astype(I32)
            for r in range(0, tk, SUBLANES):
                c8 = c8 + p[r:r + SUBLANES, :]
            return c8

        c8 = for_tiles(body, jnp.zeros((SUBLANES, tq), I32))
        return jnp.sum(c8, axis=0, keepdims=True)

    def count_half(ref, th):
        def body(kt, c16):
            p = jnp.where(ref[kt] >= th, jnp.int16(1), jnp.int16(0))
            for r in range(0, tk, I16_ROWS):
                c16 = c16 + p[r:r + I16_ROWS, :]
            return c16

        c16 = for_tiles(body, jnp.zeros((I16_ROWS, tq), I16))
        return jnp.sum(c16.astype(I32), axis=0, keepdims=True)

    def bisect_half(ref):
        def step(_, carry):
            lo, hi = carry
            mid = (lo + hi + 1) >> 1
            cnt = count_half(ref, jnp.minimum(mid, HALF_MAX).astype(I16))
            ge = (cnt >= TOPK_MAX) & (mid <= HALF_MAX)
            return jnp.where(ge, mid, lo), jnp.where(ge, hi, mid)

        lo0 = jnp.full((1, tq), HALF_MIN, I32)
        hi0 = jnp.full((1, tq), HALF_MAX + 1, I32)
        return lax.fori_loop(0, HALF_BITS, step, (lo0, hi0))[0]

    thr_scr[...] = jnp.full((SUBLANES, tq), INT_MIN + 1, I32)

    @pl.when(j >= 1)
    def _():
        top = bisect_half(hi_scr)
        top16 = top.astype(I16)

        def narrow(kt, carry):
            h = hi_scr[kt]
            lo_scr[kt] = jnp.where(h == top16, lo_scr[kt],
                                   jnp.where(h > top16, jnp.int16(HALF_MAX), jnp.int16(HALF_MIN)))
            return carry

        for_tiles(narrow, 0)
        bot = bisect_half(lo_scr)
        lo = top * (HALF_MASK + 1) + (bot - HALF_MIN)
        thr_scr[...] = jnp.broadcast_to(lo, (SUBLANES, tq))
        excess = count_half(lo_scr, bot.astype(I16)) - TOPK_MAX

        @pl.when(jnp.max(excess.astype(F32)) > 0.0)
        def _():
            need = TOPK_MAX - count(lambda kt, t: t > lo)

            def pos_of(kt):
                return kt * tk + lax.broadcasted_iota(I32, (tk, tq), 0)

            def pos_bisect(_, carry):
                plo, phi = carry
                mid = (plo + phi) >> 1
                ok = count(lambda kt, t: (t == lo) & (pos_of(kt) <= mid)) >= need
                return jnp.where(ok, plo, mid), jnp.where(ok, mid, phi)

            plo0 = jnp.full((1, tq), -1, I32)
            phi0 = jnp.full((1, tq), n_cols - 1, I32)
            steps = max(1, (n_cols - 1).bit_length()) + 1
            _, pos = lax.fori_loop(0, steps, pos_bisect, (plo0, phi0))

            def drop(kt, carry):
                t = keys_scr[kt]
                keys_scr[kt] = jnp.where((t == lo) & (pos_of(kt) > pos), INT_MIN, t)
                return carry

            for_tiles(drop, 0)

    thr = thr_scr[0:1, :]

    q = q_ref[...]
    qn = [jnp.concatenate([q[:, (KV_GROUP * n + g) * HEAD_DIM:(KV_GROUP * n + g + 1) * HEAD_DIM]
                           for g in range(KV_GROUP)], axis=0) for n in range(N_KV_HEADS)]
    m_scr[...] = jnp.full(m_scr.shape, NEG_INF, F32)
    l_scr[...] = jnp.zeros(l_scr.shape, F32)
    acc_scr[...] = jnp.zeros(acc_scr.shape, F32)

    def att_tile(kt, carry):
        start = pl.multiple_of(kt * tk, tk)
        sel = keys_scr[kt] >= thr
        for n in range(N_KV_HEADS):
            kt_n = k_ref[pl.ds(start, tk), n * HEAD_DIM:(n + 1) * HEAD_DIM]
            lg = lax.dot_general(kt_n, qn[n], (((1,), (1,)), ((), ())), preferred_element_type=F32)
            lg = lg * (ATT_SCALE * LOG2E)
            lg = jnp.concatenate([jnp.where(sel, lg[:, g * tq:(g + 1) * tq], NEG_INF)
                                  for g in range(KV_GROUP)], axis=1)
            m_prev = m_scr[n][0:1, :]
            m_new = jnp.maximum(m_prev, jnp.max(lg, axis=0, keepdims=True))
            m_safe = jnp.where(m_new == NEG_INF, 0.0, m_new)
            alpha = jnp.exp2(m_prev - m_safe)
            p = jnp.exp2(lg - m_safe)
            l_new = alpha * l_scr[n][0:1, :] + jnp.sum(p, axis=0, keepdims=True)
            vt_n = vt_ref[kt, n * HEAD_DIM:(n + 1) * HEAD_DIM, :]
            pv = jnp.dot(vt_n, p.astype(BF16), preferred_element_type=F32)
            acc_scr[n] = acc_scr[n] * alpha + pv
            m_scr[n] = jnp.broadcast_to(m_new, (SUBLANES, lanes_q))
            l_scr[n] = jnp.broadcast_to(l_new, (SUBLANES, lanes_q))
        return carry

    for_tiles(att_tile, 0)

    outs = []
    for n in range(N_KV_HEADS):
        o = acc_scr[n] / l_scr[n][0:1, :]
        outs.extend(o[:, g * tq:(g + 1) * tq].T for g in range(KV_GROUP))
    z_ref[...] = (jnp.concatenate(outs, axis=1) * _silu(ga_ref[...])).astype(BF16)


def _attend_sample_kernel(qi_ref, aux_ref, q_ref, ga_ref, cik_ref, ck_ref, cv_ref, nk_ref, nv_ref,
                          z_ref, kpad_s, k_s, v_s, keys_scr, wb_scr, thr_scr, m_scr, l_scr, acc_scr,
                          *, tq, tk, past, n_tiles):
    r = lax.broadcasted_iota(I32, (LANES, N_QUADS * QUAD_LANES), 0)
    n = lax.broadcasted_iota(I32, (LANES, N_QUADS * QUAD_LANES), 1)
    place = ((r < IDX_DIM) & ((n % QUAD_LANES) == (n // QUAD_LANES) * IDX_DIM + r)).astype(BF16)
    aux = aux_ref[...]
    kpad_s[0:past, :] = jnp.dot(cik_ref[...].astype(BF16), place[:IDX_DIM],
                                preferred_element_type=F32).astype(BF16)
    kpad_s[past:past + tq, :] = jnp.dot(aux.astype(BF16), place, preferred_element_type=F32).astype(BF16)
    for n in range(N_KV_HEADS):
        cols = slice(n * HEAD_DIM, (n + 1) * HEAD_DIM)
        k_s[0:past, cols] = ck_ref[pl.ds(n, past, stride=N_KV_HEADS), :].astype(BF16)
        k_s[past:past + tq, cols] = nk_ref[pl.ds(n, tq, stride=N_KV_HEADS), :].astype(BF16)
        v_s[0:past, cols] = cv_ref[pl.ds(n, past, stride=N_KV_HEADS), :].astype(BF16)
        v_s[past:past + tq, cols] = nv_ref[pl.ds(n, tq, stride=N_KV_HEADS), :].astype(BF16)
    tail = n_tiles * tk - (past + tq)
    for ref in (kpad_s, k_s, v_s):
        ref[past + tq:, :] = jnp.zeros((tail, ref.shape[1]), BF16)

    n_keys = past + tq
    assert n_keys >= TOPK_MAX and past % CHUNK == 0 and tq <= CHUNK

    def adm_fn(kt):
        return kt * tk + lax.broadcasted_iota(I32, (tq, tk), 1) < n_keys

    _attend(qi_ref[...], aux, q_ref[...], ga_ref[...], kpad_s, k_s, v_s, z_ref, keys_scr, wb_scr,
            thr_scr, m_scr, l_scr, acc_scr, tq=tq, tk=tk, n_tiles=n_tiles, adm_fn=adm_fn,
            search_pred=True, n_cols=n_tiles * tk)


def _attend_scratch(tq, tk, n_tiles):
    rows = KV_GROUP * tq
    return [pltpu.VMEM((n_tiles, tq, tk), I32), pltpu.VMEM((N_IDX_HEADS, tq, LANES), F32),
            pltpu.VMEM((tq, LANES), I32), pltpu.VMEM((N_KV_HEADS, rows, LANES), F32),
            pltpu.VMEM((N_KV_HEADS, rows, LANES), F32), pltpu.VMEM((N_KV_HEADS, rows, HEAD_DIM), F32)]


def _attend_prompt(qi, aux, q, ga, kpad, kb, vt, tq):
    nb, t, _ = q.shape
    tk = tq
    n_tiles = t // tk
    lanes_q = KV_GROUP * tq
    blk = lambda n: pl.BlockSpec((None, tq, n), lambda b, j: (b, j, 0))
    seq = lambda n: pl.BlockSpec((None, t, n), lambda b, j: (b, 0, 0))
    scratch = [pltpu.VMEM((n_tiles, tk, tq), I32), pltpu.VMEM((n_tiles, tk, tq), I16),
               pltpu.VMEM((n_tiles, tk, tq), I16), pltpu.VMEM((SUBLANES, tq), I32),
               pltpu.VMEM((N_KV_HEADS, SUBLANES, lanes_q), F32),
               pltpu.VMEM((N_KV_HEADS, SUBLANES, lanes_q), F32),
               pltpu.VMEM((N_KV_HEADS, HEAD_DIM, lanes_q), F32)]
    return pl.pallas_call(
        functools.partial(_attend_prompt_kernel, tq=tq, tk=tk, n_cols=t),
        grid=(nb, t // tq),
        in_specs=[blk(N_IDX_HEADS * IDX_DIM), blk(LANES), blk(D_ATTN), blk(D_ATTN),
                  seq(N_QUADS * QUAD_LANES), seq(KV_DIM),
                  pl.BlockSpec((None, n_tiles, KV_DIM, tk), lambda b, j: (b, 0, 0, 0))],
        out_specs=blk(D_ATTN),
        out_shape=jax.ShapeDtypeStruct((nb, t, D_ATTN), BF16),
        scratch_shapes=scratch,
        compiler_params=pltpu.CompilerParams(dimension_semantics=("arbitrary", "arbitrary"),
                                             vmem_limit_bytes=VMEM_LIMIT_BYTES),
        name="attend_prompt",
    )(qi, aux, q, ga, kpad, kb, vt)


def _attend_sample(qi, aux, q, ga, cache_ik, cache_k, cache_v, new_k, new_v, tk):
    nseq, tq, _ = q.shape
    past = cache_ik.shape[1]
    n_tiles = -(-(past + tq) // tk)
    blk = lambda n, m=1: pl.BlockSpec((None, m * tq, n), lambda b: (b, 0, 0))
    cache = lambda n, m=1: pl.BlockSpec((None, m * past, n), lambda b: (b, 0, 0))
    scratch = [pltpu.VMEM((n_tiles * tk, N_QUADS * QUAD_LANES), BF16),
               pltpu.VMEM((n_tiles * tk, KV_DIM), BF16), pltpu.VMEM((n_tiles * tk, KV_DIM), BF16)]
    return pl.pallas_call(
        functools.partial(_attend_sample_kernel, tq=tq, tk=tk, past=past, n_tiles=n_tiles),
        grid=(nseq,),
        in_specs=[blk(N_IDX_HEADS * IDX_DIM), blk(LANES), blk(D_ATTN), blk(D_ATTN), cache(IDX_DIM),
                  cache(HEAD_DIM, N_KV_HEADS), cache(HEAD_DIM, N_KV_HEADS), blk(HEAD_DIM, N_KV_HEADS),
                  blk(HEAD_DIM, N_KV_HEADS)],
        out_specs=blk(D_ATTN),
        out_shape=jax.ShapeDtypeStruct((nseq, tq, D_ATTN), BF16),
        scratch_shapes=scratch + _attend_scratch(tq, tk, n_tiles),
        compiler_params=pltpu.CompilerParams(dimension_semantics=("arbitrary",),
                                             vmem_limit_bytes=VMEM_LIMIT_BYTES),
        name="attend_sample",
    )(qi, aux, q, ga, cache_ik, cache_k, cache_v, new_k, new_v)


def _merge_kernel(x_ref, za_ref, zc_ref, w_ref, g_ref, y_ref):
    z = jnp.concatenate([za_ref[...], zc_ref[...]], axis=1)
    y = jnp.dot(z, w_ref[...], preferred_element_type=F32)
    y_ref[...] = x_ref[...] + _rmsnorm(y, g_ref[...])


def _merge(x, za, zc, w_out, g_post, tm):
    rows = x.shape[0]
    row_spec = lambda n: pl.BlockSpec((tm, n), lambda i: (i, 0))
    return pl.pallas_call(
        _merge_kernel,
        grid=(rows // tm,),
        in_specs=[row_spec(D_MODEL), row_spec(D_ATTN), row_spec(D_CONV), _resident((D_MODEL, D_MODEL), 1),
                  _resident((1, D_MODEL), 1)],
        out_specs=row_spec(D_MODEL),
        out_shape=jax.ShapeDtypeStruct((rows, D_MODEL), F32),
        compiler_params=pltpu.CompilerParams(dimension_semantics=("arbitrary",),
                                             vmem_limit_bytes=VMEM_LIMIT_BYTES),
        name="merge",
    )(x, za, zc, w_out, g_post)


PROJECT_ROWS = 256
ATTEND_ROWS = 256
MERGE_ROWS = 512


def _layer(xp, xs, cache_k, cache_v, cache_ik, state, g_pre, w_in, w_conv, w_out, g_post):
    nb, t, _ = xp.shape
    nseq, seqlen, _ = xs.shape
    g_pre = g_pre.reshape(1, D_MODEL)
    g_post = g_post.reshape(1, D_MODEL)
    w_pack = w_in.T.astype(BF16)
    w_out = w_out.astype(BF16)

    assert PROJECT_ROWS == ATTEND_ROWS
    q, k, v, ga, qi, ik, aux, kpad, zc, kb, vt, cs = _project_prompt(xp, g_pre, w_pack, w_conv, PROJECT_ROWS)
    za = _attend_prompt(qi, aux, q, ga, kpad, kb, vt, ATTEND_ROWS)
    yp = _merge(xp.reshape(nb * t, D_MODEL), za.reshape(nb * t, D_ATTN), zc.reshape(nb * t, D_CONV),
                w_out, g_post, MERGE_ROWS).reshape(nb, t, D_MODEL)

    sq, sk, sv, sga, sqi, sik, saux, _, szc, scs = _project_sample(xs, g_pre, w_pack, w_conv, state)
    per_seq = lambda a: a.reshape(nseq, -1, a.shape[-1])
    kv_rows = lambda a: a.reshape(nseq, -1, HEAD_DIM)
    sza = _attend_sample(per_seq(sqi), per_seq(saux), per_seq(sq), per_seq(sga), cache_ik,
                         kv_rows(cache_k), kv_rows(cache_v), per_seq(sk), per_seq(sv), ATTEND_ROWS)
    ys = _merge(xs.reshape(nseq * seqlen, D_MODEL), sza.reshape(nseq * seqlen, D_ATTN), szc, w_out,
                g_post, MERGE_ROWS).reshape(nseq, seqlen, D_MODEL)

    heads = lambda a, lead: a.reshape(lead + (N_KV_HEADS, HEAD_DIM))
    return (yp, ys, heads(k, (nb, t)), heads(v, (nb, t)), ik, cs,
            heads(sk, (nseq, seqlen)), heads(sv, (nseq, seqlen)), per_seq(sik), scs)


def kernel(x_prompt, x_sample, cache_k, cache_v, cache_idx_k, state_conv, g_pre, w_in, w_conv, w_out,
           g_post):
    depth = g_pre.shape[0]
    xp, xs = x_prompt, x_sample
    outs = []
    for l in range(depth):
        res = _layer(xp, xs, cache_k[l], cache_v[l], cache_idx_k[l], state_conv[l], g_pre[l], w_in[l],
                     w_conv[l], w_out[l], g_post[l])
        xp, xs = res[0], res[1]
        outs.append(res[2:])
    stacked = [jnp.stack([o[i] for o in outs]) for i in range(8)]
    return (xp, xs) + tuple(stacked)
```

```python
import functools

import jax
import jax.numpy as jnp
from jax import lax
from jax.experimental import pallas as pl
from jax.experimental.pallas import tpu as pltpu

F32 = jnp.float32
BF16 = jnp.bfloat16
I32 = jnp.int32
I16 = jnp.int16

D_MODEL = 2048
D_ATTN = 1024
D_CONV = 1024
HEAD_DIM = 128
N_KV_HEADS = 2
KV_GROUP = 4
N_HEADS = N_KV_HEADS * KV_GROUP
KV_DIM = N_KV_HEADS * HEAD_DIM
N_IDX_HEADS = 16
IDX_DIM = 64
TOPK_MAX = 256
CHUNK = 64
CONV_WIDTH = 3
RMS_EPS = 1e-6

LANES = 128
SUBLANES = 8
I16_ROWS = 2 * SUBLANES
VMEM_LIMIT_BYTES = 60 * 1000 * 1024

OFF_Q = 0
OFF_K = OFF_Q + D_ATTN
OFF_V = OFF_K + KV_DIM
OFF_GA = OFF_V + KV_DIM
OFF_QI = OFF_GA + D_ATTN
OFF_KW = OFF_QI + N_IDX_HEADS * IDX_DIM
OFF_B = OFF_KW + IDX_DIM + N_IDX_HEADS
OFF_C = OFF_B + D_CONV
OFF_HC = OFF_C + D_CONV
OFF_GB = OFF_HC + D_CONV
D_PROJ = OFF_GB + D_CONV
assert all(o % I16_ROWS == 0 for o in (OFF_K, OFF_V, OFF_GA, OFF_QI, OFF_KW, OFF_B, OFF_C, OFF_HC, OFF_GB))

CONV_COLS = 256
PAD_ROWS = SUBLANES
HEADS_PER_QUAD = 4
QUAD_LANES = HEADS_PER_QUAD * IDX_DIM
N_QUADS = N_IDX_HEADS // HEADS_PER_QUAD
IDX_SCALE = (IDX_DIM ** -0.5) * (N_IDX_HEADS ** -0.5)
LOG2E = 1.4426950408889634
ATT_SCALE_LOG2 = HEAD_DIM ** -0.5 * LOG2E
INT_MIN = -(2 ** 31)
INT_MAX = 2 ** 31 - 1
HALF_BITS = 16
HALF_MASK = 2 ** HALF_BITS - 1
HALF_MIN = -(2 ** (HALF_BITS - 1))
HALF_MAX = 2 ** (HALF_BITS - 1) - 1
NEG_INF = float("-inf")


def _silu(x):
    return x * jax.nn.sigmoid(x)


def _dot_nt(a, b):
    return lax.dot_general(a, b, (((1,), (1,)), ((), ())), preferred_element_type=F32)


def _rmsnorm(x, g):
    ms = jnp.mean(x * x, axis=-1, keepdims=True)
    return x * lax.rsqrt(ms + RMS_EPS) * g


def _project_rows(h, w_ref, q_ref, k_ref, v_ref, ga_ref, qi_ref, ik_ref, aux_ref):
    rows = h.shape[0]

    def mm(r0, n):
        return _dot_nt(h, w_ref[r0:r0 + n, :])

    q_ref[...] = mm(OFF_Q, D_ATTN).astype(BF16)
    kk = mm(OFF_K, KV_DIM)
    vv = mm(OFF_V, KV_DIM)
    for n in range(N_KV_HEADS):
        k_ref[pl.ds(n, rows, stride=N_KV_HEADS), :] = kk[:, n * HEAD_DIM:(n + 1) * HEAD_DIM]
        v_ref[pl.ds(n, rows, stride=N_KV_HEADS), :] = vv[:, n * HEAD_DIM:(n + 1) * HEAD_DIM]
    ga_ref[...] = mm(OFF_GA, D_ATTN)
    qi_ref[...] = mm(OFF_QI, N_IDX_HEADS * IDX_DIM).astype(BF16)
    kw = mm(OFF_KW, LANES)
    aux_ref[...] = kw
    ik_ref[...] = kw[:, :IDX_DIM]
    return kk, vv, kw


def _conv_chunk(h, w_ref, c):
    def mm(off):
        return _dot_nt(h, w_ref[off + c:off + c + CONV_COLS, :])

    return mm(OFF_B), mm(OFF_C) * mm(OFF_HC), mm(OFF_GB)


def _conv_out(bg, gb, u, um1, um2, wc_ref, c):
    w0 = wc_ref[0:1, c:c + CONV_COLS]
    w1 = wc_ref[1:2, c:c + CONV_COLS]
    w2 = wc_ref[2:3, c:c + CONV_COLS]
    conv = w0 * um2 + w1 * um1 + w2 * u
    return (bg * conv * _silu(gb)).astype(BF16)


def _project_prompt_kernel(x_ref, g_ref, w_ref, wc_ref, q_ref, k_ref, v_ref, ga_ref, qi_ref, ik_ref,
                           aux_ref, zc_ref, kpad_ref, kb_ref, vt_ref, cs_ref, upad_ref, *, tm):
    @pl.when(pl.program_id(1) == 0)
    def _():
        upad_ref[0:PAD_ROWS, :] = jnp.zeros((PAD_ROWS, D_CONV), F32)

    h = _rmsnorm(x_ref[...], g_ref[...]).astype(BF16)
    kk, vv, kw = _project_rows(h, w_ref, q_ref, k_ref, v_ref, ga_ref, qi_ref, ik_ref, aux_ref)
    kb_ref[...] = kk.astype(BF16)
    vt_ref[...] = vv.T.astype(BF16)
    lane = lax.broadcasted_iota(I32, kw.shape, 1)
    lo = jnp.where(lane < IDX_DIM, kw, 0.0)
    hi = pltpu.roll(lo, IDX_DIM, axis=1)
    zero = jnp.zeros_like(lo)
    for i, blk in enumerate((lo, zero, hi, zero, zero, lo, zero, hi)):
        kpad_ref[:, i * LANES:(i + 1) * LANES] = blk.astype(BF16)

    for c in range(0, D_CONV, CONV_COLS):
        bg, u, gb = _conv_chunk(h, w_ref, c)
        upad_ref[PAD_ROWS:PAD_ROWS + tm, c:c + CONV_COLS] = u
        um1 = upad_ref[PAD_ROWS - 1:PAD_ROWS - 1 + tm, c:c + CONV_COLS]
        um2 = upad_ref[PAD_ROWS - 2:PAD_ROWS - 2 + tm, c:c + CONV_COLS]
        zc_ref[:, c:c + CONV_COLS] = _conv_out(bg, gb, u, um1, um2, wc_ref, c)
    last = upad_ref[PAD_ROWS + tm - (CONV_WIDTH - 1):PAD_ROWS + tm, :]
    cs_ref[...] = last
    upad_ref[PAD_ROWS - (CONV_WIDTH - 1):PAD_ROWS, :] = last


def _project_sample_kernel(x_ref, g_ref, w_ref, wc_ref, st_ref, q_ref, k_ref, v_ref, ga_ref, qi_ref,
                           ik_ref, aux_ref, zc_ref, cs_ref, upad_ref, *, nseq, seqlen):
    rows = nseq * seqlen
    upad_ref[:, PAD_ROWS - (CONV_WIDTH - 1):PAD_ROWS, :] = st_ref[...]
    h = _rmsnorm(x_ref[...], g_ref[...]).astype(BF16)
    _project_rows(h, w_ref, q_ref, k_ref, v_ref, ga_ref, qi_ref, ik_ref, aux_ref)
    for c in range(0, D_CONV, CONV_COLS):
        bg, u, gb = _conv_chunk(h, w_ref, c)
        upad_ref[:, PAD_ROWS:PAD_ROWS + seqlen, c:c + CONV_COLS] = u.reshape(nseq, seqlen, CONV_COLS)
        um1 = upad_ref[:, PAD_ROWS - 1:PAD_ROWS - 1 + seqlen, c:c + CONV_COLS].reshape(rows, CONV_COLS)
        um2 = upad_ref[:, PAD_ROWS - 2:PAD_ROWS - 2 + seqlen, c:c + CONV_COLS].reshape(rows, CONV_COLS)
        zc_ref[:, c:c + CONV_COLS] = _conv_out(bg, gb, u, um1, um2, wc_ref, c)
    cs_ref[...] = upad_ref[:, PAD_ROWS + seqlen - (CONV_WIDTH - 1):PAD_ROWS + seqlen, :]


_PROJECT_OUTS = ((1, D_ATTN, BF16), (N_KV_HEADS, HEAD_DIM, F32), (N_KV_HEADS, HEAD_DIM, F32),
                 (1, D_ATTN, F32), (1, N_IDX_HEADS * IDX_DIM, BF16), (1, IDX_DIM, F32), (1, LANES, F32),
                 (1, D_CONV, BF16))


def _project_out_shapes(lead, rows):
    return [jax.ShapeDtypeStruct(lead + (m * rows, n), dt) for m, n, dt in _PROJECT_OUTS]


def _resident(shape, ngrid):
    zeros = (0,) * len(shape)
    if ngrid == 1:
        return pl.BlockSpec(shape, lambda i: zeros, pipeline_mode=pl.Buffered(1))
    return pl.BlockSpec(shape, lambda b, i: zeros, pipeline_mode=pl.Buffered(1))


def _project_prompt(x, g_pre, w_t, w_conv, tm):
    nb, t, _ = x.shape
    grid = (nb, t // tm)
    row_spec = lambda n, m=1: pl.BlockSpec((None, m * tm, n), lambda b, i: (b, i, 0))
    out_shapes = _project_out_shapes((nb,), t) + [
        jax.ShapeDtypeStruct((nb, t, N_QUADS * QUAD_LANES), BF16),
        jax.ShapeDtypeStruct((nb, t, KV_DIM), BF16),
        jax.ShapeDtypeStruct((nb, t // tm, KV_DIM, tm), BF16),
        jax.ShapeDtypeStruct((nb, CONV_WIDTH - 1, D_CONV), F32)]
    out_specs = [row_spec(n, m) for m, n, _ in _PROJECT_OUTS]
    out_specs += [row_spec(N_QUADS * QUAD_LANES), row_spec(KV_DIM),
                  pl.BlockSpec((None, None, KV_DIM, tm), lambda b, i: (b, i, 0, 0)),
                  pl.BlockSpec((None, CONV_WIDTH - 1, D_CONV), lambda b, i: (b, 0, 0))]
    return pl.pallas_call(
        functools.partial(_project_prompt_kernel, tm=tm),
        grid=grid,
        in_specs=[row_spec(D_MODEL), _resident((1, D_MODEL), 2), _resident((D_PROJ, D_MODEL), 2),
                  _resident((CONV_WIDTH, D_CONV), 2)],
        out_specs=out_specs,
        out_shape=out_shapes,
        scratch_shapes=[pltpu.VMEM((PAD_ROWS + tm, D_CONV), F32)],
        compiler_params=pltpu.CompilerParams(dimension_semantics=("arbitrary", "arbitrary"),
                                             vmem_limit_bytes=VMEM_LIMIT_BYTES),
        name="project_prompt",
    )(x, g_pre, w_t, w_conv)


def _project_sample(x, g_pre, w_t, w_conv, state):
    nseq, seqlen, _ = x.shape
    rows = nseq * seqlen
    full = lambda shape: pl.BlockSpec(shape, lambda i: (0,) * len(shape))
    out_shapes = _project_out_shapes((), rows) + [jax.ShapeDtypeStruct((nseq, CONV_WIDTH - 1, D_CONV), F32)]
    out_specs = [full((m * rows, n)) for m, n, _ in _PROJECT_OUTS] + [full((nseq, CONV_WIDTH - 1, D_CONV))]
    return pl.pallas_call(
        functools.partial(_project_sample_kernel, nseq=nseq, seqlen=seqlen),
        grid=(1,),
        in_specs=[full((rows, D_MODEL)), full((1, D_MODEL)), _resident((D_PROJ, D_MODEL), 1),
                  full((CONV_WIDTH, D_CONV)), full((nseq, CONV_WIDTH - 1, D_CONV))],
        out_specs=out_specs,
        out_shape=out_shapes,
        scratch_shapes=[pltpu.VMEM((nseq, PAD_ROWS + seqlen, D_CONV), F32)],
        compiler_params=pltpu.CompilerParams(dimension_semantics=("arbitrary",),
                                             vmem_limit_bytes=VMEM_LIMIT_BYTES),
        name="project_sample",
    )(x.reshape(rows, D_MODEL), g_pre, w_t, w_conv, state)


def _store_keys(score, admissible, keys_scr, hi_scr, lo_scr, kt):
    bits = lax.bitcast_convert_type(score, I32)
    key = bits ^ ((bits >> 31) & INT_MAX)
    key = jnp.where(admissible, key, INT_MIN)
    keys_scr[kt] = key
    hi_scr[kt] = (key >> HALF_BITS).astype(I16)
    lo_scr[kt] = ((key & HALF_MASK) + HALF_MIN).astype(I16)


def _topk_threshold(for_tiles, keys_scr, hi_scr, lo_scr, thr_scr, *, tk, width, n_cols):
    def count(pred_fn):
        def body(kt, c8):
            p = pred_fn(kt, keys_scr[kt]).astype(I32)
            for r in range(0, tk, SUBLANES):
                c8 = c8 + p[r:r + SUBLANES, :]
            return c8

        c8 = for_tiles(body, jnp.zeros((SUBLANES, width), I32))
        return jnp.sum(c8, axis=0, keepdims=True)

    def count_half(ref, th):
        def body(kt, c16):
            p = jnp.where(ref[kt] >= th, jnp.int16(1), jnp.int16(0))
            for r in range(0, tk, I16_ROWS):
                c16 = c16 + p[r:r + I16_ROWS, :]
            return c16

        c16 = for_tiles(body, jnp.zeros((I16_ROWS, width), I16))
        return jnp.sum(c16.astype(I32), axis=0, keepdims=True)

    def bisect_half(ref):
        def step(_, carry):
            lo, hi = carry
            mid = (lo + hi + 1) >> 1
            cnt = count_half(ref, jnp.minimum(mid, HALF_MAX).astype(I16))
            ge = (cnt >= TOPK_MAX) & (mid <= HALF_MAX)
            return jnp.where(ge, mid, lo), jnp.where(ge, hi, mid)

        lo0 = jnp.full((1, width), HALF_MIN, I32)
        hi0 = jnp.full((1, width), HALF_MAX + 1, I32)
        return lax.fori_loop(0, HALF_BITS, step, (lo0, hi0))[0]

    top = bisect_half(hi_scr)
    top16 = top.astype(I16)

    def narrow(kt, carry):
        h = hi_scr[kt]
        lo_scr[kt] = jnp.where(h == top16, lo_scr[kt],
                               jnp.where(h > top16, jnp.int16(HALF_MAX), jnp.int16(HALF_MIN)))
        return carry

    for_tiles(narrow, 0)
    bot = bisect_half(lo_scr)
    thr = top * (HALF_MASK + 1) + (bot - HALF_MIN)
    thr_scr[...] = jnp.broadcast_to(thr, (SUBLANES, width))
    excess = count_half(lo_scr, bot.astype(I16)) - TOPK_MAX

    @pl.when(jnp.max(excess.astype(F32)) > 0.0)
    def _():
        need = TOPK_MAX - count(lambda kt, t: t > thr)

        def pos_of(kt):
            return kt * tk + lax.broadcasted_iota(I32, (tk, width), 0)

        def pos_bisect(_, carry):
            plo, phi = carry
            mid = (plo + phi) >> 1
            ok = count(lambda kt, t: (t == thr) & (pos_of(kt) <= mid)) >= need
            return jnp.where(ok, plo, mid), jnp.where(ok, mid, phi)

        plo0 = jnp.full((1, width), -1, I32)
        phi0 = jnp.full((1, width), n_cols - 1, I32)
        steps = max(1, (n_cols - 1).bit_length()) + 1
        _, pos = lax.fori_loop(0, steps, pos_bisect, (plo0, phi0))

        def drop(kt, carry):
            t = keys_scr[kt]
            keys_scr[kt] = jnp.where((t == thr) & (pos_of(kt) > pos), INT_MIN, t)
            return carry

        for_tiles(drop, 0)


def _softmax_step(lg, m_ref, l_ref, acc_ref, pv_fn):
    m_prev = m_ref[0:1, :]
    m_new = jnp.maximum(m_prev, jnp.max(lg, axis=0, keepdims=True))
    m_safe = jnp.where(m_new == NEG_INF, 0.0, m_new)
    alpha = jnp.exp2(m_prev - m_safe)
    p = jnp.exp2(lg - m_safe)
    l_new = alpha * l_ref[0:1, :] + jnp.sum(p, axis=0, keepdims=True)
    acc_ref[...] = acc_ref[...] * alpha + pv_fn(p.astype(BF16))
    m_ref[...] = jnp.broadcast_to(m_new, m_ref.shape)
    l_ref[...] = jnp.broadcast_to(l_new, l_ref.shape)


def _attend_prompt_kernel(qi_ref, aux_ref, q_ref, ga_ref, kpad_ref, k_ref, vt_ref, z_ref, keys_scr,
                          hi_scr, lo_scr, thr_scr, m_scr, l_scr, acc_scr, lg_scr, *, tq, tk, n_cols):
    j = pl.program_id(1)
    n_tiles = j + 1

    def for_tiles(fn, init):
        return lax.fori_loop(0, n_tiles, fn, init)

    assert tq == tk == TOPK_MAX and tq % CHUNK == 0
    qi = qi_ref[...]
    qstack = jnp.concatenate([qi[:, u * QUAD_LANES:(u + 1) * QUAD_LANES] for u in range(N_QUADS)], axis=0)
    w_t = aux_ref[...].T
    w_rows = [w_t[IDX_DIM + h:IDX_DIM + h + 1, :] for h in range(N_IDX_HEADS)]
    key_chunk = lax.broadcasted_iota(I32, (tk, tq), 0) // CHUNK
    qry_chunk = lax.broadcasted_iota(I32, (tk, tq), 1) // CHUNK
    diag_adm = key_chunk <= qry_chunk

    def idx_tile(kt, carry):
        start = pl.multiple_of(kt * tk, tk)
        acc = jnp.zeros((tk, tq), F32)
        for c in range(HEADS_PER_QUAD):
            kp = kpad_ref[pl.ds(start, tk), c * QUAD_LANES:(c + 1) * QUAD_LANES]
            s = _dot_nt(kp, qstack)
            for u in range(N_QUADS):
                acc = acc + jnp.maximum(s[:, u * tq:(u + 1) * tq], 0.0) * w_rows[HEADS_PER_QUAD * u + c]
        _store_keys(acc * IDX_SCALE, (diag_adm & (kt == j)) | (kt < j), keys_scr, hi_scr, lo_scr, kt)
        return carry

    for_tiles(idx_tile, 0)

    thr_scr[...] = jnp.full((SUBLANES, tq), INT_MIN + 1, I32)

    @pl.when(j >= 1)
    def _():
        _topk_threshold(for_tiles, keys_scr, hi_scr, lo_scr, thr_scr, tk=tk, width=tq, n_cols=n_cols)

    thr = thr_scr[0:1, :]

    q = q_ref[...]
    qn = [jnp.concatenate([q[:, (KV_GROUP * n + g) * HEAD_DIM:(KV_GROUP * n + g + 1) * HEAD_DIM]
                           for g in range(KV_GROUP)], axis=0) for n in range(N_KV_HEADS)]
    m_scr[...] = jnp.full(m_scr.shape, NEG_INF, F32)
    l_scr[...] = jnp.zeros(l_scr.shape, F32)
    acc_scr[...] = jnp.zeros(acc_scr.shape, F32)

    def att_tile(kt, carry):
        start = pl.multiple_of(kt * tk, tk)
        sel = keys_scr[kt] >= thr
        for n in range(N_KV_HEADS):
            lg_scr[n] = _dot_nt(k_ref[pl.ds(start, tk), n * HEAD_DIM:(n + 1) * HEAD_DIM], qn[n])
        for n in range(N_KV_HEADS):
            lg = lg_scr[n] * ATT_SCALE_LOG2
            lg = jnp.concatenate([jnp.where(sel, lg[:, g * tq:(g + 1) * tq], NEG_INF)
                                  for g in range(KV_GROUP)], axis=1)
            vt_n = vt_ref[kt, n * HEAD_DIM:(n + 1) * HEAD_DIM, :]
            _softmax_step(lg, m_scr.at[n], l_scr.at[n], acc_scr.at[n],
                          lambda p: jnp.dot(vt_n, p, preferred_element_type=F32))
        return carry

    for_tiles(att_tile, 0)

    outs = []
    for n in range(N_KV_HEADS):
        o = acc_scr[n] / l_scr[n][0:1, :]
        outs.extend(o[:, g * tq:(g + 1) * tq].T for g in range(KV_GROUP))
    z_ref[...] = (jnp.concatenate(outs, axis=1) * _silu(ga_ref[...])).astype(BF16)


def _attend_prompt(qi, aux, q, ga, kpad, kb, vt, tq):
    nb, t, _ = q.shape
    tk = tq
    n_tiles = t // tk
    lanes_q = KV_GROUP * tq
    blk = lambda n: pl.BlockSpec((None, tq, n), lambda b, j: (b, j, 0))
    seq = lambda n: pl.BlockSpec((None, t, n), lambda b, j: (b, 0, 0))
    scratch = [pltpu.VMEM((n_tiles, tk, tq), I32), pltpu.VMEM((n_tiles, tk, tq), I16),
               pltpu.VMEM((n_tiles, tk, tq), I16), pltpu.VMEM((SUBLANES, tq), I32),
               pltpu.VMEM((N_KV_HEADS, SUBLANES, lanes_q), F32),
               pltpu.VMEM((N_KV_HEADS, SUBLANES, lanes_q), F32),
               pltpu.VMEM((N_KV_HEADS, HEAD_DIM, lanes_q), F32),
               pltpu.VMEM((N_KV_HEADS, tk, lanes_q), F32)]
    return pl.pallas_call(
        functools.partial(_attend_prompt_kernel, tq=tq, tk=tk, n_cols=t),
        grid=(nb, t // tq),
        in_specs=[blk(N_IDX_HEADS * IDX_DIM), blk(LANES), blk(D_ATTN), blk(D_ATTN),
                  seq(N_QUADS * QUAD_LANES), seq(KV_DIM),
                  pl.BlockSpec((None, n_tiles, KV_DIM, tk), lambda b, j: (b, 0, 0, 0))],
        out_specs=blk(D_ATTN),
        out_shape=jax.ShapeDtypeStruct((nb, t, D_ATTN), BF16),
        scratch_shapes=scratch,
        compiler_params=pltpu.CompilerParams(dimension_semantics=("arbitrary", "arbitrary"),
                                             vmem_limit_bytes=VMEM_LIMIT_BYTES),
        name="attend_prompt",
    )(qi, aux, q, ga, kpad, kb, vt)


def _attend_sample_kernel(qi_ref, aux_ref, q_ref, ga_ref, cik_ref, ck_ref, cv_ref, nk_ref, nv_ref,
                          z_ref, ki_s, k_s, vt_s, keys_scr, hi_scr, lo_scr, thr_scr, m_scr, l_scr,
                          acc_scr, *, tq, tk, past, n_tiles):
    n_keys = past + tq
    hl = N_IDX_HEADS * tq
    assert N_HEADS * tq == LANES and hl == 2 * LANES and past % tk == 0 and tq <= tk
    assert n_keys > TOPK_MAX and past % CHUNK == 0 and tq <= CHUNK

    def for_tiles(fn, init):
        for kt in range(n_tiles):
            init = fn(kt, init)
        return init

    aux = aux_ref[...]
    ki_s[0:past, :] = cik_ref[...].astype(BF16)
    ki_s[past:past + tq, :] = aux[:, :IDX_DIM].astype(BF16)
    ki_s[past + tq:, :] = jnp.zeros((n_tiles * tk - n_keys, IDX_DIM), BF16)
    k_s[past + tq:, :] = jnp.zeros((n_tiles * tk - n_keys, KV_DIM), BF16)
    vt_s[n_tiles - 1] = jnp.zeros((KV_DIM, tk), BF16)
    for n in range(N_KV_HEADS):
        cols = slice(n * HEAD_DIM, (n + 1) * HEAD_DIM)
        k_s[0:past, cols] = ck_ref[pl.ds(n, past, stride=N_KV_HEADS), :].astype(BF16)
        k_s[past:past + tq, cols] = nk_ref[pl.ds(n, tq, stride=N_KV_HEADS), :].astype(BF16)
        for kt in range(past // tk):
            v_tile = cv_ref[pl.ds(N_KV_HEADS * kt * tk + n, tk, stride=N_KV_HEADS), :]
            vt_s[kt, cols, :] = v_tile.T.astype(BF16)
        vt_s[n_tiles - 1, cols, 0:tq] = nv_ref[pl.ds(n, tq, stride=N_KV_HEADS), :].T.astype(BF16)

    qi = qi_ref[...]
    qrows = jnp.concatenate([qi[:, h * IDX_DIM:(h + 1) * IDX_DIM] for h in range(N_IDX_HEADS)], axis=0)
    src = lax.broadcasted_iota(I32, (LANES, hl), 0) - IDX_DIM
    dst_head = lax.broadcasted_iota(I32, (LANES, hl), 1) // tq
    spread = (src == dst_head).astype(BF16)
    w_hi = aux.astype(BF16)
    rest = aux - w_hi.astype(F32)
    w_mid = rest.astype(BF16)
    w_lo = (rest - w_mid.astype(F32)).astype(BF16)
    w_by_head = sum(jnp.dot(part, spread, preferred_element_type=F32) for part in (w_hi, w_mid, w_lo))
    own = lax.broadcasted_iota(I32, (tq, hl), 1) % tq == lax.broadcasted_iota(I32, (tq, hl), 0)
    w_lanes = jnp.sum(jnp.where(own, w_by_head, 0.0), axis=0, keepdims=True)

    def idx_tile(kt, carry):
        y = jnp.maximum(_dot_nt(ki_s[kt * tk:(kt + 1) * tk, :], qrows), 0.0) * w_lanes
        y = y[:, :LANES] + y[:, LANES:]
        shift = LANES // 2
        while shift >= tq:
            y = y + pltpu.roll(y, shift, axis=1)
            shift //= 2
        pos = kt * tk + lax.broadcasted_iota(I32, (tk, LANES), 0)
        _store_keys(y * IDX_SCALE, pos < n_keys, keys_scr, hi_scr, lo_scr, kt)
        return carry

    for_tiles(idx_tile, 0)

    _topk_threshold(for_tiles, keys_scr, hi_scr, lo_scr, thr_scr, tk=tk, width=LANES, n_cols=n_tiles * tk)
    thr = thr_scr[0:1, :]

    q = q_ref[...]
    zeros = jnp.zeros((tq, HEAD_DIM), BF16)
    qblk = jnp.concatenate(
        [jnp.concatenate([q[:, (KV_GROUP * n + g) * HEAD_DIM:(KV_GROUP * n + g + 1) * HEAD_DIM]
                          if m == n else zeros for m in range(N_KV_HEADS)], axis=1)
         for n in range(N_KV_HEADS) for g in range(KV_GROUP)], axis=0)
    m_scr[...] = jnp.full(m_scr.shape, NEG_INF, F32)
    l_scr[...] = jnp.zeros(l_scr.shape, F32)
    acc_scr[...] = jnp.zeros(acc_scr.shape, F32)

    def att_tile(kt, carry):
        lg = _dot_nt(k_s[kt * tk:(kt + 1) * tk, :], qblk) * ATT_SCALE_LOG2
        lg = jnp.where(keys_scr[kt] >= thr, lg, NEG_INF)
        _softmax_step(lg, m_scr, l_scr, acc_scr, lambda p: jnp.dot(vt_s[kt], p, preferred_element_type=F32))
        return carry

    for_tiles(att_tile, 0)

    o_t = (acc_scr[...] / l_scr[0:1, :]).T
    outs = [o_t[(KV_GROUP * n + g) * tq:(KV_GROUP * n + g + 1) * tq, n * HEAD_DIM:(n + 1) * HEAD_DIM]
            for n in range(N_KV_HEADS) for g in range(KV_GROUP)]
    z_ref[...] = (jnp.concatenate(outs, axis=1) * _silu(ga_ref[...])).astype(BF16)


def _attend_sample(qi, aux, q, ga, cache_ik, cache_k, cache_v, new_k, new_v, tk):
    nseq, tq, _ = q.shape
    past = cache_ik.shape[1]
    n_tiles = -(-(past + tq) // tk)
    blk = lambda n, m=1: pl.BlockSpec((None, m * tq, n), lambda b: (b, 0, 0))
    cache = lambda n, m=1: pl.BlockSpec((None, m * past, n), lambda b: (b, 0, 0))
    scratch = [pltpu.VMEM((n_tiles * tk, IDX_DIM), BF16), pltpu.VMEM((n_tiles * tk, KV_DIM), BF16),
               pltpu.VMEM((n_tiles, KV_DIM, tk), BF16),
               pltpu.VMEM((n_tiles, tk, LANES), I32), pltpu.VMEM((n_tiles, tk, LANES), I16),
               pltpu.VMEM((n_tiles, tk, LANES), I16), pltpu.VMEM((SUBLANES, LANES), I32),
               pltpu.VMEM((SUBLANES, LANES), F32), pltpu.VMEM((SUBLANES, LANES), F32),
               pltpu.VMEM((KV_DIM, LANES), F32)]
    return pl.pallas_call(
        functools.partial(_attend_sample_kernel, tq=tq, tk=tk, past=past, n_tiles=n_tiles),
        grid=(nseq,),
        in_specs=[blk(N_IDX_HEADS * IDX_DIM), blk(LANES), blk(D_ATTN), blk(D_ATTN), cache(IDX_DIM),
                  cache(HEAD_DIM, N_KV_HEADS), cache(HEAD_DIM, N_KV_HEADS), blk(HEAD_DIM, N_KV_HEADS),
                  blk(HEAD_DIM, N_KV_HEADS)],
        out_specs=blk(D_ATTN),
        out_shape=jax.ShapeDtypeStruct((nseq, tq, D_ATTN), BF16),
        scratch_shapes=scratch,
        compiler_params=pltpu.CompilerParams(dimension_semantics=("arbitrary",),
                                             vmem_limit_bytes=VMEM_LIMIT_BYTES),
        name="attend_sample",
    )(qi, aux, q, ga, cache_ik, cache_k, cache_v, new_k, new_v)


def _merge_kernel(x_ref, za_ref, zc_ref, w_ref, g_ref, y_ref):
    z = jnp.concatenate([za_ref[...], zc_ref[...]], axis=1)
    y = jnp.dot(z, w_ref[...], preferred_element_type=F32)
    y_ref[...] = x_ref[...] + _rmsnorm(y, g_ref[...])


def _merge(x, za, zc, w_out, g_post, tm):
    rows = x.shape[0]
    row_spec = lambda n: pl.BlockSpec((tm, n), lambda i: (i, 0))
    return pl.pallas_call(
        _merge_kernel,
        grid=(rows // tm,),
        in_specs=[row_spec(D_MODEL), row_spec(D_ATTN), row_spec(D_CONV), _resident((D_MODEL, D_MODEL), 1),
                  _resident((1, D_MODEL), 1)],
        out_specs=row_spec(D_MODEL),
        out_shape=jax.ShapeDtypeStruct((rows, D_MODEL), F32),
        compiler_params=pltpu.CompilerParams(dimension_semantics=("arbitrary",),
                                             vmem_limit_bytes=VMEM_LIMIT_BYTES),
        name="merge",
    )(x, za, zc, w_out, g_post)


PROJECT_ROWS = 256
ATTEND_ROWS = 256
MERGE_ROWS = 512


def _layer(xp, xs, cache_k, cache_v, cache_ik, state, g_pre, w_in, w_conv, w_out, g_post):
    nb, t, _ = xp.shape
    nseq, seqlen, _ = xs.shape
    g_pre = g_pre.reshape(1, D_MODEL)
    g_post = g_post.reshape(1, D_MODEL)
    w_t = w_in.T.astype(BF16)
    w_out = w_out.astype(BF16)

    assert PROJECT_ROWS == ATTEND_ROWS
    q, k, v, ga, qi, ik, aux, zc, kpad, kb, vt, cs = _project_prompt(xp, g_pre, w_t, w_conv, PROJECT_ROWS)
    za = _attend_prompt(qi, aux, q, ga, kpad, kb, vt, ATTEND_ROWS)
    yp = _merge(xp.reshape(nb * t, D_MODEL), za.reshape(nb * t, D_ATTN), zc.reshape(nb * t, D_CONV),
                w_out, g_post, MERGE_ROWS).reshape(nb, t, D_MODEL)

    sq, sk, sv, sga, sqi, sik, saux, szc, scs = _project_sample(xs, g_pre, w_t, w_conv, state)
    per_seq = lambda a: a.reshape(nseq, -1, a.shape[-1])
    kv_rows = lambda a: a.reshape(nseq, -1, HEAD_DIM)
    sza = _attend_sample(per_seq(sqi), per_seq(saux), per_seq(sq), per_seq(sga), cache_ik,
                         kv_rows(cache_k), kv_rows(cache_v), per_seq(sk), per_seq(sv), ATTEND_ROWS)
    ys = _merge(xs.reshape(nseq * seqlen, D_MODEL), sza.reshape(nseq * seqlen, D_ATTN), szc, w_out,
                g_post, MERGE_ROWS).reshape(nseq, seqlen, D_MODEL)

    heads = lambda a, lead: a.reshape(lead + (N_KV_HEADS, HEAD_DIM))
    return (yp, ys, heads(k, (nb, t)), heads(v, (nb, t)), ik, cs,
            heads(sk, (nseq, seqlen)), heads(sv, (nseq, seqlen)), per_seq(sik), scs)


def kernel(x_prompt, x_sample, cache_k, cache_v, cache_idx_k, state_conv, g_pre, w_in, w_conv, w_out,
           g_post):
    depth = g_pre.shape[0]
    xp, xs = x_prompt, x_sample
    outs = []
    for l in range(depth):
        res = _layer(xp, xs, cache_k[l], cache_v[l], cache_idx_k[l], state_conv[l], g_pre[l], w_in[l],
                     w_conv[l], w_out[l], g_post[l])
        xp, xs = res[0], res[1]
        outs.append(res[2:])
    stacked = [jnp.stack([o[i] for o in outs]) for i in range(8)]
    return (xp, xs) + tuple(stacked)
```

```python
import functools

import jax
import jax.numpy as jnp
from jax import lax
from jax.experimental import pallas as pl
from jax.experimental.pallas import tpu as pltpu

F32 = jnp.float32
BF16 = jnp.bfloat16
I32 = jnp.int32
I16 = jnp.int16

D_MODEL = 2048
D_ATTN = 1024
D_CONV = 1024
HEAD_DIM = 128
N_KV_HEADS = 2
KV_GROUP = 4
N_HEADS = N_KV_HEADS * KV_GROUP
KV_DIM = N_KV_HEADS * HEAD_DIM
N_IDX_HEADS = 16
IDX_DIM = 64
TOPK_MAX = 256
CHUNK = 64
CONV_WIDTH = 3
RMS_EPS = 1e-6

LANES = 128
SUBLANES = 8
I16_ROWS = 2 * SUBLANES
VMEM_LIMIT_BYTES = 60 * 1000 * 1024

OFF_Q = 0
OFF_K = OFF_Q + D_ATTN
OFF_V = OFF_K + KV_DIM
OFF_GA = OFF_V + KV_DIM
OFF_QI = OFF_GA + D_ATTN
OFF_KW = OFF_QI + N_IDX_HEADS * IDX_DIM
OFF_B = OFF_KW + IDX_DIM + N_IDX_HEADS
OFF_C = OFF_B + D_CONV
OFF_HC = OFF_C + D_CONV
OFF_GB = OFF_HC + D_CONV
D_PROJ = OFF_GB + D_CONV
assert all(o % I16_ROWS == 0 for o in (OFF_K, OFF_V, OFF_GA, OFF_QI, OFF_KW, OFF_B, OFF_C, OFF_HC, OFF_GB))

CONV_COLS = 256
PAD_ROWS = SUBLANES
HEADS_PER_QUAD = 4
QUAD_LANES = HEADS_PER_QUAD * IDX_DIM
N_QUADS = N_IDX_HEADS // HEADS_PER_QUAD
IDX_SCALE = (IDX_DIM ** -0.5) * (N_IDX_HEADS ** -0.5)
LOG2E = 1.4426950408889634
ATT_SCALE_LOG2 = HEAD_DIM ** -0.5 * LOG2E
INT_MAX = 2 ** 31 - 1
HALF_BITS = 16
HALF_MASK = 2 ** HALF_BITS - 1
HALF_MIN = -(2 ** (HALF_BITS - 1))
HALF_MAX = 2 ** (HALF_BITS - 1) - 1
KEY_POS_INF = 0x7F800000
KEY_NEG_INF = -KEY_POS_INF - 1
MAX_SEARCH_STEPS = 36
F32_LOWEST = -3.4028234663852886e38
COUNT_UNROLL = 4
NEG_INF = float("-inf")


def _silu(x):
    return x * jax.nn.sigmoid(x)


def _dot_nt(a, b):
    return lax.dot_general(a, b, (((1,), (1,)), ((), ())), preferred_element_type=F32)


def _rmsnorm(x, g):
    ms = jnp.mean(x * x, axis=-1, keepdims=True)
    return x * lax.rsqrt(ms + RMS_EPS) * g


def _project_rows(h, w_ref, q_ref, k_ref, v_ref, ga_ref, qi_ref, ik_ref, aux_ref):
    rows = h.shape[0]

    def mm(r0, n):
        return _dot_nt(h, w_ref[r0:r0 + n, :])

    q_ref[...] = (mm(OFF_Q, D_ATTN) * ATT_SCALE_LOG2).astype(BF16)
    kk = mm(OFF_K, KV_DIM)
    vv = mm(OFF_V, KV_DIM)
    for n in range(N_KV_HEADS):
        k_ref[pl.ds(n, rows, stride=N_KV_HEADS), :] = kk[:, n * HEAD_DIM:(n + 1) * HEAD_DIM]
        v_ref[pl.ds(n, rows, stride=N_KV_HEADS), :] = vv[:, n * HEAD_DIM:(n + 1) * HEAD_DIM]
    ga_ref[...] = mm(OFF_GA, D_ATTN)
    qi_ref[...] = mm(OFF_QI, N_IDX_HEADS * IDX_DIM).astype(BF16)
    kw = mm(OFF_KW, LANES)
    aux_ref[...] = kw
    ik_ref[...] = kw[:, :IDX_DIM]
    return kk, vv, kw


def _conv_chunk(h, w_ref, c):
    def mm(off):
        return _dot_nt(h, w_ref[off + c:off + c + CONV_COLS, :])

    return mm(OFF_B), mm(OFF_C) * mm(OFF_HC), mm(OFF_GB)


def _conv_out(bg, gb, u, um1, um2, wc_ref, c):
    w0 = wc_ref[0:1, c:c + CONV_COLS]
    w1 = wc_ref[1:2, c:c + CONV_COLS]
    w2 = wc_ref[2:3, c:c + CONV_COLS]
    conv = w0 * um2 + w1 * um1 + w2 * u
    return (bg * conv * _silu(gb)).astype(BF16)


def _project_prompt_kernel(x_ref, g_ref, w_ref, wc_ref, q_ref, k_ref, v_ref, ga_ref, qi_ref, ik_ref,
                           aux_ref, zc_ref, kpad_ref, kb_ref, vt_ref, cs_ref, upad_ref, *, tm):
    @pl.when(pl.program_id(1) == 0)
    def _():
        upad_ref[0:PAD_ROWS, :] = jnp.zeros((PAD_ROWS, D_CONV), F32)

    h = _rmsnorm(x_ref[...], g_ref[...]).astype(BF16)
    kk, vv, kw = _project_rows(h, w_ref, q_ref, k_ref, v_ref, ga_ref, qi_ref, ik_ref, aux_ref)
    kb_ref[...] = kk.astype(BF16)
    vt_ref[...] = vv.T.astype(BF16)
    lane = lax.broadcasted_iota(I32, kw.shape, 1)
    lo = jnp.where(lane < IDX_DIM, kw, 0.0)
    hi = pltpu.roll(lo, IDX_DIM, axis=1)
    zero = jnp.zeros_like(lo)
    for i, blk in enumerate((lo, zero, hi, zero, zero, lo, zero, hi)):
        kpad_ref[:, i * LANES:(i + 1) * LANES] = blk.astype(BF16)

    for c in range(0, D_CONV, CONV_COLS):
        bg, u, gb = _conv_chunk(h, w_ref, c)
        upad_ref[PAD_ROWS:PAD_ROWS + tm, c:c + CONV_COLS] = u
        um1 = upad_ref[PAD_ROWS - 1:PAD_ROWS - 1 + tm, c:c + CONV_COLS]
        um2 = upad_ref[PAD_ROWS - 2:PAD_ROWS - 2 + tm, c:c + CONV_COLS]
        zc_ref[:, c:c + CONV_COLS] = _conv_out(bg, gb, u, um1, um2, wc_ref, c)
    last = upad_ref[PAD_ROWS + tm - (CONV_WIDTH - 1):PAD_ROWS + tm, :]
    cs_ref[...] = last
    upad_ref[PAD_ROWS - (CONV_WIDTH - 1):PAD_ROWS, :] = last


def _project_sample_kernel(x_ref, g_ref, w_ref, wc_ref, st_ref, q_ref, k_ref, v_ref, ga_ref, qi_ref,
                           ik_ref, aux_ref, zc_ref, cs_ref, upad_ref, *, nseq, seqlen):
    rows = nseq * seqlen
    upad_ref[:, PAD_ROWS - (CONV_WIDTH - 1):PAD_ROWS, :] = st_ref[...]
    h = _rmsnorm(x_ref[...], g_ref[...]).astype(BF16)
    _project_rows(h, w_ref, q_ref, k_ref, v_ref, ga_ref, qi_ref, ik_ref, aux_ref)
    for c in range(0, D_CONV, CONV_COLS):
        bg, u, gb = _conv_chunk(h, w_ref, c)
        upad_ref[:, PAD_ROWS:PAD_ROWS + seqlen, c:c + CONV_COLS] = u.reshape(nseq, seqlen, CONV_COLS)
        um1 = upad_ref[:, PAD_ROWS - 1:PAD_ROWS - 1 + seqlen, c:c + CONV_COLS].reshape(rows, CONV_COLS)
        um2 = upad_ref[:, PAD_ROWS - 2:PAD_ROWS - 2 + seqlen, c:c + CONV_COLS].reshape(rows, CONV_COLS)
        zc_ref[:, c:c + CONV_COLS] = _conv_out(bg, gb, u, um1, um2, wc_ref, c)
    cs_ref[...] = upad_ref[:, PAD_ROWS + seqlen - (CONV_WIDTH - 1):PAD_ROWS + seqlen, :]


_PROJECT_OUTS = ((1, D_ATTN, BF16), (N_KV_HEADS, HEAD_DIM, F32), (N_KV_HEADS, HEAD_DIM, F32),
                 (1, D_ATTN, F32), (1, N_IDX_HEADS * IDX_DIM, BF16), (1, IDX_DIM, F32), (1, LANES, F32),
                 (1, D_CONV, BF16))


def _project_out_shapes(lead, rows):
    return [jax.ShapeDtypeStruct(lead + (m * rows, n), dt) for m, n, dt in _PROJECT_OUTS]


def _resident(shape, ngrid):
    zeros = (0,) * len(shape)
    if ngrid == 1:
        return pl.BlockSpec(shape, lambda i: zeros, pipeline_mode=pl.Buffered(1))
    return pl.BlockSpec(shape, lambda b, i: zeros, pipeline_mode=pl.Buffered(1))


def _project_prompt(x, g_pre, w_t, w_conv, tm):
    nb, t, _ = x.shape
    grid = (nb, t // tm)
    row_spec = lambda n, m=1: pl.BlockSpec((None, m * tm, n), lambda b, i: (b, i, 0))
    out_shapes = _project_out_shapes((nb,), t) + [
        jax.ShapeDtypeStruct((nb, t, N_QUADS * QUAD_LANES), BF16),
        jax.ShapeDtypeStruct((nb, t, KV_DIM), BF16),
        jax.ShapeDtypeStruct((nb, t // tm, KV_DIM, tm), BF16),
        jax.ShapeDtypeStruct((nb, CONV_WIDTH - 1, D_CONV), F32)]
    out_specs = [row_spec(n, m) for m, n, _ in _PROJECT_OUTS]
    out_specs += [row_spec(N_QUADS * QUAD_LANES), row_spec(KV_DIM),
                  pl.BlockSpec((None, None, KV_DIM, tm), lambda b, i: (b, i, 0, 0)),
                  pl.BlockSpec((None, CONV_WIDTH - 1, D_CONV), lambda b, i: (b, 0, 0))]
    return pl.pallas_call(
        functools.partial(_project_prompt_kernel, tm=tm),
        grid=grid,
        in_specs=[row_spec(D_MODEL), _resident((1, D_MODEL), 2), _resident((D_PROJ, D_MODEL), 2),
                  _resident((CONV_WIDTH, D_CONV), 2)],
        out_specs=out_specs,
        out_shape=out_shapes,
        scratch_shapes=[pltpu.VMEM((PAD_ROWS + tm, D_CONV), F32)],
        compiler_params=pltpu.CompilerParams(dimension_semantics=("arbitrary", "arbitrary"),
                                             vmem_limit_bytes=VMEM_LIMIT_BYTES),
        name="project_prompt",
    )(x, g_pre, w_t, w_conv)


def _project_sample(x, g_pre, w_t, w_conv, state):
    nseq, seqlen, _ = x.shape
    rows = nseq * seqlen
    full = lambda shape: pl.BlockSpec(shape, lambda i: (0,) * len(shape))
    out_shapes = _project_out_shapes((), rows) + [jax.ShapeDtypeStruct((nseq, CONV_WIDTH - 1, D_CONV), F32)]
    out_specs = [full((m * rows, n)) for m, n, _ in _PROJECT_OUTS] + [full((nseq, CONV_WIDTH - 1, D_CONV))]
    return pl.pallas_call(
        functools.partial(_project_sample_kernel, nseq=nseq, seqlen=seqlen),
        grid=(1,),
        in_specs=[full((rows, D_MODEL)), full((1, D_MODEL)), _resident((D_PROJ, D_MODEL), 1),
                  full((CONV_WIDTH, D_CONV)), full((nseq, CONV_WIDTH - 1, D_CONV))],
        out_specs=out_specs,
        out_shape=out_shapes,
        scratch_shapes=[pltpu.VMEM((nseq, PAD_ROWS + seqlen, D_CONV), F32)],
        compiler_params=pltpu.CompilerParams(dimension_semantics=("arbitrary",),
                                             vmem_limit_bytes=VMEM_LIMIT_BYTES),
        name="project_sample",
    )(x.reshape(rows, D_MODEL), g_pre, w_t, w_conv, state)


def _key_of_score(x):
    bits = lax.bitcast_convert_type(x, I32)
    return bits ^ ((bits >> 31) & INT_MAX)


def _score_of_key(key):
    return lax.bitcast_convert_type(key ^ ((key >> 31) & INT_MAX), F32)


def _store_scores(score, admissible, sc_scr, hi_scr, lo_scr, kt):
    score = jnp.where(admissible, score, NEG_INF)
    sc_scr[kt] = score
    key = _key_of_score(score)
    hi_scr[kt] = (key >> HALF_BITS).astype(I16)
    lo_scr[kt] = ((key & HALF_MASK) + HALF_MIN).astype(I16)


def _vreg_sum(p, rows):
    parts = [p[r:r + rows, :] for r in range(0, p.shape[0], rows)]
    while len(parts) > 1:
        parts = [a + b for a, b in zip(parts[::2], parts[1::2])] + parts[len(parts) & ~1:]
    return parts[0]


def _threshold_key_guess(for_tiles, hi_scr, lo_scr, width):
    def count_half(ref, th):
        def body(kt, c16):
            return c16 + _vreg_sum(jnp.where(ref[kt] >= th, jnp.int16(1), jnp.int16(0)), I16_ROWS)

        c16 = for_tiles(body, jnp.zeros((I16_ROWS, width), I16), COUNT_UNROLL)
        return jnp.sum(c16.astype(I32), axis=0, keepdims=True)

    def bisect_half(ref):
        def step(_, carry):
            lo, hi = carry
            mid = (lo + hi + 1) >> 1
            cnt = count_half(ref, jnp.minimum(mid, HALF_MAX).astype(I16))
            ge = (cnt >= TOPK_MAX) & (mid <= HALF_MAX)
            return jnp.where(ge, mid, lo), jnp.where(ge, hi, mid)

        lo0 = jnp.full((1, width), HALF_MIN, I32)
        hi0 = jnp.full((1, width), HALF_MAX + 1, I32)
        return lax.fori_loop(0, HALF_BITS, step, (lo0, hi0))[0]

    top = bisect_half(hi_scr)
    top16 = top.astype(I16)

    def narrow(kt, carry):
        h = hi_scr[kt]
        lo_scr[kt] = jnp.where(h == top16, lo_scr[kt],
                               jnp.where(h > top16, jnp.int16(HALF_MAX), jnp.int16(HALF_MIN)))
        return carry

    for_tiles(narrow, 0)
    bot = bisect_half(lo_scr)
    return top * (HALF_MASK + 1) + (bot - HALF_MIN)


def _topk_threshold(for_tiles, sc_scr, hi_scr, lo_scr, thr_scr, *, tk, width, n_cols):
    def count(pred_fn):
        def body(kt, c8):
            return c8 + _vreg_sum(pred_fn(kt, sc_scr[kt]).astype(I32), SUBLANES)

        c8 = for_tiles(body, jnp.zeros((SUBLANES, width), I32), COUNT_UNROLL)
        return jnp.sum(c8, axis=0, keepdims=True)

    hint_lo = _threshold_key_guess(for_tiles, hi_scr, lo_scr, width)
    hint_hi = hint_lo + 1

    def open_lanes(lo, hi):
        return jnp.max((hi > lo + 1).astype(F32)) > 0.0

    def cond(state):
        i, lo, hi, _ = state
        return (i < 2) | ((i < MAX_SEARCH_STEPS) & open_lanes(lo, hi))

    def step(state):
        i, lo, hi, n_lo = state
        mid = (lo >> 1) + (hi >> 1) + ((lo | hi) & 1)
        cand = jnp.where(i == 0, hint_lo, jnp.where(i == 1, hint_hi, mid))
        cand = jnp.minimum(jnp.maximum(cand, lo + 1), hi)
        cand_score = _score_of_key(cand)
        cnt = count(lambda kt, t: t >= cand_score)
        ge = cnt >= TOPK_MAX
        hit = cnt == TOPK_MAX
        lo_new = jnp.where(ge, cand, lo)
        hi_new = jnp.where(hit, cand + 1, jnp.where(ge, hi, cand))
        return i + 1, lo_new, hi_new, jnp.where(ge, cnt, n_lo)

    lo0 = jnp.full((1, width), KEY_NEG_INF, I32)
    hi0 = jnp.full((1, width), KEY_POS_INF + 1, I32)
    n0 = jnp.full((1, width), INT_MAX, I32)
    _, lo, _, n_lo = lax.while_loop(cond, step, (jnp.int32(0), lo0, hi0, n0))
    thr = _score_of_key(lo)
    thr_scr[...] = jnp.broadcast_to(thr, (SUBLANES, width))

    @pl.when(jnp.max((n_lo > TOPK_MAX).astype(F32)) > 0.0)
    def _():
        need = TOPK_MAX - count(lambda kt, t: t > thr)

        def pos_of(kt):
            return kt * tk + lax.broadcasted_iota(I32, (tk, width), 0)

        def pos_bisect(_, carry):
            plo, phi = carry
            mid = (plo + phi) >> 1
            ok = count(lambda kt, t: (t == thr) & (pos_of(kt) <= mid)) >= need
            return jnp.where(ok, plo, mid), jnp.where(ok, mid, phi)

        plo0 = jnp.full((1, width), -1, I32)
        phi0 = jnp.full((1, width), n_cols - 1, I32)
        steps = max(1, (n_cols - 1).bit_length()) + 1
        _, pos = lax.fori_loop(0, steps, pos_bisect, (plo0, phi0))

        def drop(kt, carry):
            t = sc_scr[kt]
            sc_scr[kt] = jnp.where((t == thr) & (pos_of(kt) > pos), NEG_INF, t)
            return carry

        for_tiles(drop, 0)


def _softmax_step(lg, m_ref, l_ref, acc_ref, pv_fn):
    m_prev = m_ref[0:1, :]
    m_new = jnp.maximum(m_prev, jnp.max(lg, axis=0, keepdims=True))
    m_safe = jnp.where(m_new == NEG_INF, 0.0, m_new)
    alpha = jnp.exp2(m_prev - m_safe)
    p = jnp.exp2(lg - m_safe)
    l_new = alpha * l_ref[0:1, :] + jnp.sum(p, axis=0, keepdims=True)
    acc_ref[...] = acc_ref[...] * alpha + pv_fn(p.astype(BF16))
    m_ref[...] = jnp.broadcast_to(m_new, m_ref.shape)
    l_ref[...] = jnp.broadcast_to(l_new, l_ref.shape)


def _attend_prompt_kernel(qi_ref, aux_ref, q_ref, ga_ref, kpad_ref, k_ref, vt_ref, z_ref, sc_scr,
                          hi_scr, lo_scr, thr_scr, m_scr, l_scr, acc_scr, lg_scr, *, tq, tk, n_cols):
    j = pl.program_id(1)
    n_tiles = j + 1

    def for_tiles(fn, init, unroll=2):
        shift = unroll.bit_length() - 1
        assert unroll == 1 << shift
        n_main = n_tiles >> shift

        def body(i, carry):
            for u in range(unroll):
                carry = fn(unroll * i + u, carry)
            return carry

        carry = lax.fori_loop(0, n_main, body, init)
        return lax.fori_loop(n_main << shift, n_tiles, fn, carry)

    assert tq == tk == TOPK_MAX and tq % CHUNK == 0
    qi = qi_ref[...]
    qstack = jnp.concatenate([qi[:, u * QUAD_LANES:(u + 1) * QUAD_LANES] for u in range(N_QUADS)], axis=0)
    w_t = aux_ref[...].T
    w_rows = [w_t[IDX_DIM + h:IDX_DIM + h + 1, :] for h in range(N_IDX_HEADS)]
    key_chunk = lax.broadcasted_iota(I32, (tk, tq), 0) // CHUNK
    qry_chunk = lax.broadcasted_iota(I32, (tk, tq), 1) // CHUNK
    diag_adm = key_chunk <= qry_chunk

    def idx_tile(kt, carry):
        start = pl.multiple_of(kt * tk, tk)
        acc = jnp.zeros((tk, tq), F32)
        for c in range(HEADS_PER_QUAD):
            kp = kpad_ref[pl.ds(start, tk), c * QUAD_LANES:(c + 1) * QUAD_LANES]
            s = _dot_nt(kp, qstack)
            for u in range(N_QUADS):
                acc = acc + jnp.maximum(s[:, u * tq:(u + 1) * tq], 0.0) * w_rows[HEADS_PER_QUAD * u + c]
        _store_scores(acc * IDX_SCALE, (diag_adm & (kt == j)) | (kt < j), sc_scr, hi_scr, lo_scr, kt)
        return carry

    for_tiles(idx_tile, 0)

    thr_scr[...] = jnp.full((SUBLANES, tq), F32_LOWEST, F32)

    @pl.when(j >= 1)
    def _():
        _topk_threshold(for_tiles, sc_scr, hi_scr, lo_scr, thr_scr, tk=tk, width=tq, n_cols=n_cols)

    thr = thr_scr[0:1, :]

    q = q_ref[...]
    qn = [jnp.concatenate([q[:, (KV_GROUP * n + g) * HEAD_DIM:(KV_GROUP * n + g + 1) * HEAD_DIM]
                           for g in range(KV_GROUP)], axis=0) for n in range(N_KV_HEADS)]
    m_scr[...] = jnp.full(m_scr.shape, NEG_INF, F32)
    l_scr[...] = jnp.zeros(l_scr.shape, F32)
    acc_scr[...] = jnp.zeros(acc_scr.shape, F32)

    def att_tile(kt, carry):
        start = pl.multiple_of(kt * tk, tk)
        sel = sc_scr[kt] >= thr
        for n in range(N_KV_HEADS):
            lg_scr[n] = _dot_nt(k_ref[pl.ds(start, tk), n * HEAD_DIM:(n + 1) * HEAD_DIM], qn[n])
        for n in range(N_KV_HEADS):
            lg = jnp.concatenate([jnp.where(sel, lg_scr[n, :, g * tq:(g + 1) * tq], NEG_INF)
                                  for g in range(KV_GROUP)], axis=1)
            vt_n = vt_ref[kt, n * HEAD_DIM:(n + 1) * HEAD_DIM, :]
            _softmax_step(lg, m_scr.at[n], l_scr.at[n], acc_scr.at[n],
                          lambda p: jnp.dot(vt_n, p, preferred_element_type=F32))
        return carry

    for_tiles(att_tile, 0)

    outs = []
    for n in range(N_KV_HEADS):
        o = acc_scr[n] / l_scr[n][0:1, :]
        outs.extend(o[:, g * tq:(g + 1) * tq].T for g in range(KV_GROUP))
    z_ref[...] = (jnp.concatenate(outs, axis=1) * _silu(ga_ref[...])).astype(BF16)


def _attend_prompt(qi, aux, q, ga, kpad, kb, vt, tq):
    nb, t, _ = q.shape
    tk = tq
    n_tiles = t // tk
    lanes_q = KV_GROUP * tq
    blk = lambda n: pl.BlockSpec((None, tq, n), lambda b, j: (b, j, 0))
    seq = lambda n: pl.BlockSpec((None, t, n), lambda b, j: (b, 0, 0))
    scratch = [pltpu.VMEM((n_tiles, tk, tq), F32), pltpu.VMEM((n_tiles, tk, tq), I16),
               pltpu.VMEM((n_tiles, tk, tq), I16), pltpu.VMEM((SUBLANES, tq), F32),
               pltpu.VMEM((N_KV_HEADS, SUBLANES, lanes_q), F32),
               pltpu.VMEM((N_KV_HEADS, SUBLANES, lanes_q), F32),
               pltpu.VMEM((N_KV_HEADS, HEAD_DIM, lanes_q), F32),
               pltpu.VMEM((N_KV_HEADS, tk, lanes_q), F32)]
    return pl.pallas_call(
        functools.partial(_attend_prompt_kernel, tq=tq, tk=tk, n_cols=t),
        grid=(nb, t // tq),
        in_specs=[blk(N_IDX_HEADS * IDX_DIM), blk(LANES), blk(D_ATTN), blk(D_ATTN),
                  seq(N_QUADS * QUAD_LANES), seq(KV_DIM),
                  pl.BlockSpec((None, n_tiles, KV_DIM, tk), lambda b, j: (b, 0, 0, 0))],
        out_specs=blk(D_ATTN),
        out_shape=jax.ShapeDtypeStruct((nb, t, D_ATTN), BF16),
        scratch_shapes=scratch,
        compiler_params=pltpu.CompilerParams(dimension_semantics=("arbitrary", "arbitrary"),
                                             vmem_limit_bytes=VMEM_LIMIT_BYTES),
        name="attend_prompt",
    )(qi, aux, q, ga, kpad, kb, vt)


def _attend_sample_kernel(qi_ref, aux_ref, q_ref, ga_ref, cik_ref, ck_ref, cv_ref, nk_ref, nv_ref,
                          z_ref, ki_s, k_s, vt_s, sc_scr, hi_scr, lo_scr, thr_scr, m_scr, l_scr, acc_scr,
                          *, tq, tk, past, n_tiles):
    n_keys = past + tq
    hl = N_IDX_HEADS * tq
    assert N_HEADS * tq == LANES and hl == 2 * LANES and past % tk == 0 and tq <= tk
    assert n_keys > TOPK_MAX and past % CHUNK == 0 and tq <= CHUNK

    def for_tiles(fn, init, unroll=None):
        for kt in range(n_tiles):
            init = fn(kt, init)
        return init

    aux = aux_ref[...]
    ki_s[0:past, :] = cik_ref[...].astype(BF16)
    ki_s[past:past + tq, :] = aux[:, :IDX_DIM].astype(BF16)
    ki_s[past + tq:, :] = jnp.zeros((n_tiles * tk - n_keys, IDX_DIM), BF16)
    k_s[past + tq:, :] = jnp.zeros((n_tiles * tk - n_keys, KV_DIM), BF16)
    vt_s[n_tiles - 1] = jnp.zeros((KV_DIM, tk), BF16)
    for n in range(N_KV_HEADS):
        cols = slice(n * HEAD_DIM, (n + 1) * HEAD_DIM)
        k_s[0:past, cols] = ck_ref[pl.ds(n, past, stride=N_KV_HEADS), :].astype(BF16)
        k_s[past:past + tq, cols] = nk_ref[pl.ds(n, tq, stride=N_KV_HEADS), :].astype(BF16)
        for kt in range(past // tk):
            v_tile = cv_ref[pl.ds(N_KV_HEADS * kt * tk + n, tk, stride=N_KV_HEADS), :]
            vt_s[kt, cols, :] = v_tile.T.astype(BF16)
        vt_s[n_tiles - 1, cols, 0:tq] = nv_ref[pl.ds(n, tq, stride=N_KV_HEADS), :].T.astype(BF16)

    qi = qi_ref[...]
    qrows = jnp.concatenate([qi[:, h * IDX_DIM:(h + 1) * IDX_DIM] for h in range(N_IDX_HEADS)], axis=0)
    src = lax.broadcasted_iota(I32, (LANES, hl), 0) - IDX_DIM
    dst_head = lax.broadcasted_iota(I32, (LANES, hl), 1) // tq
    spread = (src == dst_head).astype(BF16)
    w_hi = aux.astype(BF16)
    rest = aux - w_hi.astype(F32)
    w_mid = rest.astype(BF16)
    w_lo = (rest - w_mid.astype(F32)).astype(BF16)
    w_by_head = sum(jnp.dot(part, spread, preferred_element_type=F32) for part in (w_hi, w_mid, w_lo))
    own = lax.broadcasted_iota(I32, (tq, hl), 1) % tq == lax.broadcasted_iota(I32, (tq, hl), 0)
    w_lanes = jnp.sum(jnp.where(own, w_by_head, 0.0), axis=0, keepdims=True)

    def idx_tile(kt, carry):
        y = jnp.maximum(_dot_nt(ki_s[kt * tk:(kt + 1) * tk, :], qrows), 0.0) * w_lanes
        y = y[:, :LANES] + y[:, LANES:]
        shift = LANES // 2
        while shift >= tq:
            y = y + pltpu.roll(y, shift, axis=1)
            shift //= 2
        pos = kt * tk + lax.broadcasted_iota(I32, (tk, LANES), 0)
        _store_scores(y * IDX_SCALE, pos < n_keys, sc_scr, hi_scr, lo_scr, kt)
        return carry

    for_tiles(idx_tile, 0)

    _topk_threshold(for_tiles, sc_scr, hi_scr, lo_scr, thr_scr, tk=tk, width=LANES, n_cols=n_tiles * tk)
    thr = thr_scr[0:1, :]

    q = q_ref[...]
    zeros = jnp.zeros((tq, HEAD_DIM), BF16)
    qblk = jnp.concatenate(
        [jnp.concatenate([q[:, (KV_GROUP * n + g) * HEAD_DIM:(KV_GROUP * n + g + 1) * HEAD_DIM]
                          if m == n else zeros for m in range(N_KV_HEADS)], axis=1)
         for n in range(N_KV_HEADS) for g in range(KV_GROUP)], axis=0)
    m_scr[...] = jnp.full(m_scr.shape, NEG_INF, F32)
    l_scr[...] = jnp.zeros(l_scr.shape, F32)
    acc_scr[...] = jnp.zeros(acc_scr.shape, F32)

    def att_tile(kt, carry):
        lg = jnp.where(sc_scr[kt] >= thr, _dot_nt(k_s[kt * tk:(kt + 1) * tk, :], qblk), NEG_INF)
        _softmax_step(lg, m_scr, l_scr, acc_scr, lambda p: jnp.dot(vt_s[kt], p, preferred_element_type=F32))
        return carry

    for_tiles(att_tile, 0)

    o_t = (acc_scr[...] / l_scr[0:1, :]).T
    outs = [o_t[(KV_GROUP * n + g) * tq:(KV_GROUP * n + g + 1) * tq, n * HEAD_DIM:(n + 1) * HEAD_DIM]
            for n in range(N_KV_HEADS) for g in range(KV_GROUP)]
    z_ref[...] = (jnp.concatenate(outs, axis=1) * _silu(ga_ref[...])).astype(BF16)


def _attend_sample(qi, aux, q, ga, cache_ik, cache_k, cache_v, new_k, new_v, tk):
    nseq, tq, _ = q.shape
    past = cache_ik.shape[1]
    n_tiles = -(-(past + tq) // tk)
    blk = lambda n, m=1: pl.BlockSpec((None, m * tq, n), lambda b: (b, 0, 0))
    cache = lambda n, m=1: pl.BlockSpec((None, m * past, n), lambda b: (b, 0, 0))
    scratch = [pltpu.VMEM((n_tiles * tk, IDX_DIM), BF16), pltpu.VMEM((n_tiles * tk, KV_DIM), BF16),
               pltpu.VMEM((n_tiles, KV_DIM, tk), BF16),
               pltpu.VMEM((n_tiles, tk, LANES), F32), pltpu.VMEM((n_tiles, tk, LANES), I16),
               pltpu.VMEM((n_tiles, tk, LANES), I16),
               pltpu.VMEM((SUBLANES, LANES), F32),
               pltpu.VMEM((SUBLANES, LANES), F32), pltpu.VMEM((SUBLANES, LANES), F32),
               pltpu.VMEM((KV_DIM, LANES), F32)]
    return pl.pallas_call(
        functools.partial(_attend_sample_kernel, tq=tq, tk=tk, past=past, n_tiles=n_tiles),
        grid=(nseq,),
        in_specs=[blk(N_IDX_HEADS * IDX_DIM), blk(LANES), blk(D_ATTN), blk(D_ATTN), cache(IDX_DIM),
                  cache(HEAD_DIM, N_KV_HEADS), cache(HEAD_DIM, N_KV_HEADS), blk(HEAD_DIM, N_KV_HEADS),
                  blk(HEAD_DIM, N_KV_HEADS)],
        out_specs=blk(D_ATTN),
        out_shape=jax.ShapeDtypeStruct((nseq, tq, D_ATTN), BF16),
        scratch_shapes=scratch,
        compiler_params=pltpu.CompilerParams(dimension_semantics=("arbitrary",),
                                             vmem_limit_bytes=VMEM_LIMIT_BYTES),
        name="attend_sample",
    )(qi, aux, q, ga, cache_ik, cache_k, cache_v, new_k, new_v)


def _merge_kernel(x_ref, za_ref, zc_ref, w_ref, g_ref, y_ref):
    z = jnp.concatenate([za_ref[...], zc_ref[...]], axis=1)
    y = jnp.dot(z, w_ref[...], preferred_element_type=F32)
    y_ref[...] = x_ref[...] + _rmsnorm(y, g_ref[...])


def _merge(x, za, zc, w_out, g_post, tm):
    rows = x.shape[0]
    row_spec = lambda n: pl.BlockSpec((tm, n), lambda i: (i, 0))
    return pl.pallas_call(
        _merge_kernel,
        grid=(rows // tm,),
        in_specs=[row_spec(D_MODEL), row_spec(D_ATTN), row_spec(D_CONV), _resident((D_MODEL, D_MODEL), 1),
                  _resident((1, D_MODEL), 1)],
        out_specs=row_spec(D_MODEL),
        out_shape=jax.ShapeDtypeStruct((rows, D_MODEL), F32),
        compiler_params=pltpu.CompilerParams(dimension_semantics=("arbitrary",),
                                             vmem_limit_bytes=VMEM_LIMIT_BYTES),
        name="merge",
    )(x, za, zc, w_out, g_post)


PROJECT_ROWS = 256
ATTEND_ROWS = 256
MERGE_ROWS = 512


def _layer(xp, xs, cache_k, cache_v, cache_ik, state, g_pre, w_in, w_conv, w_out, g_post):
    nb, t, _ = xp.shape
    nseq, seqlen, _ = xs.shape
    g_pre = g_pre.reshape(1, D_MODEL)
    g_post = g_post.reshape(1, D_MODEL)
    w_t = w_in.T.astype(BF16)
    w_out = w_out.astype(BF16)

    assert PROJECT_ROWS == ATTEND_ROWS
    q, k, v, ga, qi, ik, aux, zc, kpad, kb, vt, cs = _project_prompt(xp, g_pre, w_t, w_conv, PROJECT_ROWS)
    za = _attend_prompt(qi, aux, q, ga, kpad, kb, vt, ATTEND_ROWS)
    yp = _merge(xp.reshape(nb * t, D_MODEL), za.reshape(nb * t, D_ATTN), zc.reshape(nb * t, D_CONV),
                w_out, g_post, MERGE_ROWS).reshape(nb, t, D_MODEL)

    sq, sk, sv, sga, sqi, sik, saux, szc, scs = _project_sample(xs, g_pre, w_t, w_conv, state)
    per_seq = lambda a: a.reshape(nseq, -1, a.shape[-1])
    kv_rows = lambda a: a.reshape(nseq, -1, HEAD_DIM)
    sza = _attend_sample(per_seq(sqi), per_seq(saux), per_seq(sq), per_seq(sga), cache_ik,
                         kv_rows(cache_k), kv_rows(cache_v), per_seq(sk), per_seq(sv), ATTEND_ROWS)
    ys = _merge(xs.reshape(nseq * seqlen, D_MODEL), sza.reshape(nseq * seqlen, D_ATTN), szc, w_out,
                g_post, MERGE_ROWS).reshape(nseq, seqlen, D_MODEL)

    heads = lambda a, lead: a.reshape(lead + (N_KV_HEADS, HEAD_DIM))
    return (yp, ys, heads(k, (nb, t)), heads(v, (nb, t)), ik, cs,
            heads(sk, (nseq, seqlen)), heads(sv, (nseq, seqlen)), per_seq(sik), scs)


def kernel(x_prompt, x_sample, cache_k, cache_v, cache_idx_k, state_conv, g_pre, w_in, w_conv, w_out,
           g_post):
    depth = g_pre.shape[0]
    xp, xs = x_prompt, x_sample
    outs = []
    for l in range(depth):
        res = _layer(xp, xs, cache_k[l], cache_v[l], cache_idx_k[l], state_conv[l], g_pre[l], w_in[l],
                     w_conv[l], w_out[l], g_post[l])
        xp, xs = res[0], res[1]
        outs.append(res[2:])
    stacked = [jnp.stack([o[i] for o in outs]) for i in range(8)]
    return (xp, xs) + tuple(stacked)
```

```python
import functools

import jax
import jax.numpy as jnp
from jax import lax
from jax.experimental import pallas as pl
from jax.experimental.pallas import tpu as pltpu

F32 = jnp.float32
BF16 = jnp.bfloat16
I32 = jnp.int32
I16 = jnp.int16

D_MODEL = 2048
D_ATTN = 1024
D_CONV = 1024
HEAD_DIM = 128
N_KV_HEADS = 2
KV_GROUP = 4
N_HEADS = N_KV_HEADS * KV_GROUP
KV_DIM = N_KV_HEADS * HEAD_DIM
N_IDX_HEADS = 16
IDX_DIM = 64
TOPK_MAX = 256
CHUNK = 64
CONV_WIDTH = 3
RMS_EPS = 1e-6

LANES = 128
SUBLANES = 8
I16_ROWS = 2 * SUBLANES
VMEM_LIMIT_BYTES = 60 * 1000 * 1024

OFF_Q = 0
OFF_K = OFF_Q + D_ATTN
OFF_V = OFF_K + KV_DIM
OFF_GA = OFF_V + KV_DIM
OFF_QI = OFF_GA + D_ATTN
OFF_KW = OFF_QI + N_IDX_HEADS * IDX_DIM
OFF_B = OFF_KW + IDX_DIM + N_IDX_HEADS
OFF_C = OFF_B + D_CONV
OFF_HC = OFF_C + D_CONV
OFF_GB = OFF_HC + D_CONV
D_PROJ = OFF_GB + D_CONV
assert all(o % I16_ROWS == 0 for o in (OFF_K, OFF_V, OFF_GA, OFF_QI, OFF_KW, OFF_B, OFF_C, OFF_HC, OFF_GB))

CONV_COLS = 256
PAD_ROWS = SUBLANES
HEADS_PER_QUAD = 4
QUAD_LANES = HEADS_PER_QUAD * IDX_DIM
N_QUADS = N_IDX_HEADS // HEADS_PER_QUAD
IDX_SCALE = (IDX_DIM ** -0.5) * (N_IDX_HEADS ** -0.5)
LOG2E = 1.4426950408889634
ATT_SCALE_LOG2 = HEAD_DIM ** -0.5 * LOG2E
INT_MAX = 2 ** 31 - 1
HALF_BITS = 16
HALF_MASK = 2 ** HALF_BITS - 1
HALF_MIN = -(2 ** (HALF_BITS - 1))
HALF_MAX = 2 ** (HALF_BITS - 1) - 1
KEY_POS_INF = 0x7F800000
KEY_NEG_INF = -KEY_POS_INF - 1
MAX_SEARCH_STEPS = 36
F32_LOWEST = -3.4028234663852886e38
COUNT_UNROLL = 4
NEG_INF = float("-inf")


def _silu(x):
    return x * jax.nn.sigmoid(x)


def _dot_nt(a, b):
    return lax.dot_general(a, b, (((1,), (1,)), ((), ())), preferred_element_type=F32)


def _rmsnorm(x, g):
    ms = jnp.mean(x * x, axis=-1, keepdims=True)
    return x * lax.rsqrt(ms + RMS_EPS) * g


def _project_rows(h, w_ref, q_ref, k_ref, v_ref, ga_ref, qi_ref, ik_ref, aux_ref):
    rows = h.shape[0]

    def mm(r0, n):
        return _dot_nt(h, w_ref[r0:r0 + n, :])

    q_ref[...] = (mm(OFF_Q, D_ATTN) * ATT_SCALE_LOG2).astype(BF16)
    kk = mm(OFF_K, KV_DIM)
    vv = mm(OFF_V, KV_DIM)
    for n in range(N_KV_HEADS):
        k_ref[pl.ds(n, rows, stride=N_KV_HEADS), :] = kk[:, n * HEAD_DIM:(n + 1) * HEAD_DIM]
        v_ref[pl.ds(n, rows, stride=N_KV_HEADS), :] = vv[:, n * HEAD_DIM:(n + 1) * HEAD_DIM]
    ga_ref[...] = mm(OFF_GA, D_ATTN)
    qi_ref[...] = mm(OFF_QI, N_IDX_HEADS * IDX_DIM).astype(BF16)
    kw = mm(OFF_KW, LANES)
    aux_ref[...] = kw
    if ik_ref.shape[0] == IDX_DIM:
        ik_ref[...] = kw.T[:IDX_DIM, :]
    else:
        ik_ref[...] = kw[:, :IDX_DIM]
    return kk, vv, kw


def _conv_chunk(h, w_ref, c):
    def mm(off):
        return _dot_nt(h, w_ref[off + c:off + c + CONV_COLS, :])

    return mm(OFF_B), mm(OFF_C) * mm(OFF_HC), mm(OFF_GB)


def _conv_out(bg, gb, u, um1, um2, wc_ref, c):
    w0 = wc_ref[0:1, c:c + CONV_COLS]
    w1 = wc_ref[1:2, c:c + CONV_COLS]
    w2 = wc_ref[2:3, c:c + CONV_COLS]
    conv = w0 * um2 + w1 * um1 + w2 * u
    return (bg * conv * _silu(gb)).astype(BF16)


def _project_prompt_kernel(x_ref, g_ref, w_ref, wc_ref, q_ref, k_ref, v_ref, ga_ref, qi_ref, ik_ref,
                           aux_ref, zc_ref, kpad_ref, kb_ref, vt_ref, cs_ref, upad_ref, *, tm):
    @pl.when(pl.program_id(1) == 0)
    def _():
        upad_ref[0:PAD_ROWS, :] = jnp.zeros((PAD_ROWS, D_CONV), F32)

    h = _rmsnorm(x_ref[...], g_ref[...]).astype(BF16)
    kk, vv, kw = _project_rows(h, w_ref, q_ref, k_ref, v_ref, ga_ref, qi_ref, ik_ref, aux_ref)
    kb_ref[...] = kk.astype(BF16)
    vt_ref[...] = vv.T.astype(BF16)
    lane = lax.broadcasted_iota(I32, kw.shape, 1)
    lo = jnp.where(lane < IDX_DIM, kw, 0.0)
    hi = pltpu.roll(lo, IDX_DIM, axis=1)
    zero = jnp.zeros_like(lo)
    for i, blk in enumerate((lo, zero, hi, zero, zero, lo, zero, hi)):
        kpad_ref[:, i * LANES:(i + 1) * LANES] = blk.astype(BF16)

    for c in range(0, D_CONV, CONV_COLS):
        bg, u, gb = _conv_chunk(h, w_ref, c)
        upad_ref[PAD_ROWS:PAD_ROWS + tm, c:c + CONV_COLS] = u
        um1 = upad_ref[PAD_ROWS - 1:PAD_ROWS - 1 + tm, c:c + CONV_COLS]
        um2 = upad_ref[PAD_ROWS - 2:PAD_ROWS - 2 + tm, c:c + CONV_COLS]
        zc_ref[:, c:c + CONV_COLS] = _conv_out(bg, gb, u, um1, um2, wc_ref, c)
    last = upad_ref[PAD_ROWS + tm - (CONV_WIDTH - 1):PAD_ROWS + tm, :]
    cs_ref[...] = last
    upad_ref[PAD_ROWS - (CONV_WIDTH - 1):PAD_ROWS, :] = last


def _project_sample_kernel(x_ref, g_ref, w_ref, wc_ref, st_ref, q_ref, k_ref, v_ref, ga_ref, qi_ref,
                           ik_ref, aux_ref, zc_ref, cs_ref, upad_ref, *, nseq, seqlen):
    rows = nseq * seqlen
    upad_ref[:, PAD_ROWS - (CONV_WIDTH - 1):PAD_ROWS, :] = st_ref[...]
    h = _rmsnorm(x_ref[...], g_ref[...]).astype(BF16)
    _project_rows(h, w_ref, q_ref, k_ref, v_ref, ga_ref, qi_ref, ik_ref, aux_ref)
    for c in range(0, D_CONV, CONV_COLS):
        bg, u, gb = _conv_chunk(h, w_ref, c)
        upad_ref[:, PAD_ROWS:PAD_ROWS + seqlen, c:c + CONV_COLS] = u.reshape(nseq, seqlen, CONV_COLS)
        um1 = upad_ref[:, PAD_ROWS - 1:PAD_ROWS - 1 + seqlen, c:c + CONV_COLS].reshape(rows, CONV_COLS)
        um2 = upad_ref[:, PAD_ROWS - 2:PAD_ROWS - 2 + seqlen, c:c + CONV_COLS].reshape(rows, CONV_COLS)
        zc_ref[:, c:c + CONV_COLS] = _conv_out(bg, gb, u, um1, um2, wc_ref, c)
    cs_ref[...] = upad_ref[:, PAD_ROWS + seqlen - (CONV_WIDTH - 1):PAD_ROWS + seqlen, :]


_PROJECT_OUTS = ((1, D_ATTN, BF16), (N_KV_HEADS, HEAD_DIM, F32), (N_KV_HEADS, HEAD_DIM, F32),
                 (1, D_ATTN, F32), (1, N_IDX_HEADS * IDX_DIM, BF16), (1, IDX_DIM, F32), (1, LANES, F32),
                 (1, D_CONV, BF16))
_IK_OUT = 5


def _project_out_shapes(lead, rows):
    return [jax.ShapeDtypeStruct(lead + (m * rows, n), dt) for m, n, dt in _PROJECT_OUTS]


def _resident(shape, ngrid):
    zeros = (0,) * len(shape)
    if ngrid == 1:
        return pl.BlockSpec(shape, lambda i: zeros, pipeline_mode=pl.Buffered(1))
    return pl.BlockSpec(shape, lambda b, i: zeros, pipeline_mode=pl.Buffered(1))


def _project_prompt(x, g_pre, w_t, w_conv, tm):
    nb, t, _ = x.shape
    grid = (nb, t // tm)
    row_spec = lambda n, m=1: pl.BlockSpec((None, m * tm, n), lambda b, i: (b, i, 0))
    out_shapes = _project_out_shapes((nb,), t) + [
        jax.ShapeDtypeStruct((nb, t, N_QUADS * QUAD_LANES), BF16),
        jax.ShapeDtypeStruct((nb, t, KV_DIM), BF16),
        jax.ShapeDtypeStruct((nb, t // tm, KV_DIM, tm), BF16),
        jax.ShapeDtypeStruct((nb, CONV_WIDTH - 1, D_CONV), F32)]
    out_specs = [row_spec(n, m) for m, n, _ in _PROJECT_OUTS]
    out_shapes[_IK_OUT] = jax.ShapeDtypeStruct((nb, IDX_DIM, t), F32)
    out_specs[_IK_OUT] = pl.BlockSpec((None, IDX_DIM, tm), lambda b, i: (b, 0, i))
    out_specs += [row_spec(N_QUADS * QUAD_LANES), row_spec(KV_DIM),
                  pl.BlockSpec((None, None, KV_DIM, tm), lambda b, i: (b, i, 0, 0)),
                  pl.BlockSpec((None, CONV_WIDTH - 1, D_CONV), lambda b, i: (b, 0, 0))]
    return pl.pallas_call(
        functools.partial(_project_prompt_kernel, tm=tm),
        grid=grid,
        in_specs=[row_spec(D_MODEL), _resident((1, D_MODEL), 2), _resident((D_PROJ, D_MODEL), 2),
                  _resident((CONV_WIDTH, D_CONV), 2)],
        out_specs=out_specs,
        out_shape=out_shapes,
        scratch_shapes=[pltpu.VMEM((PAD_ROWS + tm, D_CONV), F32)],
        compiler_params=pltpu.CompilerParams(dimension_semantics=("arbitrary", "arbitrary"),
                                             vmem_limit_bytes=VMEM_LIMIT_BYTES),
        name="project_prompt",
    )(x, g_pre, w_t, w_conv)


def _project_sample(x, g_pre, w_t, w_conv, state):
    nseq, seqlen, _ = x.shape
    rows = nseq * seqlen
    full = lambda shape: pl.BlockSpec(shape, lambda i: (0,) * len(shape))
    out_shapes = _project_out_shapes((), rows) + [jax.ShapeDtypeStruct((nseq, CONV_WIDTH - 1, D_CONV), F32)]
    out_specs = [full((m * rows, n)) for m, n, _ in _PROJECT_OUTS] + [full((nseq, CONV_WIDTH - 1, D_CONV))]
    return pl.pallas_call(
        functools.partial(_project_sample_kernel, nseq=nseq, seqlen=seqlen),
        grid=(1,),
        in_specs=[full((rows, D_MODEL)), full((1, D_MODEL)), _resident((D_PROJ, D_MODEL), 1),
                  full((CONV_WIDTH, D_CONV)), full((nseq, CONV_WIDTH - 1, D_CONV))],
        out_specs=out_specs,
        out_shape=out_shapes,
        scratch_shapes=[pltpu.VMEM((nseq, PAD_ROWS + seqlen, D_CONV), F32)],
        compiler_params=pltpu.CompilerParams(dimension_semantics=("arbitrary",),
                                             vmem_limit_bytes=VMEM_LIMIT_BYTES),
        name="project_sample",
    )(x.reshape(rows, D_MODEL), g_pre, w_t, w_conv, state)


def _key_of_score(x):
    bits = lax.bitcast_convert_type(x, I32)
    return bits ^ ((bits >> 31) & INT_MAX)


def _score_of_key(key):
    return lax.bitcast_convert_type(key ^ ((key >> 31) & INT_MAX), F32)


def _store_scores(score, admissible, sc_scr, hi_scr, lo_scr, kt):
    score = jnp.where(admissible, score, NEG_INF)
    sc_scr[kt] = score
    key = _key_of_score(score)
    hi_scr[kt] = (key >> HALF_BITS).astype(I16)
    lo_scr[kt] = ((key & HALF_MASK) + HALF_MIN).astype(I16)


def _vreg_sum(p, rows):
    parts = [p[r:r + rows, :] for r in range(0, p.shape[0], rows)]
    while len(parts) > 1:
        parts = [a + b for a, b in zip(parts[::2], parts[1::2])] + parts[len(parts) & ~1:]
    return parts[0]


def _threshold_key_guess(for_tiles, hi_scr, lo_scr, width):
    def count_half(ref, th):
        def body(kt, c16):
            return c16 + _vreg_sum(jnp.where(ref[kt] >= th, jnp.int16(1), jnp.int16(0)), I16_ROWS)

        c16 = for_tiles(body, jnp.zeros((I16_ROWS, width), I16), COUNT_UNROLL)
        return jnp.sum(c16.astype(I32), axis=0, keepdims=True)

    def bisect_half(ref):
        def step(_, carry):
            lo, hi = carry
            mid = (lo + hi + 1) >> 1
            cnt = count_half(ref, jnp.minimum(mid, HALF_MAX).astype(I16))
            ge = (cnt >= TOPK_MAX) & (mid <= HALF_MAX)
            return jnp.where(ge, mid, lo), jnp.where(ge, hi, mid)

        lo0 = jnp.full((1, width), HALF_MIN, I32)
        hi0 = jnp.full((1, width), HALF_MAX + 1, I32)
        return lax.fori_loop(0, HALF_BITS, step, (lo0, hi0))[0]

    top = bisect_half(hi_scr)
    top16 = top.astype(I16)

    def narrow(kt, carry):
        h = hi_scr[kt]
        lo_scr[kt] = jnp.where(h == top16, lo_scr[kt],
                               jnp.where(h > top16, jnp.int16(HALF_MAX), jnp.int16(HALF_MIN)))
        return carry

    for_tiles(narrow, 0)
    bot = bisect_half(lo_scr)
    return top * (HALF_MASK + 1) + (bot - HALF_MIN)


def _topk_threshold(for_tiles, sc_scr, hi_scr, lo_scr, thr_scr, *, tk, width, n_cols):
    def count(pred_fn):
        def body(kt, c8):
            return c8 + _vreg_sum(pred_fn(kt, sc_scr[kt]).astype(I32), SUBLANES)

        c8 = for_tiles(body, jnp.zeros((SUBLANES, width), I32), COUNT_UNROLL)
        return jnp.sum(c8, axis=0, keepdims=True)

    hint_lo = _threshold_key_guess(for_tiles, hi_scr, lo_scr, width)
    hint_hi = hint_lo + 1

    def open_lanes(lo, hi):
        return jnp.max((hi > lo + 1).astype(F32)) > 0.0

    def cond(state):
        i, lo, hi, _ = state
        return (i < 2) | ((i < MAX_SEARCH_STEPS) & open_lanes(lo, hi))

    def step(state):
        i, lo, hi, n_lo = state
        mid = (lo >> 1) + (hi >> 1) + ((lo | hi) & 1)
        cand = jnp.where(i == 0, hint_lo, jnp.where(i == 1, hint_hi, mid))
        cand = jnp.minimum(jnp.maximum(cand, lo + 1), hi)
        cand_score = _score_of_key(cand)
        cnt = count(lambda kt, t: t >= cand_score)
        ge = cnt >= TOPK_MAX
        hit = cnt == TOPK_MAX
        lo_new = jnp.where(ge, cand, lo)
        hi_new = jnp.where(hit, cand + 1, jnp.where(ge, hi, cand))
        return i + 1, lo_new, hi_new, jnp.where(ge, cnt, n_lo)

    lo0 = jnp.full((1, width), KEY_NEG_INF, I32)
    hi0 = jnp.full((1, width), KEY_POS_INF + 1, I32)
    n0 = jnp.full((1, width), INT_MAX, I32)
    _, lo, _, n_lo = lax.while_loop(cond, step, (jnp.int32(0), lo0, hi0, n0))
    thr = _score_of_key(lo)
    thr_scr[...] = jnp.broadcast_to(thr, (SUBLANES, width))

    @pl.when(jnp.max((n_lo > TOPK_MAX).astype(F32)) > 0.0)
    def _():
        need = TOPK_MAX - count(lambda kt, t: t > thr)

        def pos_of(kt):
            return kt * tk + lax.broadcasted_iota(I32, (tk, width), 0)

        def pos_bisect(_, carry):
            plo, phi = carry
            mid = (plo + phi) >> 1
            ok = count(lambda kt, t: (t == thr) & (pos_of(kt) <= mid)) >= need
            return jnp.where(ok, plo, mid), jnp.where(ok, mid, phi)

        plo0 = jnp.full((1, width), -1, I32)
        phi0 = jnp.full((1, width), n_cols - 1, I32)
        steps = max(1, (n_cols - 1).bit_length()) + 1
        _, pos = lax.fori_loop(0, steps, pos_bisect, (plo0, phi0))

        def drop(kt, carry):
            t = sc_scr[kt]
            sc_scr[kt] = jnp.where((t == thr) & (pos_of(kt) > pos), NEG_INF, t)
            return carry

        for_tiles(drop, 0)


def _softmax_step(lg, m_ref, l_ref, acc_ref, pv_fn):
    m_prev = m_ref[0:1, :]
    m_new = jnp.maximum(m_prev, jnp.max(lg, axis=0, keepdims=True))
    m_safe = jnp.where(m_new == NEG_INF, 0.0, m_new)
    alpha = jnp.exp2(m_prev - m_safe)
    p = jnp.exp2(lg - m_safe)
    l_new = alpha * l_ref[0:1, :] + jnp.sum(p, axis=0, keepdims=True)
    acc_ref[...] = acc_ref[...] * alpha + pv_fn(p.astype(BF16))
    m_ref[...] = jnp.broadcast_to(m_new, m_ref.shape)
    l_ref[...] = jnp.broadcast_to(l_new, l_ref.shape)


def _attend_prompt_kernel(qi_ref, aux_ref, q_ref, ga_ref, kpad_ref, k_ref, vt_ref, z_ref, sc_scr,
                          hi_scr, lo_scr, thr_scr, m_scr, l_scr, acc_scr, lg_scr, *, tq, tk, n_cols):
    j = pl.program_id(1)
    n_tiles = j + 1

    def for_tiles(fn, init, unroll=2):
        shift = unroll.bit_length() - 1
        assert unroll == 1 << shift
        n_main = n_tiles >> shift

        def body(i, carry):
            for u in range(unroll):
                carry = fn(unroll * i + u, carry)
            return carry

        carry = lax.fori_loop(0, n_main, body, init)
        return lax.fori_loop(n_main << shift, n_tiles, fn, carry)

    assert tq == tk == TOPK_MAX and tq % CHUNK == 0
    qi = qi_ref[...]
    qstack = jnp.concatenate([qi[:, u * QUAD_LANES:(u + 1) * QUAD_LANES] for u in range(N_QUADS)], axis=0)
    w_t = aux_ref[...].T
    w_rows = [w_t[IDX_DIM + h:IDX_DIM + h + 1, :] for h in range(N_IDX_HEADS)]
    key_chunk = lax.broadcasted_iota(I32, (tk, tq), 0) // CHUNK
    qry_chunk = lax.broadcasted_iota(I32, (tk, tq), 1) // CHUNK
    diag_adm = key_chunk <= qry_chunk

    def idx_tile(kt, carry):
        start = pl.multiple_of(kt * tk, tk)
        acc = jnp.zeros((tk, tq), F32)
        for c in range(HEADS_PER_QUAD):
            kp = kpad_ref[pl.ds(start, tk), c * QUAD_LANES:(c + 1) * QUAD_LANES]
            s = _dot_nt(kp, qstack)
            for u in range(N_QUADS):
                acc = acc + jnp.maximum(s[:, u * tq:(u + 1) * tq], 0.0) * w_rows[HEADS_PER_QUAD * u + c]
        _store_scores(acc * IDX_SCALE, (diag_adm & (kt == j)) | (kt < j), sc_scr, hi_scr, lo_scr, kt)
        return carry

    for_tiles(idx_tile, 0)

    thr_scr[...] = jnp.full((SUBLANES, tq), F32_LOWEST, F32)

    @pl.when(j >= 1)
    def _():
        _topk_threshold(for_tiles, sc_scr, hi_scr, lo_scr, thr_scr, tk=tk, width=tq, n_cols=n_cols)

    thr = thr_scr[0:1, :]

    q = q_ref[...]
    qn = [jnp.concatenate([q[:, (KV_GROUP * n + g) * HEAD_DIM:(KV_GROUP * n + g + 1) * HEAD_DIM]
                           for g in range(KV_GROUP)], axis=0) for n in range(N_KV_HEADS)]
    m_scr[...] = jnp.full(m_scr.shape, NEG_INF, F32)
    l_scr[...] = jnp.zeros(l_scr.shape, F32)
    acc_scr[...] = jnp.zeros(acc_scr.shape, F32)

    def att_tile(kt, carry):
        start = pl.multiple_of(kt * tk, tk)
        sel = sc_scr[kt] >= thr
        for n in range(N_KV_HEADS):
            lg_scr[n] = _dot_nt(k_ref[pl.ds(start, tk), n * HEAD_DIM:(n + 1) * HEAD_DIM], qn[n])
        for n in range(N_KV_HEADS):
            lg = jnp.concatenate([jnp.where(sel, lg_scr[n, :, g * tq:(g + 1) * tq], NEG_INF)
                                  for g in range(KV_GROUP)], axis=1)
            vt_n = vt_ref[kt, n * HEAD_DIM:(n + 1) * HEAD_DIM, :]
            _softmax_step(lg, m_scr.at[n], l_scr.at[n], acc_scr.at[n],
                          lambda p: jnp.dot(vt_n, p, preferred_element_type=F32))
        return carry

    for_tiles(att_tile, 0)

    outs = []
    for n in range(N_KV_HEADS):
        o = acc_scr[n] / l_scr[n][0:1, :]
        outs.extend(o[:, g * tq:(g + 1) * tq].T for g in range(KV_GROUP))
    z_ref[...] = (jnp.concatenate(outs, axis=1) * _silu(ga_ref[...])).astype(BF16)


def _attend_prompt(qi, aux, q, ga, kpad, kb, vt, tq):
    nb, t, _ = q.shape
    tk = tq
    n_tiles = t // tk
    lanes_q = KV_GROUP * tq
    blk = lambda n: pl.BlockSpec((None, tq, n), lambda b, j: (b, j, 0))
    seq = lambda n: pl.BlockSpec((None, t, n), lambda b, j: (b, 0, 0))
    scratch = [pltpu.VMEM((n_tiles, tk, tq), F32), pltpu.VMEM((n_tiles, tk, tq), I16),
               pltpu.VMEM((n_tiles, tk, tq), I16), pltpu.VMEM((SUBLANES, tq), F32),
               pltpu.VMEM((N_KV_HEADS, SUBLANES, lanes_q), F32),
               pltpu.VMEM((N_KV_HEADS, SUBLANES, lanes_q), F32),
               pltpu.VMEM((N_KV_HEADS, HEAD_DIM, lanes_q), F32),
               pltpu.VMEM((N_KV_HEADS, tk, lanes_q), F32)]
    return pl.pallas_call(
        functools.partial(_attend_prompt_kernel, tq=tq, tk=tk, n_cols=t),
        grid=(nb, t // tq),
        in_specs=[blk(N_IDX_HEADS * IDX_DIM), blk(LANES), blk(D_ATTN), blk(D_ATTN),
                  seq(N_QUADS * QUAD_LANES), seq(KV_DIM),
                  pl.BlockSpec((None, n_tiles, KV_DIM, tk), lambda b, j: (b, 0, 0, 0))],
        out_specs=blk(D_ATTN),
        out_shape=jax.ShapeDtypeStruct((nb, t, D_ATTN), BF16),
        scratch_shapes=scratch,
        compiler_params=pltpu.CompilerParams(dimension_semantics=("arbitrary", "arbitrary"),
                                             vmem_limit_bytes=VMEM_LIMIT_BYTES),
        name="attend_prompt",
    )(qi, aux, q, ga, kpad, kb, vt)


def _attend_sample_kernel(qi_ref, aux_ref, q_ref, ga_ref, cikt_ref, ck_ref, cv_ref, nk_ref, nv_ref,
                          z_ref, kit_s, k_s, vt_s, sc_scr, hi_scr, lo_scr, thr_scr, m_scr, l_scr, acc_scr,
                          *, tq, tk, past, n_tiles):
    n_keys = past + tq
    hl = N_IDX_HEADS * tq
    assert N_HEADS * tq == LANES and hl == 2 * LANES and past % tk == 0 and tq <= tk
    assert n_keys > TOPK_MAX and past % CHUNK == 0 and tq <= CHUNK

    def for_tiles(fn, init, unroll=None):
        for kt in range(n_tiles):
            init = fn(kt, init)
        return init

    aux = aux_ref[...]
    kit_s[:, 0:past] = cikt_ref[...].astype(BF16)
    kit_s[:, past:] = jnp.zeros((IDX_DIM, n_tiles * tk - past), BF16)
    kit_s[:, past:past + tq] = aux.T[:IDX_DIM, :].astype(BF16)
    k_s[past + tq:, :] = jnp.zeros((n_tiles * tk - n_keys, KV_DIM), BF16)
    vt_s[n_tiles - 1] = jnp.zeros((KV_DIM, tk), BF16)
    for n in range(N_KV_HEADS):
        cols = slice(n * HEAD_DIM, (n + 1) * HEAD_DIM)
        k_s[0:past, cols] = ck_ref[pl.ds(n, past, stride=N_KV_HEADS), :].astype(BF16)
        k_s[past:past + tq, cols] = nk_ref[pl.ds(n, tq, stride=N_KV_HEADS), :].astype(BF16)
        for kt in range(past // tk):
            v_tile = cv_ref[pl.ds(N_KV_HEADS * kt * tk + n, tk, stride=N_KV_HEADS), :]
            vt_s[kt, cols, :] = v_tile.T.astype(BF16)
        vt_s[n_tiles - 1, cols, 0:tq] = nv_ref[pl.ds(n, tq, stride=N_KV_HEADS), :].T.astype(BF16)

    qi = qi_ref[...]
    qrows = jnp.concatenate([qi[:, h * IDX_DIM:(h + 1) * IDX_DIM] for h in range(N_IDX_HEADS)], axis=0)
    w_rows = jnp.concatenate([aux[:, IDX_DIM + h:IDX_DIM + h + 1] for h in range(N_IDX_HEADS)], axis=0)
    w_rows = jnp.broadcast_to(w_rows, (hl, LANES))

    def idx_tile(kt, carry):
        s = jnp.dot(qrows, kit_s[:, kt * tk:(kt + 1) * tk], preferred_element_type=F32)
        y = jnp.concatenate([jnp.maximum(s[:, c:c + LANES], 0.0) * w_rows for c in range(0, tk, LANES)],
                            axis=1)
        per_query = _vreg_sum(y, tq) * IDX_SCALE
        score = jnp.concatenate([per_query] * N_HEADS, axis=0).T
        pos = kt * tk + lax.broadcasted_iota(I32, (tk, LANES), 0)
        _store_scores(score, pos < n_keys, sc_scr, hi_scr, lo_scr, kt)
        return carry

    for_tiles(idx_tile, 0)

    _topk_threshold(for_tiles, sc_scr, hi_scr, lo_scr, thr_scr, tk=tk, width=LANES, n_cols=n_tiles * tk)
    thr = thr_scr[0:1, :]

    q = q_ref[...]
    zeros = jnp.zeros((tq, HEAD_DIM), BF16)
    qblk = jnp.concatenate(
        [jnp.concatenate([q[:, (KV_GROUP * n + g) * HEAD_DIM:(KV_GROUP * n + g + 1) * HEAD_DIM]
                          if m == n else zeros for m in range(N_KV_HEADS)], axis=1)
         for n in range(N_KV_HEADS) for g in range(KV_GROUP)], axis=0)
    m_scr[...] = jnp.full(m_scr.shape, NEG_INF, F32)
    l_scr[...] = jnp.zeros(l_scr.shape, F32)
    acc_scr[...] = jnp.zeros(acc_scr.shape, F32)

    def att_tile(kt, carry):
        lg = jnp.where(sc_scr[kt] >= thr, _dot_nt(k_s[kt * tk:(kt + 1) * tk, :], qblk), NEG_INF)
        _softmax_step(lg, m_scr, l_scr, acc_scr, lambda p: jnp.dot(vt_s[kt], p, preferred_element_type=F32))
        return carry

    for_tiles(att_tile, 0)

    o_t = (acc_scr[...] / l_scr[0:1, :]).T
    outs = [o_t[(KV_GROUP * n + g) * tq:(KV_GROUP * n + g + 1) * tq, n * HEAD_DIM:(n + 1) * HEAD_DIM]
            for n in range(N_KV_HEADS) for g in range(KV_GROUP)]
    z_ref[...] = (jnp.concatenate(outs, axis=1) * _silu(ga_ref[...])).astype(BF16)


def _attend_sample(qi, aux, q, ga, cache_ik, cache_k, cache_v, new_k, new_v, tk):
    nseq, tq, _ = q.shape
    past = cache_ik.shape[1]
    cache_ikt = jnp.transpose(cache_ik, (0, 2, 1))
    n_tiles = -(-(past + tq) // tk)
    blk = lambda n, m=1: pl.BlockSpec((None, m * tq, n), lambda b: (b, 0, 0))
    cache = lambda n, m=1: pl.BlockSpec((None, m * past, n), lambda b: (b, 0, 0))
    scratch = [pltpu.VMEM((IDX_DIM, n_tiles * tk), BF16), pltpu.VMEM((n_tiles * tk, KV_DIM), BF16),
               pltpu.VMEM((n_tiles, KV_DIM, tk), BF16),
               pltpu.VMEM((n_tiles, tk, LANES), F32), pltpu.VMEM((n_tiles, tk, LANES), I16),
               pltpu.VMEM((n_tiles, tk, LANES), I16),
               pltpu.VMEM((SUBLANES, LANES), F32),
               pltpu.VMEM((SUBLANES, LANES), F32), pltpu.VMEM((SUBLANES, LANES), F32),
               pltpu.VMEM((KV_DIM, LANES), F32)]
    return pl.pallas_call(
        functools.partial(_attend_sample_kernel, tq=tq, tk=tk, past=past, n_tiles=n_tiles),
        grid=(nseq,),
        in_specs=[blk(N_IDX_HEADS * IDX_DIM), blk(LANES), blk(D_ATTN), blk(D_ATTN),
                  pl.BlockSpec((None, IDX_DIM, past), lambda b: (b, 0, 0)),
                  cache(HEAD_DIM, N_KV_HEADS), cache(HEAD_DIM, N_KV_HEADS), blk(HEAD_DIM, N_KV_HEADS),
                  blk(HEAD_DIM, N_KV_HEADS)],
        out_specs=blk(D_ATTN),
        out_shape=jax.ShapeDtypeStruct((nseq, tq, D_ATTN), BF16),
        scratch_shapes=scratch,
        compiler_params=pltpu.CompilerParams(dimension_semantics=("arbitrary",),
                                             vmem_limit_bytes=VMEM_LIMIT_BYTES),
        name="attend_sample",
    )(qi, aux, q, ga, cache_ikt, cache_k, cache_v, new_k, new_v)


def _merge_kernel(x_ref, za_ref, zc_ref, w_ref, g_ref, y_ref):
    z = jnp.concatenate([za_ref[...], zc_ref[...]], axis=1)
    y = jnp.dot(z, w_ref[...], preferred_element_type=F32)
    y_ref[...] = x_ref[...] + _rmsnorm(y, g_ref[...])


def _merge(x, za, zc, w_out, g_post, tm):
    rows = x.shape[0]
    row_spec = lambda n: pl.BlockSpec((tm, n), lambda i: (i, 0))
    return pl.pallas_call(
        _merge_kernel,
        grid=(rows // tm,),
        in_specs=[row_spec(D_MODEL), row_spec(D_ATTN), row_spec(D_CONV), _resident((D_MODEL, D_MODEL), 1),
                  _resident((1, D_MODEL), 1)],
        out_specs=row_spec(D_MODEL),
        out_shape=jax.ShapeDtypeStruct((rows, D_MODEL), F32),
        compiler_params=pltpu.CompilerParams(dimension_semantics=("arbitrary",),
                                             vmem_limit_bytes=VMEM_LIMIT_BYTES),
        name="merge",
    )(x, za, zc, w_out, g_post)


PROJECT_ROWS = 256
ATTEND_ROWS = 256
MERGE_ROWS = 512


def _layer(xp, xs, cache_k, cache_v, cache_ik, state, g_pre, w_in, w_conv, w_out, g_post):
    nb, t, _ = xp.shape
    nseq, seqlen, _ = xs.shape
    g_pre = g_pre.reshape(1, D_MODEL)
    g_post = g_post.reshape(1, D_MODEL)
    w_t = w_in.T.astype(BF16)
    w_out = w_out.astype(BF16)

    assert PROJECT_ROWS == ATTEND_ROWS
    q, k, v, ga, qi, ik, aux, zc, kpad, kb, vt, cs = _project_prompt(xp, g_pre, w_t, w_conv, PROJECT_ROWS)
    za = _attend_prompt(qi, aux, q, ga, kpad, kb, vt, ATTEND_ROWS)
    yp = _merge(xp.reshape(nb * t, D_MODEL), za.reshape(nb * t, D_ATTN), zc.reshape(nb * t, D_CONV),
                w_out, g_post, MERGE_ROWS).reshape(nb, t, D_MODEL)

    sq, sk, sv, sga, sqi, sik, saux, szc, scs = _project_sample(xs, g_pre, w_t, w_conv, state)
    per_seq = lambda a: a.reshape(nseq, -1, a.shape[-1])
    kv_rows = lambda a: a.reshape(nseq, -1, HEAD_DIM)
    sza = _attend_sample(per_seq(sqi), per_seq(saux), per_seq(sq), per_seq(sga), cache_ik,
                         kv_rows(cache_k), kv_rows(cache_v), per_seq(sk), per_seq(sv), ATTEND_ROWS)
    ys = _merge(xs.reshape(nseq * seqlen, D_MODEL), sza.reshape(nseq * seqlen, D_ATTN), szc, w_out,
                g_post, MERGE_ROWS).reshape(nseq, seqlen, D_MODEL)

    heads = lambda a, lead: a.reshape(lead + (N_KV_HEADS, HEAD_DIM))
    return (yp, ys, heads(k, (nb, t)), heads(v, (nb, t)), jnp.transpose(ik, (0, 2, 1)), cs,
            heads(sk, (nseq, seqlen)), heads(sv, (nseq, seqlen)), per_seq(sik), scs)


def kernel(x_prompt, x_sample, cache_k, cache_v, cache_idx_k, state_conv, g_pre, w_in, w_conv, w_out,
           g_post):
    depth = g_pre.shape[0]
    xp, xs = x_prompt, x_sample
    outs = []
    for l in range(depth):
        res = _layer(xp, xs, cache_k[l], cache_v[l], cache_idx_k[l], state_conv[l], g_pre[l], w_in[l],
                     w_conv[l], w_out[l], g_post[l])
        xp, xs = res[0], res[1]
        outs.append(res[2:])
    stacked = [jnp.stack([o[i] for o in outs]) for i in range(8)]
    return (xp, xs) + tuple(stacked)
```

```python
import functools

import jax
import jax.numpy as jnp
from jax import lax
from jax.experimental import pallas as pl
from jax.experimental.pallas import tpu as pltpu

F32 = jnp.float32
BF16 = jnp.bfloat16
I32 = jnp.int32
I16 = jnp.int16

D_MODEL = 2048
D_ATTN = 1024
D_CONV = 1024
HEAD_DIM = 128
N_KV_HEADS = 2
KV_GROUP = 4
N_HEADS = N_KV_HEADS * KV_GROUP
KV_DIM = N_KV_HEADS * HEAD_DIM
N_IDX_HEADS = 16
IDX_DIM = 64
TOPK_MAX = 256
CHUNK = 64
CONV_WIDTH = 3
RMS_EPS = 1e-6

LANES = 128
SUBLANES = 8
I16_ROWS = 2 * SUBLANES
VMEM_LIMIT_BYTES = 60 * 1000 * 1024

OFF_Q = 0
OFF_K = OFF_Q + D_ATTN
OFF_V = OFF_K + KV_DIM
OFF_GA = OFF_V + KV_DIM
OFF_QI = OFF_GA + D_ATTN
OFF_KW = OFF_QI + N_IDX_HEADS * IDX_DIM
OFF_B = OFF_KW + IDX_DIM + N_IDX_HEADS
OFF_C = OFF_B + D_CONV
OFF_HC = OFF_C + D_CONV
OFF_GB = OFF_HC + D_CONV
D_PROJ = OFF_GB + D_CONV
assert all(o % I16_ROWS == 0 for o in (OFF_K, OFF_V, OFF_GA, OFF_QI, OFF_KW, OFF_B, OFF_C, OFF_HC, OFF_GB))

CONV_COLS = 256
PAD_ROWS = SUBLANES
HEADS_PER_QUAD = 4
QUAD_LANES = HEADS_PER_QUAD * IDX_DIM
N_QUADS = N_IDX_HEADS // HEADS_PER_QUAD
IDX_SCALE = (IDX_DIM ** -0.5) * (N_IDX_HEADS ** -0.5)
LOG2E = 1.4426950408889634
ATT_SCALE_LOG2 = HEAD_DIM ** -0.5 * LOG2E
INT_MAX = 2 ** 31 - 1
HALF_BITS = 16
HALF_MASK = 2 ** HALF_BITS - 1
HALF_MIN = -(2 ** (HALF_BITS - 1))
HALF_MAX = 2 ** (HALF_BITS - 1) - 1
KEY_POS_INF = 0x7F800000
KEY_NEG_INF = -KEY_POS_INF - 1
MAX_SEARCH_STEPS = 36
F32_LOWEST = -3.4028234663852886e38
NEAR_MAX_HALF_KEYS = 5 << 7
COUNT_UNROLL = 4
ATT_GROUP = 2
NEG_INF = float("-inf")


def _silu(x):
    return x * jax.nn.sigmoid(x)


def _dot_nt(a, b):
    return lax.dot_general(a, b, (((1,), (1,)), ((), ())), preferred_element_type=F32)


def _rmsnorm(x, g):
    ms = jnp.mean(x * x, axis=-1, keepdims=True)
    return x * lax.rsqrt(ms + RMS_EPS) * g


def _project_rows(h, w_ref, q_ref, k_ref, v_ref, ga_ref, qi_ref, ik_ref, aux_ref):
    rows = h.shape[0]

    def mm(r0, n):
        return _dot_nt(h, w_ref[r0:r0 + n, :])

    q_ref[...] = (mm(OFF_Q, D_ATTN) * ATT_SCALE_LOG2).astype(BF16)
    kk = mm(OFF_K, KV_DIM)
    vv = mm(OFF_V, KV_DIM)
    for n in range(N_KV_HEADS):
        k_ref[pl.ds(n, rows, stride=N_KV_HEADS), :] = kk[:, n * HEAD_DIM:(n + 1) * HEAD_DIM]
        v_ref[pl.ds(n, rows, stride=N_KV_HEADS), :] = vv[:, n * HEAD_DIM:(n + 1) * HEAD_DIM]
    ga_ref[...] = mm(OFF_GA, D_ATTN)
    qi_ref[...] = mm(OFF_QI, N_IDX_HEADS * IDX_DIM).astype(BF16)
    kw = mm(OFF_KW, LANES)
    aux_ref[...] = kw
    if ik_ref.shape[0] == IDX_DIM:
        ik_ref[...] = kw.T[:IDX_DIM, :]
    else:
        ik_ref[...] = kw[:, :IDX_DIM]
    return kk, vv, kw


def _conv_chunk(h, w_ref, c):
    def mm(off):
        return _dot_nt(h, w_ref[off + c:off + c + CONV_COLS, :])

    return mm(OFF_B), mm(OFF_C) * mm(OFF_HC), mm(OFF_GB)


def _conv_out(bg, gb, u, um1, um2, wc_ref, c):
    w0 = wc_ref[0:1, c:c + CONV_COLS]
    w1 = wc_ref[1:2, c:c + CONV_COLS]
    w2 = wc_ref[2:3, c:c + CONV_COLS]
    conv = w0 * um2 + w1 * um1 + w2 * u
    return (bg * conv * _silu(gb)).astype(BF16)


def _project_prompt_kernel(x_ref, g_ref, w_ref, wc_ref, q_ref, k_ref, v_ref, ga_ref, qi_ref, ik_ref,
                           aux_ref, zc_ref, kpad_ref, kb_ref, vt_ref, cs_ref, upad_ref, *, tm):
    @pl.when(pl.program_id(1) == 0)
    def _():
        upad_ref[0:PAD_ROWS, :] = jnp.zeros((PAD_ROWS, D_CONV), F32)

    h = _rmsnorm(x_ref[...], g_ref[...]).astype(BF16)
    kk, vv, kw = _project_rows(h, w_ref, q_ref, k_ref, v_ref, ga_ref, qi_ref, ik_ref, aux_ref)
    kb_ref[...] = kk.astype(BF16)
    vt_ref[...] = vv.T.astype(BF16)
    lane = lax.broadcasted_iota(I32, kw.shape, 1)
    lo = jnp.where(lane < IDX_DIM, kw, 0.0)
    hi = pltpu.roll(lo, IDX_DIM, axis=1)
    zero = jnp.zeros_like(lo)
    for i, blk in enumerate((lo, zero, hi, zero, zero, lo, zero, hi)):
        kpad_ref[:, i * LANES:(i + 1) * LANES] = blk.astype(BF16)

    for c in range(0, D_CONV, CONV_COLS):
        bg, u, gb = _conv_chunk(h, w_ref, c)
        upad_ref[PAD_ROWS:PAD_ROWS + tm, c:c + CONV_COLS] = u
        um1 = upad_ref[PAD_ROWS - 1:PAD_ROWS - 1 + tm, c:c + CONV_COLS]
        um2 = upad_ref[PAD_ROWS - 2:PAD_ROWS - 2 + tm, c:c + CONV_COLS]
        zc_ref[:, c:c + CONV_COLS] = _conv_out(bg, gb, u, um1, um2, wc_ref, c)
    last = upad_ref[PAD_ROWS + tm - (CONV_WIDTH - 1):PAD_ROWS + tm, :]
    cs_ref[...] = last
    upad_ref[PAD_ROWS - (CONV_WIDTH - 1):PAD_ROWS, :] = last


def _project_sample_kernel(x_ref, g_ref, w_ref, wc_ref, st_ref, q_ref, k_ref, v_ref, ga_ref, qi_ref,
                           ik_ref, aux_ref, zc_ref, cs_ref, upad_ref, *, nseq, seqlen):
    rows = nseq * seqlen
    upad_ref[:, PAD_ROWS - (CONV_WIDTH - 1):PAD_ROWS, :] = st_ref[...]
    h = _rmsnorm(x_ref[...], g_ref[...]).astype(BF16)
    _project_rows(h, w_ref, q_ref, k_ref, v_ref, ga_ref, qi_ref, ik_ref, aux_ref)
    for c in range(0, D_CONV, CONV_COLS):
        bg, u, gb = _conv_chunk(h, w_ref, c)
        upad_ref[:, PAD_ROWS:PAD_ROWS + seqlen, c:c + CONV_COLS] = u.reshape(nseq, seqlen, CONV_COLS)
        um1 = upad_ref[:, PAD_ROWS - 1:PAD_ROWS - 1 + seqlen, c:c + CONV_COLS].reshape(rows, CONV_COLS)
        um2 = upad_ref[:, PAD_ROWS - 2:PAD_ROWS - 2 + seqlen, c:c + CONV_COLS].reshape(rows, CONV_COLS)
        zc_ref[:, c:c + CONV_COLS] = _conv_out(bg, gb, u, um1, um2, wc_ref, c)
    cs_ref[...] = upad_ref[:, PAD_ROWS + seqlen - (CONV_WIDTH - 1):PAD_ROWS + seqlen, :]


_PROJECT_OUTS = ((1, D_ATTN, BF16), (N_KV_HEADS, HEAD_DIM, F32), (N_KV_HEADS, HEAD_DIM, F32),
                 (1, D_ATTN, F32), (1, N_IDX_HEADS * IDX_DIM, BF16), (1, IDX_DIM, F32), (1, LANES, F32),
                 (1, D_CONV, BF16))
_IK_OUT = 5


def _project_out_shapes(lead, rows):
    return [jax.ShapeDtypeStruct(lead + (m * rows, n), dt) for m, n, dt in _PROJECT_OUTS]


def _resident(shape, ngrid):
    zeros = (0,) * len(shape)
    if ngrid == 1:
        return pl.BlockSpec(shape, lambda i: zeros, pipeline_mode=pl.Buffered(1))
    return pl.BlockSpec(shape, lambda b, i: zeros, pipeline_mode=pl.Buffered(1))


def _project_prompt(x, g_pre, w_t, w_conv, tm):
    nb, t, _ = x.shape
    grid = (nb, t // tm)
    row_spec = lambda n, m=1: pl.BlockSpec((None, m * tm, n), lambda b, i: (b, i, 0))
    out_shapes = _project_out_shapes((nb,), t) + [
        jax.ShapeDtypeStruct((nb, t, N_QUADS * QUAD_LANES), BF16),
        jax.ShapeDtypeStruct((nb, t, KV_DIM), BF16),
        jax.ShapeDtypeStruct((nb, t // tm, KV_DIM, tm), BF16),
        jax.ShapeDtypeStruct((nb, CONV_WIDTH - 1, D_CONV), F32)]
    out_specs = [row_spec(n, m) for m, n, _ in _PROJECT_OUTS]
    out_shapes[_IK_OUT] = jax.ShapeDtypeStruct((nb, IDX_DIM, t), F32)
    out_specs[_IK_OUT] = pl.BlockSpec((None, IDX_DIM, tm), lambda b, i: (b, 0, i))
    out_specs += [row_spec(N_QUADS * QUAD_LANES), row_spec(KV_DIM),
                  pl.BlockSpec((None, None, KV_DIM, tm), lambda b, i: (b, i, 0, 0)),
                  pl.BlockSpec((None, CONV_WIDTH - 1, D_CONV), lambda b, i: (b, 0, 0))]
    return pl.pallas_call(
        functools.partial(_project_prompt_kernel, tm=tm),
        grid=grid,
        in_specs=[row_spec(D_MODEL), _resident((1, D_MODEL), 2), _resident((D_PROJ, D_MODEL), 2),
                  _resident((CONV_WIDTH, D_CONV), 2)],
        out_specs=out_specs,
        out_shape=out_shapes,
        scratch_shapes=[pltpu.VMEM((PAD_ROWS + tm, D_CONV), F32)],
        compiler_params=pltpu.CompilerParams(dimension_semantics=("arbitrary", "arbitrary"),
                                             vmem_limit_bytes=VMEM_LIMIT_BYTES),
        name="project_prompt",
    )(x, g_pre, w_t, w_conv)


def _project_sample(x, g_pre, w_t, w_conv, state):
    nseq, seqlen, _ = x.shape
    rows = nseq * seqlen
    full = lambda shape: pl.BlockSpec(shape, lambda i: (0,) * len(shape))
    out_shapes = _project_out_shapes((), rows) + [jax.ShapeDtypeStruct((nseq, CONV_WIDTH - 1, D_CONV), F32)]
    out_specs = [full((m * rows, n)) for m, n, _ in _PROJECT_OUTS] + [full((nseq, CONV_WIDTH - 1, D_CONV))]
    return pl.pallas_call(
        functools.partial(_project_sample_kernel, nseq=nseq, seqlen=seqlen),
        grid=(1,),
        in_specs=[full((rows, D_MODEL)), full((1, D_MODEL)), _resident((D_PROJ, D_MODEL), 1),
                  full((CONV_WIDTH, D_CONV)), full((nseq, CONV_WIDTH - 1, D_CONV))],
        out_specs=out_specs,
        out_shape=out_shapes,
        scratch_shapes=[pltpu.VMEM((nseq, PAD_ROWS + seqlen, D_CONV), F32)],
        compiler_params=pltpu.CompilerParams(dimension_semantics=("arbitrary",),
                                             vmem_limit_bytes=VMEM_LIMIT_BYTES),
        name="project_sample",
    )(x.reshape(rows, D_MODEL), g_pre, w_t, w_conv, state)


def _key_of_score(x):
    bits = lax.bitcast_convert_type(x, I32)
    return bits ^ ((bits >> 31) & INT_MAX)


def _score_of_key(key):
    return lax.bitcast_convert_type(key ^ ((key >> 31) & INT_MAX), F32)


def _store_scores(score, admissible, sc_scr, hi_scr, lo_scr, gmax_scr, kt):
    score = jnp.where(admissible, score, NEG_INF)
    sc_scr[kt] = score
    gmax_scr[...] = jnp.maximum(gmax_scr[...], score)
    key = _key_of_score(score)
    hi_scr[kt] = (key >> HALF_BITS).astype(I16)
    lo_scr[kt] = key.astype(I16) ^ jnp.int16(HALF_MIN)


def _vreg_sum(p, rows):
    parts = [p[r:r + rows, :] for r in range(0, p.shape[0], rows)]
    while len(parts) > 1:
        parts = [a + b for a, b in zip(parts[::2], parts[1::2])] + parts[len(parts) & ~1:]
    return parts[0]


def _threshold_key_guess(for_tiles, hi_scr, lo_scr, gmax_scr, width):
    def count_half(ref, th):
        def body(kt, c16):
            return c16 + _vreg_sum(jnp.where(ref[kt] >= th, jnp.int16(1), jnp.int16(0)), I16_ROWS)

        c16 = for_tiles(body, jnp.zeros((I16_ROWS, width), I16), COUNT_UNROLL)
        return jnp.sum(c16.astype(I32), axis=0, keepdims=True)

    def bisect_half(ref, lo0, hi0, steps):
        def step(_, carry):
            lo, hi = carry
            mid = (lo + hi + 1) >> 1
            cnt = count_half(ref, jnp.minimum(mid, HALF_MAX).astype(I16))
            ge = (cnt >= TOPK_MAX) & (mid <= HALF_MAX)
            return jnp.where(ge, mid, lo), jnp.where(ge, hi, mid)

        return lax.fori_loop(0, steps, step, (lo0, hi0))[0]

    top_max = _key_of_score(jnp.max(gmax_scr[...], axis=0, keepdims=True)) >> HALF_BITS
    top_hi = top_max + 1
    near = jnp.maximum(top_max - NEAR_MAX_HALF_KEYS, HALF_MIN + 1)
    enough = count_half(hi_scr, near.astype(I16)) >= TOPK_MAX
    top_lo = jnp.where(enough, near, HALF_MIN)
    widest = jnp.max((top_hi - top_lo).astype(F32))
    steps = sum((widest > float(1 << b)).astype(I32) for b in range(HALF_BITS))
    top = bisect_half(hi_scr, top_lo, top_hi, steps)
    top16 = top.astype(I16)

    def narrow(kt, carry):
        h = hi_scr[kt]
        lo_scr[kt] = jnp.where(h == top16, lo_scr[kt],
                               jnp.where(h > top16, jnp.int16(HALF_MAX), jnp.int16(HALF_MIN)))
        return carry

    for_tiles(narrow, 0)
    bot = bisect_half(lo_scr, jnp.full((1, width), HALF_MIN, I32), jnp.full((1, width), HALF_MAX + 1, I32),
                      HALF_BITS)
    return top * (HALF_MASK + 1) + (bot - HALF_MIN)


def _topk_threshold(for_tiles, sc_scr, hi_scr, lo_scr, gmax_scr, thr_scr, *, tk, width, n_cols):
    def count(pred_fn):
        def body(kt, c8):
            return c8 + _vreg_sum(pred_fn(kt, sc_scr[kt]).astype(I32), SUBLANES)

        c8 = for_tiles(body, jnp.zeros((SUBLANES, width), I32), COUNT_UNROLL)
        return jnp.sum(c8, axis=0, keepdims=True)

    hint_lo = _threshold_key_guess(for_tiles, hi_scr, lo_scr, gmax_scr, width)
    hint_hi = hint_lo + 1

    def open_lanes(lo, hi):
        return jnp.max((hi > lo + 1).astype(F32)) > 0.0

    def cond(state):
        i, lo, hi, _ = state
        return (i < 2) | ((i < MAX_SEARCH_STEPS) & open_lanes(lo, hi))

    def step(state):
        i, lo, hi, n_lo = state
        mid = (lo >> 1) + (hi >> 1) + ((lo | hi) & 1)
        cand = jnp.where(i == 0, hint_lo, jnp.where(i == 1, hint_hi, mid))
        cand = jnp.minimum(jnp.maximum(cand, lo + 1), hi)
        cand_score = _score_of_key(cand)
        cnt = count(lambda kt, t: t >= cand_score)
        ge = cnt >= TOPK_MAX
        hit = cnt == TOPK_MAX
        lo_new = jnp.where(ge, cand, lo)
        hi_new = jnp.where(hit, cand + 1, jnp.where(ge, hi, cand))
        return i + 1, lo_new, hi_new, jnp.where(ge, cnt, n_lo)

    lo0 = jnp.full((1, width), KEY_NEG_INF, I32)
    hi0 = jnp.full((1, width), KEY_POS_INF + 1, I32)
    n0 = jnp.full((1, width), INT_MAX, I32)
    _, lo, _, n_lo = lax.while_loop(cond, step, (jnp.int32(0), lo0, hi0, n0))
    thr = _score_of_key(lo)
    thr_scr[...] = jnp.broadcast_to(thr, (SUBLANES, width))

    @pl.when(jnp.max((n_lo > TOPK_MAX).astype(F32)) > 0.0)
    def _():
        need = TOPK_MAX - count(lambda kt, t: t > thr)

        def pos_of(kt):
            return kt * tk + lax.broadcasted_iota(I32, (tk, width), 0)

        def pos_bisect(_, carry):
            plo, phi = carry
            mid = (plo + phi) >> 1
            ok = count(lambda kt, t: (t == thr) & (pos_of(kt) <= mid)) >= need
            return jnp.where(ok, plo, mid), jnp.where(ok, mid, phi)

        plo0 = jnp.full((1, width), -1, I32)
        phi0 = jnp.full((1, width), n_cols - 1, I32)
        steps = max(1, (n_cols - 1).bit_length()) + 1
        _, pos = lax.fori_loop(0, steps, pos_bisect, (plo0, phi0))

        def drop(kt, carry):
            t = sc_scr[kt]
            sc_scr[kt] = jnp.where((t == thr) & (pos_of(kt) > pos), NEG_INF, t)
            return carry

        for_tiles(drop, 0)


def _softmax_step(lg, m_ref, l_ref, acc_ref, pv_fn):
    m_prev = m_ref[0:1, :]
    m_new = jnp.maximum(m_prev, jnp.max(lg, axis=0, keepdims=True))
    m_safe = jnp.where(m_new == NEG_INF, 0.0, m_new)
    alpha = jnp.exp2(m_prev - m_safe)
    p = jnp.exp2(lg - m_safe)
    l_new = alpha * l_ref[0:1, :] + jnp.sum(p, axis=0, keepdims=True)
    acc_ref[...] = acc_ref[...] * alpha + pv_fn(p.astype(BF16))
    m_ref[...] = jnp.broadcast_to(m_new, m_ref.shape)
    l_ref[...] = jnp.broadcast_to(l_new, l_ref.shape)


def _attend_prompt_kernel(qi_ref, aux_ref, q_ref, ga_ref, kpad_ref, k_ref, vt_ref, z_ref, sc_scr,
                          hi_scr, lo_scr, gmax_scr, thr_scr, m_scr, l_scr, acc_scr, lg_scr, *, tq, tk, n_cols):
    j = pl.program_id(1)
    n_tiles = j + 1

    def for_tiles(fn, init, unroll=2):
        shift = unroll.bit_length() - 1
        assert unroll == 1 << shift
        n_main = n_tiles >> shift

        def body(i, carry):
            for u in range(unroll):
                carry = fn(unroll * i + u, carry)
            return carry

        carry = lax.fori_loop(0, n_main, body, init)
        return lax.fori_loop(n_main << shift, n_tiles, fn, carry)

    assert tq == tk == TOPK_MAX and tq % CHUNK == 0
    qi = qi_ref[...]
    qstack = jnp.concatenate([qi[:, u * QUAD_LANES:(u + 1) * QUAD_LANES] for u in range(N_QUADS)], axis=0)
    w_t = aux_ref[...].T
    w_rows = [w_t[IDX_DIM + h:IDX_DIM + h + 1, :] for h in range(N_IDX_HEADS)]
    key_chunk = lax.broadcasted_iota(I32, (tk, tq), 0) // CHUNK
    qry_chunk = lax.broadcasted_iota(I32, (tk, tq), 1) // CHUNK
    diag_adm = key_chunk <= qry_chunk

    def idx_tile(kt, carry):
        start = pl.multiple_of(kt * tk, tk)
        acc = jnp.zeros((tk, tq), F32)
        for c in range(HEADS_PER_QUAD):
            kp = kpad_ref[pl.ds(start, tk), c * QUAD_LANES:(c + 1) * QUAD_LANES]
            s = _dot_nt(kp, qstack)
            for u in range(N_QUADS):
                acc = acc + jnp.maximum(s[:, u * tq:(u + 1) * tq], 0.0) * w_rows[HEADS_PER_QUAD * u + c]
        _store_scores(acc * IDX_SCALE, (diag_adm & (kt == j)) | (kt < j), sc_scr, hi_scr, lo_scr, gmax_scr, kt)
        return carry

    gmax_scr[...] = jnp.full((tk, tq), NEG_INF, F32)
    for_tiles(idx_tile, 0)

    thr_scr[...] = jnp.full((SUBLANES, tq), F32_LOWEST, F32)

    @pl.when(j >= 1)
    def _():
        _topk_threshold(for_tiles, sc_scr, hi_scr, lo_scr, gmax_scr, thr_scr, tk=tk, width=tq, n_cols=n_cols)

    thr = thr_scr[0:1, :]

    q = q_ref[...]
    qn = [jnp.concatenate([q[:, (KV_GROUP * n + g) * HEAD_DIM:(KV_GROUP * n + g + 1) * HEAD_DIM]
                           for g in range(KV_GROUP)], axis=0) for n in range(N_KV_HEADS)]
    m_scr[...] = jnp.full(m_scr.shape, NEG_INF, F32)
    l_scr[...] = jnp.zeros(l_scr.shape, F32)
    acc_scr[...] = jnp.zeros(acc_scr.shape, F32)

    def att_tiles(kts):
        for slot, kt in enumerate(kts):
            start = pl.multiple_of(kt * tk, tk)
            for n in range(N_KV_HEADS):
                lg_scr[slot, n] = _dot_nt(k_ref[pl.ds(start, tk), n * HEAD_DIM:(n + 1) * HEAD_DIM], qn[n])
        for slot, kt in enumerate(kts):
            sel = sc_scr[kt] >= thr
            for n in range(N_KV_HEADS):
                lg = jnp.concatenate([jnp.where(sel, lg_scr[slot, n, :, g * tq:(g + 1) * tq], NEG_INF)
                                      for g in range(KV_GROUP)], axis=1)
                vt_n = vt_ref[kt, n * HEAD_DIM:(n + 1) * HEAD_DIM, :]
                _softmax_step(lg, m_scr.at[n], l_scr.at[n], acc_scr.at[n],
                              lambda p: jnp.dot(vt_n, p, preferred_element_type=F32))

    def att_group(size):
        def body(i, carry):
            att_tiles(tuple(i + u for u in range(size)))
            return carry

        return body

    done = 0
    size = ATT_GROUP
    while size >= 1:
        n_groups = (n_tiles - done) // size
        lax.fori_loop(0, n_groups, lambda g, c, size=size, done=done: att_group(size)(done + g * size, c), 0)
        done = done + n_groups * size
        size //= 2

    outs = []
    for n in range(N_KV_HEADS):
        o = acc_scr[n] / l_scr[n][0:1, :]
        outs.extend(o[:, g * tq:(g + 1) * tq].T for g in range(KV_GROUP))
    z_ref[...] = (jnp.concatenate(outs, axis=1) * _silu(ga_ref[...])).astype(BF16)


def _attend_prompt(qi, aux, q, ga, kpad, kb, vt, tq):
    nb, t, _ = q.shape
    tk = tq
    n_tiles = t // tk
    lanes_q = KV_GROUP * tq
    blk = lambda n: pl.BlockSpec((None, tq, n), lambda b, j: (b, j, 0))
    seq = lambda n: pl.BlockSpec((None, t, n), lambda b, j: (b, 0, 0))
    scratch = [pltpu.VMEM((n_tiles, tk, tq), F32), pltpu.VMEM((n_tiles, tk, tq), I16),
               pltpu.VMEM((n_tiles, tk, tq), I16), pltpu.VMEM((tk, tq), F32), pltpu.VMEM((SUBLANES, tq), F32),
               pltpu.VMEM((N_KV_HEADS, SUBLANES, lanes_q), F32),
               pltpu.VMEM((N_KV_HEADS, SUBLANES, lanes_q), F32),
               pltpu.VMEM((N_KV_HEADS, HEAD_DIM, lanes_q), F32),
               pltpu.VMEM((ATT_GROUP, N_KV_HEADS, tk, lanes_q), F32)]
    return pl.pallas_call(
        functools.partial(_attend_prompt_kernel, tq=tq, tk=tk, n_cols=t),
        grid=(nb, t // tq),
        in_specs=[blk(N_IDX_HEADS * IDX_DIM), blk(LANES), blk(D_ATTN), blk(D_ATTN),
                  seq(N_QUADS * QUAD_LANES), seq(KV_DIM),
                  pl.BlockSpec((None, n_tiles, KV_DIM, tk), lambda b, j: (b, 0, 0, 0))],
        out_specs=blk(D_ATTN),
        out_shape=jax.ShapeDtypeStruct((nb, t, D_ATTN), BF16),
        scratch_shapes=scratch,
        compiler_params=pltpu.CompilerParams(dimension_semantics=("arbitrary", "arbitrary"),
                                             vmem_limit_bytes=VMEM_LIMIT_BYTES),
        name="attend_prompt",
    )(qi, aux, q, ga, kpad, kb, vt)


def _attend_sample_kernel(qi_ref, aux_ref, q_ref, ga_ref, cikt_ref, ck_ref, cv_ref, nk_ref, nv_ref,
                          z_ref, kit_s, k_s, vt_s, sc_scr, hi_scr, lo_scr, gmax_scr, thr_scr, m_scr, l_scr,
                          acc_scr, *, tq, tk, past, n_tiles):
    n_keys = past + tq
    hl = N_IDX_HEADS * tq
    assert N_HEADS * tq == LANES and hl == 2 * LANES and past % tk == 0 and tq <= tk
    assert n_keys > TOPK_MAX and past % CHUNK == 0 and tq <= CHUNK

    def for_tiles(fn, init, unroll=None):
        for kt in range(n_tiles):
            init = fn(kt, init)
        return init

    aux = aux_ref[...]
    kit_s[:, 0:past] = cikt_ref[...].astype(BF16)
    kit_s[:, past:] = jnp.zeros((IDX_DIM, n_tiles * tk - past), BF16)
    kit_s[:, past:past + tq] = aux.T[:IDX_DIM, :].astype(BF16)
    k_s[past + tq:, :] = jnp.zeros((n_tiles * tk - n_keys, KV_DIM), BF16)
    vt_s[n_tiles - 1] = jnp.zeros((KV_DIM, tk), BF16)
    for n in range(N_KV_HEADS):
        cols = slice(n * HEAD_DIM, (n + 1) * HEAD_DIM)
        k_s[0:past, cols] = ck_ref[pl.ds(n, past, stride=N_KV_HEADS), :].astype(BF16)
        k_s[past:past + tq, cols] = nk_ref[pl.ds(n, tq, stride=N_KV_HEADS), :].astype(BF16)
        for kt in range(past // tk):
            v_tile = cv_ref[pl.ds(N_KV_HEADS * kt * tk + n, tk, stride=N_KV_HEADS), :]
            vt_s[kt, cols, :] = v_tile.T.astype(BF16)
        vt_s[n_tiles - 1, cols, 0:tq] = nv_ref[pl.ds(n, tq, stride=N_KV_HEADS), :].T.astype(BF16)

    qi = qi_ref[...]
    qrows = jnp.concatenate([qi[:, h * IDX_DIM:(h + 1) * IDX_DIM] for h in range(N_IDX_HEADS)], axis=0)
    w_rows = jnp.concatenate([aux[:, IDX_DIM + h:IDX_DIM + h + 1] for h in range(N_IDX_HEADS)], axis=0)
    w_rows = jnp.broadcast_to(w_rows, (hl, LANES))

    def idx_tile(kt, carry):
        s = jnp.dot(qrows, kit_s[:, kt * tk:(kt + 1) * tk], preferred_element_type=F32)
        y = jnp.concatenate([jnp.maximum(s[:, c:c + LANES], 0.0) * w_rows for c in range(0, tk, LANES)],
                            axis=1)
        per_query = _vreg_sum(y, tq) * IDX_SCALE
        score = jnp.concatenate([per_query] * N_HEADS, axis=0).T
        pos = kt * tk + lax.broadcasted_iota(I32, (tk, LANES), 0)
        _store_scores(score, pos < n_keys, sc_scr, hi_scr, lo_scr, gmax_scr, kt)
        return carry

    gmax_scr[...] = jnp.full((tk, LANES), NEG_INF, F32)
    for_tiles(idx_tile, 0)

    _topk_threshold(for_tiles, sc_scr, hi_scr, lo_scr, gmax_scr, thr_scr, tk=tk, width=LANES,
                    n_cols=n_tiles * tk)
    thr = thr_scr[0:1, :]

    q = q_ref[...]
    zeros = jnp.zeros((tq, HEAD_DIM), BF16)
    qblk = jnp.concatenate(
        [jnp.concatenate([q[:, (KV_GROUP * n + g) * HEAD_DIM:(KV_GROUP * n + g + 1) * HEAD_DIM]
                          if m == n else zeros for m in range(N_KV_HEADS)], axis=1)
         for n in range(N_KV_HEADS) for g in range(KV_GROUP)], axis=0)
    m_scr[...] = jnp.full(m_scr.shape, NEG_INF, F32)
    l_scr[...] = jnp.zeros(l_scr.shape, F32)
    acc_scr[...] = jnp.zeros(acc_scr.shape, F32)

    def att_tile(kt, carry):
        lg = jnp.where(sc_scr[kt] >= thr, _dot_nt(k_s[kt * tk:(kt + 1) * tk, :], qblk), NEG_INF)
        _softmax_step(lg, m_scr, l_scr, acc_scr, lambda p: jnp.dot(vt_s[kt], p, preferred_element_type=F32))
        return carry

    for_tiles(att_tile, 0)

    o_t = (acc_scr[...] / l_scr[0:1, :]).T
    outs = [o_t[(KV_GROUP * n + g) * tq:(KV_GROUP * n + g + 1) * tq, n * HEAD_DIM:(n + 1) * HEAD_DIM]
            for n in range(N_KV_HEADS) for g in range(KV_GROUP)]
    z_ref[...] = (jnp.concatenate(outs, axis=1) * _silu(ga_ref[...])).astype(BF16)


def _attend_sample(qi, aux, q, ga, cache_ik, cache_k, cache_v, new_k, new_v, tk):
    nseq, tq, _ = q.shape
    past = cache_ik.shape[1]
    cache_ikt = jnp.transpose(cache_ik, (0, 2, 1))
    n_tiles = -(-(past + tq) // tk)
    blk = lambda n, m=1: pl.BlockSpec((None, m * tq, n), lambda b: (b, 0, 0))
    cache = lambda n, m=1: pl.BlockSpec((None, m * past, n), lambda b: (b, 0, 0))
    scratch = [pltpu.VMEM((IDX_DIM, n_tiles * tk), BF16), pltpu.VMEM((n_tiles * tk, KV_DIM), BF16),
               pltpu.VMEM((n_tiles, KV_DIM, tk), BF16),
               pltpu.VMEM((n_tiles, tk, LANES), F32), pltpu.VMEM((n_tiles, tk, LANES), I16),
               pltpu.VMEM((n_tiles, tk, LANES), I16), pltpu.VMEM((tk, LANES), F32),
               pltpu.VMEM((SUBLANES, LANES), F32),
               pltpu.VMEM((SUBLANES, LANES), F32), pltpu.VMEM((SUBLANES, LANES), F32),
               pltpu.VMEM((KV_DIM, LANES), F32)]
    return pl.pallas_call(
        functools.partial(_attend_sample_kernel, tq=tq, tk=tk, past=past, n_tiles=n_tiles),
        grid=(nseq,),
        in_specs=[blk(N_IDX_HEADS * IDX_DIM), blk(LANES), blk(D_ATTN), blk(D_ATTN),
                  pl.BlockSpec((None, IDX_DIM, past), lambda b: (b, 0, 0)),
                  cache(HEAD_DIM, N_KV_HEADS), cache(HEAD_DIM, N_KV_HEADS), blk(HEAD_DIM, N_KV_HEADS),
                  blk(HEAD_DIM, N_KV_HEADS)],
        out_specs=blk(D_ATTN),
        out_shape=jax.ShapeDtypeStruct((nseq, tq, D_ATTN), BF16),
        scratch_shapes=scratch,
        compiler_params=pltpu.CompilerParams(dimension_semantics=("arbitrary",),
                                             vmem_limit_bytes=VMEM_LIMIT_BYTES),
        name="attend_sample",
    )(qi, aux, q, ga, cache_ikt, cache_k, cache_v, new_k, new_v)


def _merge_kernel(x_ref, za_ref, zc_ref, w_ref, g_ref, y_ref):
    z = jnp.concatenate([za_ref[...], zc_ref[...]], axis=1)
    y = jnp.dot(z, w_ref[...], preferred_element_type=F32)
    y_ref[...] = x_ref[...] + _rmsnorm(y, g_ref[...])


def _merge(x, za, zc, w_out, g_post, tm):
    rows = x.shape[0]
    row_spec = lambda n: pl.BlockSpec((tm, n), lambda i: (i, 0))
    return pl.pallas_call(
        _merge_kernel,
        grid=(rows // tm,),
        in_specs=[row_spec(D_MODEL), row_spec(D_ATTN), row_spec(D_CONV), _resident((D_MODEL, D_MODEL), 1),
                  _resident((1, D_MODEL), 1)],
        out_specs=row_spec(D_MODEL),
        out_shape=jax.ShapeDtypeStruct((rows, D_MODEL), F32),
        compiler_params=pltpu.CompilerParams(dimension_semantics=("arbitrary",),
                                             vmem_limit_bytes=VMEM_LIMIT_BYTES),
        name="merge",
    )(x, za, zc, w_out, g_post)


PROJECT_ROWS = 256
ATTEND_ROWS = 256
MERGE_ROWS = 512


def _layer(xp, xs, cache_k, cache_v, cache_ik, state, g_pre, w_in, w_conv, w_out, g_post):
    nb, t, _ = xp.shape
    nseq, seqlen, _ = xs.shape
    g_pre = g_pre.reshape(1, D_MODEL)
    g_post = g_post.reshape(1, D_MODEL)
    w_t = w_in.T.astype(BF16)
    w_out = w_out.astype(BF16)

    assert PROJECT_ROWS == ATTEND_ROWS
    q, k, v, ga, qi, ik, aux, zc, kpad, kb, vt, cs = _project_prompt(xp, g_pre, w_t, w_conv, PROJECT_ROWS)
    za = _attend_prompt(qi, aux, q, ga, kpad, kb, vt, ATTEND_ROWS)
    yp = _merge(xp.reshape(nb * t, D_MODEL), za.reshape(nb * t, D_ATTN), zc.reshape(nb * t, D_CONV),
                w_out, g_post, MERGE_ROWS).reshape(nb, t, D_MODEL)

    sq, sk, sv, sga, sqi, sik, saux, szc, scs = _project_sample(xs, g_pre, w_t, w_conv, state)
    per_seq = lambda a: a.reshape(nseq, -1, a.shape[-1])
    kv_rows = lambda a: a.reshape(nseq, -1, HEAD_DIM)
    sza = _attend_sample(per_seq(sqi), per_seq(saux), per_seq(sq), per_seq(sga), cache_ik,
                         kv_rows(cache_k), kv_rows(cache_v), per_seq(sk), per_seq(sv), ATTEND_ROWS)
    ys = _merge(xs.reshape(nseq * seqlen, D_MODEL), sza.reshape(nseq * seqlen, D_ATTN), szc, w_out,
                g_post, MERGE_ROWS).reshape(nseq, seqlen, D_MODEL)

    heads = lambda a, lead: a.reshape(lead + (N_KV_HEADS, HEAD_DIM))
    return (yp, ys, heads(k, (nb, t)), heads(v, (nb, t)), jnp.transpose(ik, (0, 2, 1)), cs,
            heads(sk, (nseq, seqlen)), heads(sv, (nseq, seqlen)), per_seq(sik), scs)


def kernel(x_prompt, x_sample, cache_k, cache_v, cache_idx_k, state_conv, g_pre, w_in, w_conv, w_out,
           g_post):
    depth = g_pre.shape[0]
    xp, xs = x_prompt, x_sample
    outs = []
    for l in range(depth):
        res = _layer(xp, xs, cache_k[l], cache_v[l], cache_idx_k[l], state_conv[l], g_pre[l], w_in[l],
                     w_conv[l], w_out[l], g_post[l])
        xp, xs = res[0], res[1]
        outs.append(res[2:])
    stacked = [jnp.stack([o[i] for o in outs]) for i in range(8)]
    return (xp, xs) + tuple(stacked)
```

```python
import functools

import jax
import jax.numpy as jnp
from jax import lax
from jax.experimental import pallas as pl
from jax.experimental.pallas import tpu as pltpu

F32 = jnp.float32
BF16 = jnp.bfloat16
I32 = jnp.int32
I16 = jnp.int16

D_MODEL = 2048
D_ATTN = 1024
D_CONV = 1024
HEAD_DIM = 128
N_KV_HEADS = 2
KV_GROUP = 4
N_HEADS = N_KV_HEADS * KV_GROUP
KV_DIM = N_KV_HEADS * HEAD_DIM
N_IDX_HEADS = 16
IDX_DIM = 64
TOPK_MAX = 256
CHUNK = 64
CONV_WIDTH = 3
RMS_EPS = 1e-6

LANES = 128
SUBLANES = 8
I16_ROWS = 2 * SUBLANES
VMEM_LIMIT_BYTES = 60 * 1000 * 1024

OFF_Q = 0
OFF_K = OFF_Q + D_ATTN
OFF_V = OFF_K + KV_DIM
OFF_GA = OFF_V + KV_DIM
OFF_QI = OFF_GA + D_ATTN
OFF_KW = OFF_QI + N_IDX_HEADS * IDX_DIM
OFF_B = OFF_KW + IDX_DIM + N_IDX_HEADS
OFF_C = OFF_B + D_CONV
OFF_HC = OFF_C + D_CONV
OFF_GB = OFF_HC + D_CONV
D_PROJ = OFF_GB + D_CONV
assert all(o % I16_ROWS == 0 for o in (OFF_K, OFF_V, OFF_GA, OFF_QI, OFF_KW, OFF_B, OFF_C, OFF_HC, OFF_GB))

CONV_COLS = 256
PAD_ROWS = SUBLANES
HEADS_PER_QUAD = 4
QUAD_LANES = HEADS_PER_QUAD * IDX_DIM
N_QUADS = N_IDX_HEADS // HEADS_PER_QUAD
IDX_SCALE = (IDX_DIM ** -0.5) * (N_IDX_HEADS ** -0.5)
LOG2E = 1.4426950408889634
ATT_SCALE_LOG2 = HEAD_DIM ** -0.5 * LOG2E
INT_MAX = 2 ** 31 - 1
HALF_BITS = 16
HALF_MASK = 2 ** HALF_BITS - 1
HALF_MIN = -(2 ** (HALF_BITS - 1))
HALF_MAX = 2 ** (HALF_BITS - 1) - 1
KEY_POS_INF = 0x7F800000
KEY_NEG_INF = -KEY_POS_INF - 1
MAX_SEARCH_STEPS = 36
F32_LOWEST = -3.4028234663852886e38
COUNT_UNROLL = 4
ATT_GROUP = 4
IDX_UNROLL = 4
NEG_INF = float("-inf")


def _silu(x):
    return x * jax.nn.sigmoid(x)


def _dot_nt(a, b):
    return lax.dot_general(a, b, (((1,), (1,)), ((), ())), preferred_element_type=F32)


def _rmsnorm(x, g):
    ms = jnp.mean(x * x, axis=-1, keepdims=True)
    return x * lax.rsqrt(ms + RMS_EPS) * g


def _project_rows(h, w_ref, q_ref, k_ref, v_ref, ga_ref, qi_ref, ik_ref, aux_ref):
    rows = h.shape[0]

    def mm(r0, n):
        return _dot_nt(h, w_ref[r0:r0 + n, :])

    q_ref[...] = (mm(OFF_Q, D_ATTN) * ATT_SCALE_LOG2).astype(BF16)
    kk = mm(OFF_K, KV_DIM)
    vv = mm(OFF_V, KV_DIM)
    for n in range(N_KV_HEADS):
        k_ref[pl.ds(n, rows, stride=N_KV_HEADS), :] = kk[:, n * HEAD_DIM:(n + 1) * HEAD_DIM]
        v_ref[pl.ds(n, rows, stride=N_KV_HEADS), :] = vv[:, n * HEAD_DIM:(n + 1) * HEAD_DIM]
    ga_ref[...] = mm(OFF_GA, D_ATTN)
    qi_ref[...] = mm(OFF_QI, N_IDX_HEADS * IDX_DIM).astype(BF16)
    kw = mm(OFF_KW, LANES)
    aux_ref[...] = kw
    if ik_ref.shape[0] == IDX_DIM:
        ik_ref[...] = kw.T[:IDX_DIM, :]
    else:
        ik_ref[...] = kw[:, :IDX_DIM]
    return kk, vv, kw


def _conv_chunk(h, w_ref, c):
    def mm(off):
        return _dot_nt(h, w_ref[off + c:off + c + CONV_COLS, :])

    return mm(OFF_B), mm(OFF_C) * mm(OFF_HC), mm(OFF_GB)


def _conv_out(bg, gb, u, um1, um2, wc_ref, c):
    w0 = wc_ref[0:1, c:c + CONV_COLS]
    w1 = wc_ref[1:2, c:c + CONV_COLS]
    w2 = wc_ref[2:3, c:c + CONV_COLS]
    conv = w0 * um2 + w1 * um1 + w2 * u
    return (bg * conv * _silu(gb)).astype(BF16)


def _project_prompt_kernel(x_ref, g_ref, w_ref, wc_ref, q_ref, k_ref, v_ref, ga_ref, qi_ref, ik_ref,
                           aux_ref, zc_ref, kpad_ref, kb_ref, vt_ref, cs_ref, upad_ref, *, tm):
    @pl.when(pl.program_id(1) == 0)
    def _():
        upad_ref[0:PAD_ROWS, :] = jnp.zeros((PAD_ROWS, D_CONV), F32)

    h = _rmsnorm(x_ref[...], g_ref[...]).astype(BF16)
    kk, vv, kw = _project_rows(h, w_ref, q_ref, k_ref, v_ref, ga_ref, qi_ref, ik_ref, aux_ref)
    kb_ref[...] = kk.astype(BF16)
    vt_ref[...] = vv.T.astype(BF16)
    lane = lax.broadcasted_iota(I32, kw.shape, 1)
    lo = jnp.where(lane < IDX_DIM, kw, 0.0)
    hi = pltpu.roll(lo, IDX_DIM, axis=1)
    zero = jnp.zeros_like(lo)
    for i, blk in enumerate((lo, zero, hi, zero, zero, lo, zero, hi)):
        kpad_ref[:, i * LANES:(i + 1) * LANES] = blk.astype(BF16)

    for c in range(0, D_CONV, CONV_COLS):
        bg, u, gb = _conv_chunk(h, w_ref, c)
        upad_ref[PAD_ROWS:PAD_ROWS + tm, c:c + CONV_COLS] = u
        um1 = upad_ref[PAD_ROWS - 1:PAD_ROWS - 1 + tm, c:c + CONV_COLS]
        um2 = upad_ref[PAD_ROWS - 2:PAD_ROWS - 2 + tm, c:c + CONV_COLS]
        zc_ref[:, c:c + CONV_COLS] = _conv_out(bg, gb, u, um1, um2, wc_ref, c)
    last = upad_ref[PAD_ROWS + tm - (CONV_WIDTH - 1):PAD_ROWS + tm, :]
    cs_ref[...] = last
    upad_ref[PAD_ROWS - (CONV_WIDTH - 1):PAD_ROWS, :] = last


def _project_sample_kernel(x_ref, g_ref, w_ref, wc_ref, st_ref, q_ref, k_ref, v_ref, ga_ref, qi_ref,
                           ik_ref, aux_ref, zc_ref, cs_ref, upad_ref, *, nseq, seqlen):
    rows = nseq * seqlen
    upad_ref[:, PAD_ROWS - (CONV_WIDTH - 1):PAD_ROWS, :] = st_ref[...]
    h = _rmsnorm(x_ref[...], g_ref[...]).astype(BF16)
    _project_rows(h, w_ref, q_ref, k_ref, v_ref, ga_ref, qi_ref, ik_ref, aux_ref)
    for c in range(0, D_CONV, CONV_COLS):
        bg, u, gb = _conv_chunk(h, w_ref, c)
        upad_ref[:, PAD_ROWS:PAD_ROWS + seqlen, c:c + CONV_COLS] = u.reshape(nseq, seqlen, CONV_COLS)
        um1 = upad_ref[:, PAD_ROWS - 1:PAD_ROWS - 1 + seqlen, c:c + CONV_COLS].reshape(rows, CONV_COLS)
        um2 = upad_ref[:, PAD_ROWS - 2:PAD_ROWS - 2 + seqlen, c:c + CONV_COLS].reshape(rows, CONV_COLS)
        zc_ref[:, c:c + CONV_COLS] = _conv_out(bg, gb, u, um1, um2, wc_ref, c)
    cs_ref[...] = upad_ref[:, PAD_ROWS + seqlen - (CONV_WIDTH - 1):PAD_ROWS + seqlen, :]


_PROJECT_OUTS = ((1, D_ATTN, BF16), (N_KV_HEADS, HEAD_DIM, F32), (N_KV_HEADS, HEAD_DIM, F32),
                 (1, D_ATTN, F32), (1, N_IDX_HEADS * IDX_DIM, BF16), (1, IDX_DIM, F32), (1, LANES, F32),
                 (1, D_CONV, BF16))
_IK_OUT = 5


def _project_out_shapes(lead, rows):
    return [jax.ShapeDtypeStruct(lead + (m * rows, n), dt) for m, n, dt in _PROJECT_OUTS]


def _resident(shape, ngrid):
    zeros = (0,) * len(shape)
    if ngrid == 1:
        return pl.BlockSpec(shape, lambda i: zeros, pipeline_mode=pl.Buffered(1))
    return pl.BlockSpec(shape, lambda b, i: zeros, pipeline_mode=pl.Buffered(1))


def _project_prompt(x, g_pre, w_t, w_conv, tm):
    nb, t, _ = x.shape
    grid = (nb, t // tm)
    row_spec = lambda n, m=1: pl.BlockSpec((None, m * tm, n), lambda b, i: (b, i, 0))
    out_shapes = _project_out_shapes((nb,), t) + [
        jax.ShapeDtypeStruct((nb, t, N_QUADS * QUAD_LANES), BF16),
        jax.ShapeDtypeStruct((nb, t, KV_DIM), BF16),
        jax.ShapeDtypeStruct((nb, t // tm, KV_DIM, tm), BF16),
        jax.ShapeDtypeStruct((nb, CONV_WIDTH - 1, D_CONV), F32)]
    out_specs = [row_spec(n, m) for m, n, _ in _PROJECT_OUTS]
    out_shapes[_IK_OUT] = jax.ShapeDtypeStruct((nb, IDX_DIM, t), F32)
    out_specs[_IK_OUT] = pl.BlockSpec((None, IDX_DIM, tm), lambda b, i: (b, 0, i))
    out_specs += [row_spec(N_QUADS * QUAD_LANES), row_spec(KV_DIM),
                  pl.BlockSpec((None, None, KV_DIM, tm), lambda b, i: (b, i, 0, 0)),
                  pl.BlockSpec((None, CONV_WIDTH - 1, D_CONV), lambda b, i: (b, 0, 0))]
    return pl.pallas_call(
        functools.partial(_project_prompt_kernel, tm=tm),
        grid=grid,
        in_specs=[row_spec(D_MODEL), _resident((1, D_MODEL), 2), _resident((D_PROJ, D_MODEL), 2),
                  _resident((CONV_WIDTH, D_CONV), 2)],
        out_specs=out_specs,
        out_shape=out_shapes,
        scratch_shapes=[pltpu.VMEM((PAD_ROWS + tm, D_CONV), F32)],
        compiler_params=pltpu.CompilerParams(dimension_semantics=("arbitrary", "arbitrary"),
                                             vmem_limit_bytes=VMEM_LIMIT_BYTES),
        name="project_prompt",
    )(x, g_pre, w_t, w_conv)


def _project_sample(x, g_pre, w_t, w_conv, state):
    nseq, seqlen, _ = x.shape
    rows = nseq * seqlen
    full = lambda shape: pl.BlockSpec(shape, lambda i: (0,) * len(shape))
    out_shapes = _project_out_shapes((), rows) + [jax.ShapeDtypeStruct((nseq, CONV_WIDTH - 1, D_CONV), F32)]
    out_specs = [full((m * rows, n)) for m, n, _ in _PROJECT_OUTS] + [full((nseq, CONV_WIDTH - 1, D_CONV))]
    return pl.pallas_call(
        functools.partial(_project_sample_kernel, nseq=nseq, seqlen=seqlen),
        grid=(1,),
        in_specs=[full((rows, D_MODEL)), full((1, D_MODEL)), _resident((D_PROJ, D_MODEL), 1),
                  full((CONV_WIDTH, D_CONV)), full((nseq, CONV_WIDTH - 1, D_CONV))],
        out_specs=out_specs,
        out_shape=out_shapes,
        scratch_shapes=[pltpu.VMEM((nseq, PAD_ROWS + seqlen, D_CONV), F32)],
        compiler_params=pltpu.CompilerParams(dimension_semantics=("arbitrary",),
                                             vmem_limit_bytes=VMEM_LIMIT_BYTES),
        name="project_sample",
    )(x.reshape(rows, D_MODEL), g_pre, w_t, w_conv, state)


def _key_of_score(x):
    bits = lax.bitcast_convert_type(x, I32)
    return bits ^ ((bits >> 31) & INT_MAX)


def _score_of_key(key):
    return lax.bitcast_convert_type(key ^ ((key >> 31) & INT_MAX), F32)


def _store_scores(score, admissible, sc_scr, hi_scr, lo_scr, kt):
    score = jnp.where(admissible, score, NEG_INF)
    sc_scr[kt] = score
    key = _key_of_score(score)
    hi_scr[kt] = (key >> HALF_BITS).astype(I16)
    lo_scr[kt] = key.astype(I16) ^ jnp.int16(HALF_MIN)


def _vreg_sum(p, rows):
    parts = [p[r:r + rows, :] for r in range(0, p.shape[0], rows)]
    while len(parts) > 1:
        parts = [a + b for a, b in zip(parts[::2], parts[1::2])] + parts[len(parts) & ~1:]
    return parts[0]


def _threshold_key_guess(for_tiles, hi_scr, lo_scr, width):
    def count_half(ref, th):
        def body(kt, c16):
            return c16 + _vreg_sum(jnp.where(ref[kt] >= th, jnp.int16(1), jnp.int16(0)), I16_ROWS)

        c16 = for_tiles(body, jnp.zeros((I16_ROWS, width), I16), COUNT_UNROLL)
        return jnp.sum(c16.astype(I32), axis=0, keepdims=True)

    def bisect_half(ref):
        def step(_, carry):
            lo, hi = carry
            mid = (lo + hi + 1) >> 1
            cnt = count_half(ref, jnp.minimum(mid, HALF_MAX).astype(I16))
            ge = (cnt >= TOPK_MAX) & (mid <= HALF_MAX)
            return jnp.where(ge, mid, lo), jnp.where(ge, hi, mid)

        lo0 = jnp.full((1, width), HALF_MIN, I32)
        hi0 = jnp.full((1, width), HALF_MAX + 1, I32)
        return lax.fori_loop(0, HALF_BITS, step, (lo0, hi0))[0]

    top = bisect_half(hi_scr)
    top16 = top.astype(I16)

    def narrow(kt, carry):
        h = hi_scr[kt]
        lo_scr[kt] = jnp.where(h == top16, lo_scr[kt],
                               jnp.where(h > top16, jnp.int16(HALF_MAX), jnp.int16(HALF_MIN)))
        return carry

    for_tiles(narrow, 0)
    bot = bisect_half(lo_scr)
    return top * (HALF_MASK + 1) + (bot - HALF_MIN)


def _topk_threshold(for_tiles, sc_scr, hi_scr, lo_scr, thr_scr, *, tk, width, n_cols):
    def count(pred_fn):
        def body(kt, c8):
            return c8 + _vreg_sum(pred_fn(kt, sc_scr[kt]).astype(I32), SUBLANES)

        c8 = for_tiles(body, jnp.zeros((SUBLANES, width), I32), COUNT_UNROLL)
        return jnp.sum(c8, axis=0, keepdims=True)

    hint_lo = _threshold_key_guess(for_tiles, hi_scr, lo_scr, width)
    hint_hi = hint_lo + 1

    def open_lanes(lo, hi):
        return jnp.max((hi > lo + 1).astype(F32)) > 0.0

    def cond(state):
        i, lo, hi, _ = state
        return (i < 2) | ((i < MAX_SEARCH_STEPS) & open_lanes(lo, hi))

    def step(state):
        i, lo, hi, n_lo = state
        mid = (lo >> 1) + (hi >> 1) + ((lo | hi) & 1)
        cand = jnp.where(i == 0, hint_lo, jnp.where(i == 1, hint_hi, mid))
        cand = jnp.minimum(jnp.maximum(cand, lo + 1), hi)
        cand_score = _score_of_key(cand)
        cnt = count(lambda kt, t: t >= cand_score)
        ge = cnt >= TOPK_MAX
        hit = cnt == TOPK_MAX
        lo_new = jnp.where(ge, cand, lo)
        hi_new = jnp.where(hit, cand + 1, jnp.where(ge, hi, cand))
        return i + 1, lo_new, hi_new, jnp.where(ge, cnt, n_lo)

    lo0 = jnp.full((1, width), KEY_NEG_INF, I32)
    hi0 = jnp.full((1, width), KEY_POS_INF + 1, I32)
    n0 = jnp.full((1, width), INT_MAX, I32)
    _, lo, _, n_lo = lax.while_loop(cond, step, (jnp.int32(0), lo0, hi0, n0))
    thr = _score_of_key(lo)
    thr_scr[...] = jnp.broadcast_to(thr, (SUBLANES, width))

    @pl.when(jnp.max((n_lo > TOPK_MAX).astype(F32)) > 0.0)
    def _():
        need = TOPK_MAX - count(lambda kt, t: t > thr)

        def pos_of(kt):
            return kt * tk + lax.broadcasted_iota(I32, (tk, width), 0)

        def pos_bisect(_, carry):
            plo, phi = carry
            mid = (plo + phi) >> 1
            ok = count(lambda kt, t: (t == thr) & (pos_of(kt) <= mid)) >= need
            return jnp.where(ok, plo, mid), jnp.where(ok, mid, phi)

        plo0 = jnp.full((1, width), -1, I32)
        phi0 = jnp.full((1, width), n_cols - 1, I32)
        steps = max(1, (n_cols - 1).bit_length()) + 1
        _, pos = lax.fori_loop(0, steps, pos_bisect, (plo0, phi0))

        def drop(kt, carry):
            t = sc_scr[kt]
            sc_scr[kt] = jnp.where((t == thr) & (pos_of(kt) > pos), NEG_INF, t)
            return carry

        for_tiles(drop, 0)


def _softmax_step(lg, m_ref, l_ref, acc_ref, pv_fn):
    m_prev = m_ref[0:1, :]
    m_new = jnp.maximum(m_prev, jnp.max(lg, axis=0, keepdims=True))
    m_safe = jnp.where(m_new == NEG_INF, 0.0, m_new)
    alpha = jnp.exp2(m_prev - m_safe)
    p = jnp.exp2(lg - m_safe)
    l_new = alpha * l_ref[0:1, :] + jnp.sum(p, axis=0, keepdims=True)
    acc_ref[...] = acc_ref[...] * alpha + pv_fn(p.astype(BF16))
    m_ref[...] = jnp.broadcast_to(m_new, m_ref.shape)
    l_ref[...] = jnp.broadcast_to(l_new, l_ref.shape)


def _attend_prompt_kernel(qi_ref, aux_ref, q_ref, ga_ref, kpad_ref, k_ref, vt_ref, z_ref, sc_scr,
                          hi_scr, lo_scr, thr_scr, m_scr, l_scr, acc_scr, lg_scr, *, tq, tk, n_cols):
    j = pl.program_id(1)
    n_tiles = j + 1

    def for_tiles(fn, init, unroll=2, halve=False):
        shift = unroll.bit_length() - 1
        assert unroll == 1 << shift
        done, carry = 0, init
        for size in ([unroll >> s for s in range(shift + 1)] if halve else [unroll, 1]):
            n_groups = (n_tiles - done) // size

            def body(g, carry, size=size, done=done):
                for u in range(size):
                    carry = fn(done + g * size + u, carry)
                return carry

            carry = lax.fori_loop(0, n_groups, body, carry)
            done = done + n_groups * size
        return carry

    assert tq == tk == TOPK_MAX and tq % CHUNK == 0
    qi = qi_ref[...]
    qstack = jnp.concatenate([qi[:, u * QUAD_LANES:(u + 1) * QUAD_LANES] for u in range(N_QUADS)], axis=0)
    w_t = aux_ref[...].T
    w_rows = [w_t[IDX_DIM + h:IDX_DIM + h + 1, :] for h in range(N_IDX_HEADS)]
    key_chunk = lax.broadcasted_iota(I32, (tk, tq), 0) // CHUNK
    qry_chunk = lax.broadcasted_iota(I32, (tk, tq), 1) // CHUNK
    diag_adm = key_chunk <= qry_chunk

    def idx_tile(kt, carry):
        start = pl.multiple_of(kt * tk, tk)
        acc = jnp.zeros((tk, tq), F32)
        for c in range(HEADS_PER_QUAD):
            kp = kpad_ref[pl.ds(start, tk), c * QUAD_LANES:(c + 1) * QUAD_LANES]
            s = _dot_nt(kp, qstack)
            for u in range(N_QUADS):
                acc = acc + jnp.maximum(s[:, u * tq:(u + 1) * tq], 0.0) * w_rows[HEADS_PER_QUAD * u + c]
        _store_scores(acc * IDX_SCALE, (diag_adm & (kt == j)) | (kt < j), sc_scr, hi_scr, lo_scr, kt)
        return carry

    for_tiles(idx_tile, 0, IDX_UNROLL, halve=True)

    thr_scr[...] = jnp.full((SUBLANES, tq), F32_LOWEST, F32)

    @pl.when(j >= 1)
    def _():
        _topk_threshold(for_tiles, sc_scr, hi_scr, lo_scr, thr_scr, tk=tk, width=tq, n_cols=n_cols)

    thr = thr_scr[0:1, :]

    q = q_ref[...]
    qn = [jnp.concatenate([q[:, (KV_GROUP * n + g) * HEAD_DIM:(KV_GROUP * n + g + 1) * HEAD_DIM]
                           for g in range(KV_GROUP)], axis=0) for n in range(N_KV_HEADS)]
    m_scr[...] = jnp.full(m_scr.shape, NEG_INF, F32)
    l_scr[...] = jnp.zeros(l_scr.shape, F32)
    acc_scr[...] = jnp.zeros(acc_scr.shape, F32)

    def att_tiles(kts):
        for slot, kt in enumerate(kts):
            start = pl.multiple_of(kt * tk, tk)
            for n in range(N_KV_HEADS):
                lg_scr[slot, n] = _dot_nt(k_ref[pl.ds(start, tk), n * HEAD_DIM:(n + 1) * HEAD_DIM], qn[n])
        for slot, kt in enumerate(kts):
            sel = sc_scr[kt] >= thr
            for n in range(N_KV_HEADS):
                lg = jnp.concatenate([jnp.where(sel, lg_scr[slot, n, :, g * tq:(g + 1) * tq], NEG_INF)
                                      for g in range(KV_GROUP)], axis=1)
                vt_n = vt_ref[kt, n * HEAD_DIM:(n + 1) * HEAD_DIM, :]
                _softmax_step(lg, m_scr.at[n], l_scr.at[n], acc_scr.at[n],
                              lambda p: jnp.dot(vt_n, p, preferred_element_type=F32))

    def att_group(size):
        def body(i, carry):
            att_tiles(tuple(i + u for u in range(size)))
            return carry

        return body

    done = 0
    size = ATT_GROUP
    while size >= 1:
        n_groups = (n_tiles - done) // size
        lax.fori_loop(0, n_groups, lambda g, c, size=size, done=done: att_group(size)(done + g * size, c), 0)
        done = done + n_groups * size
        size //= 2

    outs = []
    for n in range(N_KV_HEADS):
        o = acc_scr[n] / l_scr[n][0:1, :]
        outs.extend(o[:, g * tq:(g + 1) * tq].T for g in range(KV_GROUP))
    z_ref[...] = (jnp.concatenate(outs, axis=1) * _silu(ga_ref[...])).astype(BF16)


def _attend_prompt(qi, aux, q, ga, kpad, kb, vt, tq):
    nb, t, _ = q.shape
    tk = tq
    n_tiles = t // tk
    lanes_q = KV_GROUP * tq
    blk = lambda n: pl.BlockSpec((None, tq, n), lambda b, j: (b, j, 0))
    seq = lambda n: pl.BlockSpec((None, t, n), lambda b, j: (b, 0, 0))
    scratch = [pltpu.VMEM((n_tiles, tk, tq), F32), pltpu.VMEM((n_tiles, tk, tq), I16),
               pltpu.VMEM((n_tiles, tk, tq), I16), pltpu.VMEM((SUBLANES, tq), F32),
               pltpu.VMEM((N_KV_HEADS, SUBLANES, lanes_q), F32),
               pltpu.VMEM((N_KV_HEADS, SUBLANES, lanes_q), F32),
               pltpu.VMEM((N_KV_HEADS, HEAD_DIM, lanes_q), F32),
               pltpu.VMEM((ATT_GROUP, N_KV_HEADS, tk, lanes_q), F32)]
    return pl.pallas_call(
        functools.partial(_attend_prompt_kernel, tq=tq, tk=tk, n_cols=t),
        grid=(nb, t // tq),
        in_specs=[blk(N_IDX_HEADS * IDX_DIM), blk(LANES), blk(D_ATTN), blk(D_ATTN),
                  seq(N_QUADS * QUAD_LANES), seq(KV_DIM),
                  pl.BlockSpec((None, n_tiles, KV_DIM, tk), lambda b, j: (b, 0, 0, 0))],
        out_specs=blk(D_ATTN),
        out_shape=jax.ShapeDtypeStruct((nb, t, D_ATTN), BF16),
        scratch_shapes=scratch,
        compiler_params=pltpu.CompilerParams(dimension_semantics=("arbitrary", "arbitrary"),
                                             vmem_limit_bytes=VMEM_LIMIT_BYTES),
        name="attend_prompt",
    )(qi, aux, q, ga, kpad, kb, vt)


def _attend_sample_kernel(qi_ref, aux_ref, q_ref, ga_ref, cikt_ref, ck_ref, cv_ref, nk_ref, nv_ref,
                          z_ref, kit_s, k_s, vt_s, sc_scr, hi_scr, lo_scr, thr_scr, m_scr, l_scr, acc_scr,
                          *, tq, tk, past, n_tiles):
    n_keys = past + tq
    hl = N_IDX_HEADS * tq
    assert N_HEADS * tq == LANES and hl == 2 * LANES and past % tk == 0 and tq <= tk
    assert n_keys > TOPK_MAX and past % CHUNK == 0 and tq <= CHUNK

    def for_tiles(fn, init, unroll=None):
        for kt in range(n_tiles):
            init = fn(kt, init)
        return init

    aux = aux_ref[...]
    kit_s[:, 0:past] = cikt_ref[...].astype(BF16)
    kit_s[:, past:] = jnp.zeros((IDX_DIM, n_tiles * tk - past), BF16)
    kit_s[:, past:past + tq] = aux.T[:IDX_DIM, :].astype(BF16)
    k_s[past + tq:, :] = jnp.zeros((n_tiles * tk - n_keys, KV_DIM), BF16)
    vt_s[n_tiles - 1] = jnp.zeros((KV_DIM, tk), BF16)
    for n in range(N_KV_HEADS):
        cols = slice(n * HEAD_DIM, (n + 1) * HEAD_DIM)
        k_s[0:past, cols] = ck_ref[pl.ds(n, past, stride=N_KV_HEADS), :].astype(BF16)
        k_s[past:past + tq, cols] = nk_ref[pl.ds(n, tq, stride=N_KV_HEADS), :].astype(BF16)
        for kt in range(past // tk):
            v_tile = cv_ref[pl.ds(N_KV_HEADS * kt * tk + n, tk, stride=N_KV_HEADS), :]
            vt_s[kt, cols, :] = v_tile.T.astype(BF16)
        vt_s[n_tiles - 1, cols, 0:tq] = nv_ref[pl.ds(n, tq, stride=N_KV_HEADS), :].T.astype(BF16)

    qi = qi_ref[...]
    qrows = jnp.concatenate([qi[:, h * IDX_DIM:(h + 1) * IDX_DIM] for h in range(N_IDX_HEADS)], axis=0)
    w_rows = jnp.concatenate([aux[:, IDX_DIM + h:IDX_DIM + h + 1] for h in range(N_IDX_HEADS)], axis=0)
    w_rows = jnp.broadcast_to(w_rows, (hl, LANES))

    def idx_tile(kt, carry):
        s = jnp.dot(qrows, kit_s[:, kt * tk:(kt + 1) * tk], preferred_element_type=F32)
        y = jnp.concatenate([jnp.maximum(s[:, c:c + LANES], 0.0) * w_rows for c in range(0, tk, LANES)],
                            axis=1)
        per_query = _vreg_sum(y, tq) * IDX_SCALE
        score = jnp.concatenate([per_query] * N_HEADS, axis=0).T
        pos = kt * tk + lax.broadcasted_iota(I32, (tk, LANES), 0)
        _store_scores(score, pos < n_keys, sc_scr, hi_scr, lo_scr, kt)
        return carry

    for_tiles(idx_tile, 0)

    _topk_threshold(for_tiles, sc_scr, hi_scr, lo_scr, thr_scr, tk=tk, width=LANES, n_cols=n_tiles * tk)
    thr = thr_scr[0:1, :]

    q = q_ref[...]
    zeros = jnp.zeros((tq, HEAD_DIM), BF16)
    qblk = jnp.concatenate(
        [jnp.concatenate([q[:, (KV_GROUP * n + g) * HEAD_DIM:(KV_GROUP * n + g + 1) * HEAD_DIM]
                          if m == n else zeros for m in range(N_KV_HEADS)], axis=1)
         for n in range(N_KV_HEADS) for g in range(KV_GROUP)], axis=0)
    m_scr[...] = jnp.full(m_scr.shape, NEG_INF, F32)
    l_scr[...] = jnp.zeros(l_scr.shape, F32)
    acc_scr[...] = jnp.zeros(acc_scr.shape, F32)

    def att_tile(kt, carry):
        lg = jnp.where(sc_scr[kt] >= thr, _dot_nt(k_s[kt * tk:(kt + 1) * tk, :], qblk), NEG_INF)
        _softmax_step(lg, m_scr, l_scr, acc_scr, lambda p: jnp.dot(vt_s[kt], p, preferred_element_type=F32))
        return carry

    for_tiles(att_tile, 0)

    o_t = (acc_scr[...] / l_scr[0:1, :]).T
    outs = [o_t[(KV_GROUP * n + g) * tq:(KV_GROUP * n + g + 1) * tq, n * HEAD_DIM:(n + 1) * HEAD_DIM]
            for n in range(N_KV_HEADS) for g in range(KV_GROUP)]
    z_ref[...] = (jnp.concatenate(outs, axis=1) * _silu(ga_ref[...])).astype(BF16)


def _attend_sample(qi, aux, q, ga, cache_ik, cache_k, cache_v, new_k, new_v, tk):
    nseq, tq, _ = q.shape
    past = cache_ik.shape[1]
    cache_ikt = jnp.transpose(cache_ik, (0, 2, 1))
    n_tiles = -(-(past + tq) // tk)
    blk = lambda n, m=1: pl.BlockSpec((None, m * tq, n), lambda b: (b, 0, 0))
    cache = lambda n, m=1: pl.BlockSpec((None, m * past, n), lambda b: (b, 0, 0))
    scratch = [pltpu.VMEM((IDX_DIM, n_tiles * tk), BF16), pltpu.VMEM((n_tiles * tk, KV_DIM), BF16),
               pltpu.VMEM((n_tiles, KV_DIM, tk), BF16),
               pltpu.VMEM((n_tiles, tk, LANES), F32), pltpu.VMEM((n_tiles, tk, LANES), I16),
               pltpu.VMEM((n_tiles, tk, LANES), I16),
               pltpu.VMEM((SUBLANES, LANES), F32),
               pltpu.VMEM((SUBLANES, LANES), F32), pltpu.VMEM((SUBLANES, LANES), F32),
               pltpu.VMEM((KV_DIM, LANES), F32)]
    return pl.pallas_call(
        functools.partial(_attend_sample_kernel, tq=tq, tk=tk, past=past, n_tiles=n_tiles),
        grid=(nseq,),
        in_specs=[blk(N_IDX_HEADS * IDX_DIM), blk(LANES), blk(D_ATTN), blk(D_ATTN),
                  pl.BlockSpec((None, IDX_DIM, past), lambda b: (b, 0, 0)),
                  cache(HEAD_DIM, N_KV_HEADS), cache(HEAD_DIM, N_KV_HEADS), blk(HEAD_DIM, N_KV_HEADS),
                  blk(HEAD_DIM, N_KV_HEADS)],
        out_specs=blk(D_ATTN),
        out_shape=jax.ShapeDtypeStruct((nseq, tq, D_ATTN), BF16),
        scratch_shapes=scratch,
        compiler_params=pltpu.CompilerParams(dimension_semantics=("arbitrary",),
                                             vmem_limit_bytes=VMEM_LIMIT_BYTES),
        name="attend_sample",
    )(qi, aux, q, ga, cache_ikt, cache_k, cache_v, new_k, new_v)


def _merge_kernel(x_ref, za_ref, zc_ref, w_ref, g_ref, y_ref):
    z = jnp.concatenate([za_ref[...], zc_ref[...]], axis=1)
    y = jnp.dot(z, w_ref[...], preferred_element_type=F32)
    y_ref[...] = x_ref[...] + _rmsnorm(y, g_ref[...])


def _merge(x, za, zc, w_out, g_post, tm):
    rows = x.shape[0]
    row_spec = lambda n: pl.BlockSpec((tm, n), lambda i: (i, 0))
    return pl.pallas_call(
        _merge_kernel,
        grid=(rows // tm,),
        in_specs=[row_spec(D_MODEL), row_spec(D_ATTN), row_spec(D_CONV), _resident((D_MODEL, D_MODEL), 1),
                  _resident((1, D_MODEL), 1)],
        out_specs=row_spec(D_MODEL),
        out_shape=jax.ShapeDtypeStruct((rows, D_MODEL), F32),
        compiler_params=pltpu.CompilerParams(dimension_semantics=("arbitrary",),
                                             vmem_limit_bytes=VMEM_LIMIT_BYTES),
        name="merge",
    )(x, za, zc, w_out, g_post)


PROJECT_ROWS = 256
ATTEND_ROWS = 256
MERGE_ROWS = 512


def _layer(xp, xs, cache_k, cache_v, cache_ik, state, g_pre, w_in, w_conv, w_out, g_post):
    nb, t, _ = xp.shape
    nseq, seqlen, _ = xs.shape
    g_pre = g_pre.reshape(1, D_MODEL)
    g_post = g_post.reshape(1, D_MODEL)
    w_t = w_in.T.astype(BF16)
    w_out = w_out.astype(BF16)

    assert PROJECT_ROWS == ATTEND_ROWS
    q, k, v, ga, qi, ik, aux, zc, kpad, kb, vt, cs = _project_prompt(xp, g_pre, w_t, w_conv, PROJECT_ROWS)
    za = _attend_prompt(qi, aux, q, ga, kpad, kb, vt, ATTEND_ROWS)
    yp = _merge(xp.reshape(nb * t, D_MODEL), za.reshape(nb * t, D_ATTN), zc.reshape(nb * t, D_CONV),
                w_out, g_post, MERGE_ROWS).reshape(nb, t, D_MODEL)

    sq, sk, sv, sga, sqi, sik, saux, szc, scs = _project_sample(xs, g_pre, w_t, w_conv, state)
    per_seq = lambda a: a.reshape(nseq, -1, a.shape[-1])
    kv_rows = lambda a: a.reshape(nseq, -1, HEAD_DIM)
    sza = _attend_sample(per_seq(sqi), per_seq(saux), per_seq(sq), per_seq(sga), cache_ik,
                         kv_rows(cache_k), kv_rows(cache_v), per_seq(sk), per_seq(sv), ATTEND_ROWS)
    ys = _merge(xs.reshape(nseq * seqlen, D_MODEL), sza.reshape(nseq * seqlen, D_ATTN), szc, w_out,
                g_post, MERGE_ROWS).reshape(nseq, seqlen, D_MODEL)

    heads = lambda a, lead: a.reshape(lead + (N_KV_HEADS, HEAD_DIM))
    return (yp, ys, heads(k, (nb, t)), heads(v, (nb, t)), jnp.transpose(ik, (0, 2, 1)), cs,
            heads(sk, (nseq, seqlen)), heads(sv, (nseq, seqlen)), per_seq(sik), scs)


def kernel(x_prompt, x_sample, cache_k, cache_v, cache_idx_k, state_conv, g_pre, w_in, w_conv, w_out,
           g_post):
    depth = g_pre.shape[0]
    xp, xs = x_prompt, x_sample
    outs = []
    for l in range(depth):
        res = _layer(xp, xs, cache_k[l], cache_v[l], cache_idx_k[l], state_conv[l], g_pre[l], w_in[l],
                     w_conv[l], w_out[l], g_post[l])
        xp, xs = res[0], res[1]
        outs.append(res[2:])
    stacked = [jnp.stack([o[i] for o in outs]) for i in range(8)]
    return (xp, xs) + tuple(stacked)
```

```python
import functools

import jax
import jax.numpy as jnp
from jax import lax
from jax.experimental import pallas as pl
from jax.experimental.pallas import tpu as pltpu

F32 = jnp.float32
BF16 = jnp.bfloat16
I32 = jnp.int32
I16 = jnp.int16

D_MODEL = 2048
D_ATTN = 1024
D_CONV = 1024
HEAD_DIM = 128
N_KV_HEADS = 2
KV_GROUP = 4
N_HEADS = N_KV_HEADS * KV_GROUP
KV_DIM = N_KV_HEADS * HEAD_DIM
N_IDX_HEADS = 16
IDX_DIM = 64
TOPK_MAX = 256
CHUNK = 64
CONV_WIDTH = 3
RMS_EPS = 1e-6

LANES = 128
SUBLANES = 8
I16_ROWS = 2 * SUBLANES
VMEM_LIMIT_BYTES = 60 * 1000 * 1024

OFF_Q = 0
OFF_K = OFF_Q + D_ATTN
OFF_V = OFF_K + KV_DIM
OFF_GA = OFF_V + KV_DIM
OFF_QI = OFF_GA + D_ATTN
OFF_KW = OFF_QI + N_IDX_HEADS * IDX_DIM
OFF_B = OFF_KW + IDX_DIM + N_IDX_HEADS
OFF_C = OFF_B + D_CONV
OFF_HC = OFF_C + D_CONV
OFF_GB = OFF_HC + D_CONV
D_PROJ = OFF_GB + D_CONV
assert all(o % I16_ROWS == 0 for o in (OFF_K, OFF_V, OFF_GA, OFF_QI, OFF_KW, OFF_B, OFF_C, OFF_HC, OFF_GB))

CONV_COLS = 256
PAD_ROWS = SUBLANES
HEADS_PER_QUAD = 4
QUAD_LANES = HEADS_PER_QUAD * IDX_DIM
N_QUADS = N_IDX_HEADS // HEADS_PER_QUAD
IDX_SCALE = (IDX_DIM ** -0.5) * (N_IDX_HEADS ** -0.5)
LOG2E = 1.4426950408889634
ATT_SCALE_LOG2 = HEAD_DIM ** -0.5 * LOG2E
INT_MAX = 2 ** 31 - 1
HALF_BITS = 16
HALF_MASK = 2 ** HALF_BITS - 1
HALF_MIN = -(2 ** (HALF_BITS - 1))
HALF_MAX = 2 ** (HALF_BITS - 1) - 1
KEY_POS_INF = 0x7F800000
KEY_NEG_INF = -KEY_POS_INF - 1
MAX_SEARCH_STEPS = 36
F32_LOWEST = -3.4028234663852886e38
COUNT_UNROLL = 4
ATT_GROUP = 4
IDX_UNROLL = 4
SAMPLE_STREAMS = 2
NEG_INF = float("-inf")


def _silu(x):
    return x * jax.nn.sigmoid(x)


def _dot_nt(a, b):
    return lax.dot_general(a, b, (((1,), (1,)), ((), ())), preferred_element_type=F32)


def _rmsnorm(x, g):
    ms = jnp.mean(x * x, axis=-1, keepdims=True)
    return x * lax.rsqrt(ms + RMS_EPS) * g


def _project_rows(h, w_ref, q_ref, k_ref, v_ref, ga_ref, qi_ref, ik_ref, aux_ref):
    rows = h.shape[0]

    def mm(r0, n):
        return _dot_nt(h, w_ref[r0:r0 + n, :])

    q_ref[...] = (mm(OFF_Q, D_ATTN) * ATT_SCALE_LOG2).astype(BF16)
    kk = mm(OFF_K, KV_DIM)
    vv = mm(OFF_V, KV_DIM)
    for n in range(N_KV_HEADS):
        k_ref[pl.ds(n, rows, stride=N_KV_HEADS), :] = kk[:, n * HEAD_DIM:(n + 1) * HEAD_DIM]
        v_ref[pl.ds(n, rows, stride=N_KV_HEADS), :] = vv[:, n * HEAD_DIM:(n + 1) * HEAD_DIM]
    ga_ref[...] = mm(OFF_GA, D_ATTN)
    qi_ref[...] = mm(OFF_QI, N_IDX_HEADS * IDX_DIM).astype(BF16)
    kw = mm(OFF_KW, LANES)
    aux_ref[...] = kw
    if ik_ref.shape[0] == IDX_DIM:
        ik_ref[...] = kw.T[:IDX_DIM, :]
    else:
        ik_ref[...] = kw[:, :IDX_DIM]
    return kk, vv, kw


def _conv_chunk(h, w_ref, c):
    def mm(off):
        return _dot_nt(h, w_ref[off + c:off + c + CONV_COLS, :])

    return mm(OFF_B), mm(OFF_C) * mm(OFF_HC), mm(OFF_GB)


def _conv_out(bg, gb, u, um1, um2, wc_ref, c):
    w0 = wc_ref[0:1, c:c + CONV_COLS]
    w1 = wc_ref[1:2, c:c + CONV_COLS]
    w2 = wc_ref[2:3, c:c + CONV_COLS]
    conv = w0 * um2 + w1 * um1 + w2 * u
    return (bg * conv * _silu(gb)).astype(BF16)


def _project_prompt_kernel(x_ref, g_ref, w_ref, wc_ref, q_ref, k_ref, v_ref, ga_ref, qi_ref, ik_ref,
                           aux_ref, zc_ref, kpad_ref, kb_ref, vt_ref, cs_ref, upad_ref, *, tm):
    @pl.when(pl.program_id(1) == 0)
    def _():
        upad_ref[0:PAD_ROWS, :] = jnp.zeros((PAD_ROWS, D_CONV), F32)

    h = _rmsnorm(x_ref[...], g_ref[...]).astype(BF16)
    kk, vv, kw = _project_rows(h, w_ref, q_ref, k_ref, v_ref, ga_ref, qi_ref, ik_ref, aux_ref)
    kb_ref[...] = kk.astype(BF16)
    vt_ref[...] = vv.T.astype(BF16)
    lane = lax.broadcasted_iota(I32, kw.shape, 1)
    lo = jnp.where(lane < IDX_DIM, kw, 0.0)
    hi = pltpu.roll(lo, IDX_DIM, axis=1)
    zero = jnp.zeros_like(lo)
    for i, blk in enumerate((lo, zero, hi, zero, zero, lo, zero, hi)):
        kpad_ref[:, i * LANES:(i + 1) * LANES] = blk.astype(BF16)

    for c in range(0, D_CONV, CONV_COLS):
        bg, u, gb = _conv_chunk(h, w_ref, c)
        upad_ref[PAD_ROWS:PAD_ROWS + tm, c:c + CONV_COLS] = u
        um1 = upad_ref[PAD_ROWS - 1:PAD_ROWS - 1 + tm, c:c + CONV_COLS]
        um2 = upad_ref[PAD_ROWS - 2:PAD_ROWS - 2 + tm, c:c + CONV_COLS]
        zc_ref[:, c:c + CONV_COLS] = _conv_out(bg, gb, u, um1, um2, wc_ref, c)
    last = upad_ref[PAD_ROWS + tm - (CONV_WIDTH - 1):PAD_ROWS + tm, :]
    cs_ref[...] = last
    upad_ref[PAD_ROWS - (CONV_WIDTH - 1):PAD_ROWS, :] = last


def _project_sample_kernel(x_ref, g_ref, w_ref, wc_ref, st_ref, q_ref, k_ref, v_ref, ga_ref, qi_ref,
                           ik_ref, aux_ref, zc_ref, cs_ref, upad_ref, *, nseq, seqlen):
    rows = nseq * seqlen
    upad_ref[:, PAD_ROWS - (CONV_WIDTH - 1):PAD_ROWS, :] = st_ref[...]
    h = _rmsnorm(x_ref[...], g_ref[...]).astype(BF16)
    _project_rows(h, w_ref, q_ref, k_ref, v_ref, ga_ref, qi_ref, ik_ref, aux_ref)
    for c in range(0, D_CONV, CONV_COLS):
        bg, u, gb = _conv_chunk(h, w_ref, c)
        upad_ref[:, PAD_ROWS:PAD_ROWS + seqlen, c:c + CONV_COLS] = u.reshape(nseq, seqlen, CONV_COLS)
        um1 = upad_ref[:, PAD_ROWS - 1:PAD_ROWS - 1 + seqlen, c:c + CONV_COLS].reshape(rows, CONV_COLS)
        um2 = upad_ref[:, PAD_ROWS - 2:PAD_ROWS - 2 + seqlen, c:c + CONV_COLS].reshape(rows, CONV_COLS)
        zc_ref[:, c:c + CONV_COLS] = _conv_out(bg, gb, u, um1, um2, wc_ref, c)
    cs_ref[...] = upad_ref[:, PAD_ROWS + seqlen - (CONV_WIDTH - 1):PAD_ROWS + seqlen, :]


_PROJECT_OUTS = ((1, D_ATTN, BF16), (N_KV_HEADS, HEAD_DIM, F32), (N_KV_HEADS, HEAD_DIM, F32),
                 (1, D_ATTN, F32), (1, N_IDX_HEADS * IDX_DIM, BF16), (1, IDX_DIM, F32), (1, LANES, F32),
                 (1, D_CONV, BF16))
_IK_OUT = 5


def _project_out_shapes(lead, rows):
    return [jax.ShapeDtypeStruct(lead + (m * rows, n), dt) for m, n, dt in _PROJECT_OUTS]


def _resident(shape, ngrid):
    zeros = (0,) * len(shape)
    if ngrid == 1:
        return pl.BlockSpec(shape, lambda i: zeros, pipeline_mode=pl.Buffered(1))
    return pl.BlockSpec(shape, lambda b, i: zeros, pipeline_mode=pl.Buffered(1))


def _project_prompt(x, g_pre, w_t, w_conv, tm):
    nb, t, _ = x.shape
    grid = (nb, t // tm)
    row_spec = lambda n, m=1: pl.BlockSpec((None, m * tm, n), lambda b, i: (b, i, 0))
    out_shapes = _project_out_shapes((nb,), t) + [
        jax.ShapeDtypeStruct((nb, t, N_QUADS * QUAD_LANES), BF16),
        jax.ShapeDtypeStruct((nb, t, KV_DIM), BF16),
        jax.ShapeDtypeStruct((nb, t // tm, KV_DIM, tm), BF16),
        jax.ShapeDtypeStruct((nb, CONV_WIDTH - 1, D_CONV), F32)]
    out_specs = [row_spec(n, m) for m, n, _ in _PROJECT_OUTS]
    out_shapes[_IK_OUT] = jax.ShapeDtypeStruct((nb, IDX_DIM, t), F32)
    out_specs[_IK_OUT] = pl.BlockSpec((None, IDX_DIM, tm), lambda b, i: (b, 0, i))
    out_specs += [row_spec(N_QUADS * QUAD_LANES), row_spec(KV_DIM),
                  pl.BlockSpec((None, None, KV_DIM, tm), lambda b, i: (b, i, 0, 0)),
                  pl.BlockSpec((None, CONV_WIDTH - 1, D_CONV), lambda b, i: (b, 0, 0))]
    return pl.pallas_call(
        functools.partial(_project_prompt_kernel, tm=tm),
        grid=grid,
        in_specs=[row_spec(D_MODEL), _resident((1, D_MODEL), 2), _resident((D_PROJ, D_MODEL), 2),
                  _resident((CONV_WIDTH, D_CONV), 2)],
        out_specs=out_specs,
        out_shape=out_shapes,
        scratch_shapes=[pltpu.VMEM((PAD_ROWS + tm, D_CONV), F32)],
        compiler_params=pltpu.CompilerParams(dimension_semantics=("arbitrary", "arbitrary"),
                                             vmem_limit_bytes=VMEM_LIMIT_BYTES),
        name="project_prompt",
    )(x, g_pre, w_t, w_conv)


def _project_sample(x, g_pre, w_t, w_conv, state):
    nseq, seqlen, _ = x.shape
    rows = nseq * seqlen
    full = lambda shape: pl.BlockSpec(shape, lambda i: (0,) * len(shape))
    out_shapes = _project_out_shapes((), rows) + [jax.ShapeDtypeStruct((nseq, CONV_WIDTH - 1, D_CONV), F32)]
    out_specs = [full((m * rows, n)) for m, n, _ in _PROJECT_OUTS] + [full((nseq, CONV_WIDTH - 1, D_CONV))]
    return pl.pallas_call(
        functools.partial(_project_sample_kernel, nseq=nseq, seqlen=seqlen),
        grid=(1,),
        in_specs=[full((rows, D_MODEL)), full((1, D_MODEL)), _resident((D_PROJ, D_MODEL), 1),
                  full((CONV_WIDTH, D_CONV)), full((nseq, CONV_WIDTH - 1, D_CONV))],
        out_specs=out_specs,
        out_shape=out_shapes,
        scratch_shapes=[pltpu.VMEM((nseq, PAD_ROWS + seqlen, D_CONV), F32)],
        compiler_params=pltpu.CompilerParams(dimension_semantics=("arbitrary",),
                                             vmem_limit_bytes=VMEM_LIMIT_BYTES),
        name="project_sample",
    )(x.reshape(rows, D_MODEL), g_pre, w_t, w_conv, state)


def _key_of_score(x):
    bits = lax.bitcast_convert_type(x, I32)
    return bits ^ ((bits >> 31) & INT_MAX)


def _score_of_key(key):
    return lax.bitcast_convert_type(key ^ ((key >> 31) & INT_MAX), F32)


def _store_scores(score, admissible, sc_scr, hi_scr, lo_scr, kt):
    score = jnp.where(admissible, score, NEG_INF)
    sc_scr[kt] = score
    key = _key_of_score(score)
    hi_scr[kt] = (key >> HALF_BITS).astype(I16)
    lo_scr[kt] = key.astype(I16) ^ jnp.int16(HALF_MIN)


def _vreg_sum(p, rows):
    parts = [p[r:r + rows, :] for r in range(0, p.shape[0], rows)]
    while len(parts) > 1:
        parts = [a + b for a, b in zip(parts[::2], parts[1::2])] + parts[len(parts) & ~1:]
    return parts[0]


def _threshold_key_guess(for_tiles, hi_scr, lo_scr, width):
    def count_half(ref, th):
        def body(kt, c16):
            return c16 + _vreg_sum(jnp.where(ref[kt] >= th, jnp.int16(1), jnp.int16(0)), I16_ROWS)

        c16 = for_tiles(body, jnp.zeros((I16_ROWS, width), I16), COUNT_UNROLL)
        return jnp.sum(c16.astype(I32), axis=0, keepdims=True)

    def bisect_half(ref):
        def step(_, carry):
            lo, hi = carry
            mid = (lo + hi + 1) >> 1
            cnt = count_half(ref, jnp.minimum(mid, HALF_MAX).astype(I16))
            ge = (cnt >= TOPK_MAX) & (mid <= HALF_MAX)
            return jnp.where(ge, mid, lo), jnp.where(ge, hi, mid)

        lo0 = jnp.full((1, width), HALF_MIN, I32)
        hi0 = jnp.full((1, width), HALF_MAX + 1, I32)
        return lax.fori_loop(0, HALF_BITS, step, (lo0, hi0))[0]

    top = bisect_half(hi_scr)
    top16 = top.astype(I16)

    def narrow(kt, carry):
        h = hi_scr[kt]
        lo_scr[kt] = jnp.where(h == top16, lo_scr[kt],
                               jnp.where(h > top16, jnp.int16(HALF_MAX), jnp.int16(HALF_MIN)))
        return carry

    for_tiles(narrow, 0)
    bot = bisect_half(lo_scr)
    return top * (HALF_MASK + 1) + (bot - HALF_MIN)


def _topk_threshold(for_tiles, sc_scr, hi_scr, lo_scr, thr_scr, *, tk, width, n_cols):
    def count(pred_fn):
        def body(kt, c8):
            return c8 + _vreg_sum(pred_fn(kt, sc_scr[kt]).astype(I32), SUBLANES)

        c8 = for_tiles(body, jnp.zeros((SUBLANES, width), I32), COUNT_UNROLL)
        return jnp.sum(c8, axis=0, keepdims=True)

    hint_lo = _threshold_key_guess(for_tiles, hi_scr, lo_scr, width)
    hint_hi = hint_lo + 1

    def open_lanes(lo, hi):
        return jnp.max((hi > lo + 1).astype(F32)) > 0.0

    def cond(state):
        i, lo, hi, _ = state
        return (i < 2) | ((i < MAX_SEARCH_STEPS) & open_lanes(lo, hi))

    def step(state):
        i, lo, hi, n_lo = state
        mid = (lo >> 1) + (hi >> 1) + ((lo | hi) & 1)
        cand = jnp.where(i == 0, hint_lo, jnp.where(i == 1, hint_hi, mid))
        cand = jnp.minimum(jnp.maximum(cand, lo + 1), hi)
        cand_score = _score_of_key(cand)
        cnt = count(lambda kt, t: t >= cand_score)
        ge = cnt >= TOPK_MAX
        hit = cnt == TOPK_MAX
        lo_new = jnp.where(ge, cand, lo)
        hi_new = jnp.where(hit, cand + 1, jnp.where(ge, hi, cand))
        return i + 1, lo_new, hi_new, jnp.where(ge, cnt, n_lo)

    lo0 = jnp.full((1, width), KEY_NEG_INF, I32)
    hi0 = jnp.full((1, width), KEY_POS_INF + 1, I32)
    n0 = jnp.full((1, width), INT_MAX, I32)
    _, lo, _, n_lo = lax.while_loop(cond, step, (jnp.int32(0), lo0, hi0, n0))
    thr = _score_of_key(lo)
    thr_scr[...] = jnp.broadcast_to(thr, (SUBLANES, width))

    @pl.when(jnp.max((n_lo > TOPK_MAX).astype(F32)) > 0.0)
    def _():
        need = TOPK_MAX - count(lambda kt, t: t > thr)

        def pos_of(kt):
            return kt * tk + lax.broadcasted_iota(I32, (tk, width), 0)

        def pos_bisect(_, carry):
            plo, phi = carry
            mid = (plo + phi) >> 1
            ok = count(lambda kt, t: (t == thr) & (pos_of(kt) <= mid)) >= need
            return jnp.where(ok, plo, mid), jnp.where(ok, mid, phi)

        plo0 = jnp.full((1, width), -1, I32)
        phi0 = jnp.full((1, width), n_cols - 1, I32)
        steps = max(1, (n_cols - 1).bit_length()) + 1
        _, pos = lax.fori_loop(0, steps, pos_bisect, (plo0, phi0))

        def drop(kt, carry):
            t = sc_scr[kt]
            sc_scr[kt] = jnp.where((t == thr) & (pos_of(kt) > pos), NEG_INF, t)
            return carry

        for_tiles(drop, 0)


def _softmax_step(lg, m_ref, l_ref, acc_ref, pv_fn):
    m_prev = m_ref[0:1, :]
    m_new = jnp.maximum(m_prev, jnp.max(lg, axis=0, keepdims=True))
    m_safe = jnp.where(m_new == NEG_INF, 0.0, m_new)
    alpha = jnp.exp2(m_prev - m_safe)
    p = jnp.exp2(lg - m_safe)
    l_new = alpha * l_ref[0:1, :] + jnp.sum(p, axis=0, keepdims=True)
    acc_ref[...] = acc_ref[...] * alpha + pv_fn(p.astype(BF16))
    m_ref[...] = jnp.broadcast_to(m_new, m_ref.shape)
    l_ref[...] = jnp.broadcast_to(l_new, l_ref.shape)


def _attend_prompt_kernel(qi_ref, aux_ref, q_ref, ga_ref, kpad_ref, k_ref, vt_ref, z_ref, sc_scr,
                          hi_scr, lo_scr, thr_scr, m_scr, l_scr, acc_scr, lg_scr, *, tq, tk, n_cols):
    j = pl.program_id(1)
    n_tiles = j + 1

    def for_tiles(fn, init, unroll=2, halve=False):
        shift = unroll.bit_length() - 1
        assert unroll == 1 << shift
        done, carry = 0, init
        for size in ([unroll >> s for s in range(shift + 1)] if halve else [unroll, 1]):
            n_groups = (n_tiles - done) // size

            def body(g, carry, size=size, done=done):
                for u in range(size):
                    carry = fn(done + g * size + u, carry)
                return carry

            carry = lax.fori_loop(0, n_groups, body, carry)
            done = done + n_groups * size
        return carry

    assert tq == tk == TOPK_MAX and tq % CHUNK == 0
    qi = qi_ref[...]
    qstack = jnp.concatenate([qi[:, u * QUAD_LANES:(u + 1) * QUAD_LANES] for u in range(N_QUADS)], axis=0)
    w_t = aux_ref[...].T
    w_rows = [w_t[IDX_DIM + h:IDX_DIM + h + 1, :] for h in range(N_IDX_HEADS)]
    key_chunk = lax.broadcasted_iota(I32, (tk, tq), 0) // CHUNK
    qry_chunk = lax.broadcasted_iota(I32, (tk, tq), 1) // CHUNK
    diag_adm = key_chunk <= qry_chunk

    def idx_tile(kt, carry):
        start = pl.multiple_of(kt * tk, tk)
        acc = jnp.zeros((tk, tq), F32)
        for c in range(HEADS_PER_QUAD):
            kp = kpad_ref[pl.ds(start, tk), c * QUAD_LANES:(c + 1) * QUAD_LANES]
            s = _dot_nt(kp, qstack)
            for u in range(N_QUADS):
                acc = acc + jnp.maximum(s[:, u * tq:(u + 1) * tq], 0.0) * w_rows[HEADS_PER_QUAD * u + c]
        _store_scores(acc * IDX_SCALE, (diag_adm & (kt == j)) | (kt < j), sc_scr, hi_scr, lo_scr, kt)
        return carry

    for_tiles(idx_tile, 0, IDX_UNROLL, halve=True)

    thr_scr[...] = jnp.full((SUBLANES, tq), F32_LOWEST, F32)

    @pl.when(j >= 1)
    def _():
        _topk_threshold(for_tiles, sc_scr, hi_scr, lo_scr, thr_scr, tk=tk, width=tq, n_cols=n_cols)

    thr = thr_scr[0:1, :]

    q = q_ref[...]
    qn = [jnp.concatenate([q[:, (KV_GROUP * n + g) * HEAD_DIM:(KV_GROUP * n + g + 1) * HEAD_DIM]
                           for g in range(KV_GROUP)], axis=0) for n in range(N_KV_HEADS)]
    m_scr[...] = jnp.full(m_scr.shape, NEG_INF, F32)
    l_scr[...] = jnp.zeros(l_scr.shape, F32)
    acc_scr[...] = jnp.zeros(acc_scr.shape, F32)

    def att_tiles(kts):
        for slot, kt in enumerate(kts):
            start = pl.multiple_of(kt * tk, tk)
            for n in range(N_KV_HEADS):
                lg_scr[slot, n] = _dot_nt(k_ref[pl.ds(start, tk), n * HEAD_DIM:(n + 1) * HEAD_DIM], qn[n])
        for slot, kt in enumerate(kts):
            sel = sc_scr[kt] >= thr
            for n in range(N_KV_HEADS):
                lg = jnp.concatenate([jnp.where(sel, lg_scr[slot, n, :, g * tq:(g + 1) * tq], NEG_INF)
                                      for g in range(KV_GROUP)], axis=1)
                vt_n = vt_ref[kt, n * HEAD_DIM:(n + 1) * HEAD_DIM, :]
                _softmax_step(lg, m_scr.at[n], l_scr.at[n], acc_scr.at[n],
                              lambda p: jnp.dot(vt_n, p, preferred_element_type=F32))

    def att_group(size):
        def body(i, carry):
            att_tiles(tuple(i + u for u in range(size)))
            return carry

        return body

    done = 0
    size = ATT_GROUP
    while size >= 1:
        n_groups = (n_tiles - done) // size
        lax.fori_loop(0, n_groups, lambda g, c, size=size, done=done: att_group(size)(done + g * size, c), 0)
        done = done + n_groups * size
        size //= 2

    outs = []
    for n in range(N_KV_HEADS):
        o = acc_scr[n] / l_scr[n][0:1, :]
        outs.extend(o[:, g * tq:(g + 1) * tq].T for g in range(KV_GROUP))
    z_ref[...] = (jnp.concatenate(outs, axis=1) * _silu(ga_ref[...])).astype(BF16)


def _attend_prompt(qi, aux, q, ga, kpad, kb, vt, tq):
    nb, t, _ = q.shape
    tk = tq
    n_tiles = t // tk
    lanes_q = KV_GROUP * tq
    blk = lambda n: pl.BlockSpec((None, tq, n), lambda b, j: (b, j, 0))
    seq = lambda n: pl.BlockSpec((None, t, n), lambda b, j: (b, 0, 0))
    scratch = [pltpu.VMEM((n_tiles, tk, tq), F32), pltpu.VMEM((n_tiles, tk, tq), I16),
               pltpu.VMEM((n_tiles, tk, tq), I16), pltpu.VMEM((SUBLANES, tq), F32),
               pltpu.VMEM((N_KV_HEADS, SUBLANES, lanes_q), F32),
               pltpu.VMEM((N_KV_HEADS, SUBLANES, lanes_q), F32),
               pltpu.VMEM((N_KV_HEADS, HEAD_DIM, lanes_q), F32),
               pltpu.VMEM((ATT_GROUP, N_KV_HEADS, tk, lanes_q), F32)]
    return pl.pallas_call(
        functools.partial(_attend_prompt_kernel, tq=tq, tk=tk, n_cols=t),
        grid=(nb, t // tq),
        in_specs=[blk(N_IDX_HEADS * IDX_DIM), blk(LANES), blk(D_ATTN), blk(D_ATTN),
                  seq(N_QUADS * QUAD_LANES), seq(KV_DIM),
                  pl.BlockSpec((None, n_tiles, KV_DIM, tk), lambda b, j: (b, 0, 0, 0))],
        out_specs=blk(D_ATTN),
        out_shape=jax.ShapeDtypeStruct((nb, t, D_ATTN), BF16),
        scratch_shapes=scratch,
        compiler_params=pltpu.CompilerParams(dimension_semantics=("arbitrary", "arbitrary"),
                                             vmem_limit_bytes=VMEM_LIMIT_BYTES),
        name="attend_prompt",
    )(qi, aux, q, ga, kpad, kb, vt)


def _attend_sample_kernel(qi_ref, aux_ref, q_ref, ga_ref, cikt_ref, ck_ref, cv_ref, nk_ref, nv_ref,
                          z_ref, kit_s, k_s, vt_s, sc_scr, hi_scr, lo_scr, thr_scr, m_scr, l_scr, acc_scr,
                          *, tq, tk, past, n_tiles):
    n_keys = past + tq
    hl = N_IDX_HEADS * tq
    streams = range(SAMPLE_STREAMS)
    width = SAMPLE_STREAMS * LANES
    assert N_HEADS * tq == LANES and hl == 2 * LANES and past % tk == 0 and tq <= tk
    assert n_keys > TOPK_MAX and past % CHUNK == 0 and tq <= CHUNK

    def for_tiles(fn, init, unroll=None):
        for kt in range(n_tiles):
            init = fn(kt, init)
        return init

    for s in streams:
        lanes = slice(s * LANES, (s + 1) * LANES)
        aux = aux_ref[s]
        kit_s[s, :, 0:past] = cikt_ref[s].astype(BF16)
        kit_s[s, :, past:] = jnp.zeros((IDX_DIM, n_tiles * tk - past), BF16)
        kit_s[s, :, past:past + tq] = aux.T[:IDX_DIM, :].astype(BF16)
        k_s[s, past + tq:, :] = jnp.zeros((n_tiles * tk - n_keys, KV_DIM), BF16)
        vt_s[s, n_tiles - 1] = jnp.zeros((KV_DIM, tk), BF16)
        for n in range(N_KV_HEADS):
            cols = slice(n * HEAD_DIM, (n + 1) * HEAD_DIM)
            k_s[s, 0:past, cols] = ck_ref[s, pl.ds(n, past, stride=N_KV_HEADS), :].astype(BF16)
            k_s[s, past:past + tq, cols] = nk_ref[s, pl.ds(n, tq, stride=N_KV_HEADS), :].astype(BF16)
            for kt in range(past // tk):
                v_tile = cv_ref[s, pl.ds(N_KV_HEADS * kt * tk + n, tk, stride=N_KV_HEADS), :]
                vt_s[s, kt, cols, :] = v_tile.T.astype(BF16)
            vt_s[s, n_tiles - 1, cols, 0:tq] = nv_ref[s, pl.ds(n, tq, stride=N_KV_HEADS), :].T.astype(BF16)

        qi = qi_ref[s]
        qrows = jnp.concatenate([qi[:, h * IDX_DIM:(h + 1) * IDX_DIM] for h in range(N_IDX_HEADS)], axis=0)
        w_rows = jnp.concatenate([aux[:, IDX_DIM + h:IDX_DIM + h + 1] for h in range(N_IDX_HEADS)], axis=0)
        w_rows = jnp.broadcast_to(w_rows, (hl, LANES))

        def idx_tile(kt, carry):
            d = jnp.dot(qrows, kit_s[s, :, kt * tk:(kt + 1) * tk], preferred_element_type=F32)
            y = jnp.concatenate([jnp.maximum(d[:, c:c + LANES], 0.0) * w_rows for c in range(0, tk, LANES)],
                                axis=1)
            per_query = _vreg_sum(y, tq) * IDX_SCALE
            score = jnp.concatenate([per_query] * N_HEADS, axis=0).T
            pos = kt * tk + lax.broadcasted_iota(I32, (tk, LANES), 0)
            _store_scores(score, pos < n_keys, sc_scr.at[:, :, lanes], hi_scr.at[:, :, lanes],
                          lo_scr.at[:, :, lanes], kt)
            return carry

        for_tiles(idx_tile, 0)

    _topk_threshold(for_tiles, sc_scr, hi_scr, lo_scr, thr_scr, tk=tk, width=width, n_cols=n_tiles * tk)
    thr = thr_scr[0:1, :]

    zeros = jnp.zeros((tq, HEAD_DIM), BF16)
    m_scr[...] = jnp.full(m_scr.shape, NEG_INF, F32)
    l_scr[...] = jnp.zeros(l_scr.shape, F32)
    acc_scr[...] = jnp.zeros(acc_scr.shape, F32)
    for s in streams:
        lanes = slice(s * LANES, (s + 1) * LANES)
        q = q_ref[s]
        qblk = jnp.concatenate(
            [jnp.concatenate([q[:, (KV_GROUP * n + g) * HEAD_DIM:(KV_GROUP * n + g + 1) * HEAD_DIM]
                              if m == n else zeros for m in range(N_KV_HEADS)], axis=1)
             for n in range(N_KV_HEADS) for g in range(KV_GROUP)], axis=0)

        def att_tile(kt, carry):
            lg = jnp.where(sc_scr[kt, :, lanes] >= thr[:, lanes],
                           _dot_nt(k_s[s, kt * tk:(kt + 1) * tk, :], qblk), NEG_INF)
            _softmax_step(lg, m_scr.at[:, lanes], l_scr.at[:, lanes], acc_scr.at[:, lanes],
                          lambda p: jnp.dot(vt_s[s, kt], p, preferred_element_type=F32))
            return carry

        for_tiles(att_tile, 0)

        o_t = (acc_scr[:, lanes] / l_scr[0:1, lanes]).T
        outs = [o_t[(KV_GROUP * n + g) * tq:(KV_GROUP * n + g + 1) * tq, n * HEAD_DIM:(n + 1) * HEAD_DIM]
                for n in range(N_KV_HEADS) for g in range(KV_GROUP)]
        z_ref[s] = (jnp.concatenate(outs, axis=1) * _silu(ga_ref[s])).astype(BF16)


def _attend_sample(qi, aux, q, ga, cache_ik, cache_k, cache_v, new_k, new_v, tk):
    nseq, tq, _ = q.shape
    past = cache_ik.shape[1]
    assert nseq % SAMPLE_STREAMS == 0
    cache_ikt = jnp.transpose(cache_ik, (0, 2, 1))
    n_tiles = -(-(past + tq) // tk)
    width = SAMPLE_STREAMS * LANES
    blk = lambda n, m=1: pl.BlockSpec((SAMPLE_STREAMS, m * tq, n), lambda b: (b, 0, 0))
    cache = lambda n, m=1: pl.BlockSpec((SAMPLE_STREAMS, m * past, n), lambda b: (b, 0, 0))
    scratch = [pltpu.VMEM((SAMPLE_STREAMS, IDX_DIM, n_tiles * tk), BF16),
               pltpu.VMEM((SAMPLE_STREAMS, n_tiles * tk, KV_DIM), BF16),
               pltpu.VMEM((SAMPLE_STREAMS, n_tiles, KV_DIM, tk), BF16),
               pltpu.VMEM((n_tiles, tk, width), F32), pltpu.VMEM((n_tiles, tk, width), I16),
               pltpu.VMEM((n_tiles, tk, width), I16),
               pltpu.VMEM((SUBLANES, width), F32),
               pltpu.VMEM((SUBLANES, width), F32), pltpu.VMEM((SUBLANES, width), F32),
               pltpu.VMEM((KV_DIM, width), F32)]
    return pl.pallas_call(
        functools.partial(_attend_sample_kernel, tq=tq, tk=tk, past=past, n_tiles=n_tiles),
        grid=(nseq // SAMPLE_STREAMS,),
        in_specs=[blk(N_IDX_HEADS * IDX_DIM), blk(LANES), blk(D_ATTN), blk(D_ATTN),
                  pl.BlockSpec((SAMPLE_STREAMS, IDX_DIM, past), lambda b: (b, 0, 0)),
                  cache(HEAD_DIM, N_KV_HEADS), cache(HEAD_DIM, N_KV_HEADS), blk(HEAD_DIM, N_KV_HEADS),
                  blk(HEAD_DIM, N_KV_HEADS)],
        out_specs=blk(D_ATTN),
        out_shape=jax.ShapeDtypeStruct((nseq, tq, D_ATTN), BF16),
        scratch_shapes=scratch,
        compiler_params=pltpu.CompilerParams(dimension_semantics=("arbitrary",),
                                             vmem_limit_bytes=VMEM_LIMIT_BYTES),
        name="attend_sample",
    )(qi, aux, q, ga, cache_ikt, cache_k, cache_v, new_k, new_v)


def _merge_kernel(x_ref, za_ref, zc_ref, w_ref, g_ref, y_ref):
    z = jnp.concatenate([za_ref[...], zc_ref[...]], axis=1)
    y = jnp.dot(z, w_ref[...], preferred_element_type=F32)
    y_ref[...] = x_ref[...] + _rmsnorm(y, g_ref[...])


def _merge(x, za, zc, w_out, g_post, tm):
    rows = x.shape[0]
    row_spec = lambda n: pl.BlockSpec((tm, n), lambda i: (i, 0))
    return pl.pallas_call(
        _merge_kernel,
        grid=(rows // tm,),
        in_specs=[row_spec(D_MODEL), row_spec(D_ATTN), row_spec(D_CONV), _resident((D_MODEL, D_MODEL), 1),
                  _resident((1, D_MODEL), 1)],
        out_specs=row_spec(D_MODEL),
        out_shape=jax.ShapeDtypeStruct((rows, D_MODEL), F32),
        compiler_params=pltpu.CompilerParams(dimension_semantics=("arbitrary",),
                                             vmem_limit_bytes=VMEM_LIMIT_BYTES),
        name="merge",
    )(x, za, zc, w_out, g_post)


PROJECT_ROWS = 256
ATTEND_ROWS = 256
MERGE_ROWS = 512


def _layer(xp, xs, cache_k, cache_v, cache_ik, state, g_pre, w_in, w_conv, w_out, g_post):
    nb, t, _ = xp.shape
    nseq, seqlen, _ = xs.shape
    g_pre = g_pre.reshape(1, D_MODEL)
    g_post = g_post.reshape(1, D_MODEL)
    w_t = w_in.T.astype(BF16)
    w_out = w_out.astype(BF16)

    assert PROJECT_ROWS == ATTEND_ROWS
    q, k, v, ga, qi, ik, aux, zc, kpad, kb, vt, cs = _project_prompt(xp, g_pre, w_t, w_conv, PROJECT_ROWS)
    za = _attend_prompt(qi, aux, q, ga, kpad, kb, vt, ATTEND_ROWS)
    yp = _merge(xp.reshape(nb * t, D_MODEL), za.reshape(nb * t, D_ATTN), zc.reshape(nb * t, D_CONV),
                w_out, g_post, MERGE_ROWS).reshape(nb, t, D_MODEL)

    sq, sk, sv, sga, sqi, sik, saux, szc, scs = _project_sample(xs, g_pre, w_t, w_conv, state)
    per_seq = lambda a: a.reshape(nseq, -1, a.shape[-1])
    kv_rows = lambda a: a.reshape(nseq, -1, HEAD_DIM)
    sza = _attend_sample(per_seq(sqi), per_seq(saux), per_seq(sq), per_seq(sga), cache_ik,
                         kv_rows(cache_k), kv_rows(cache_v), per_seq(sk), per_seq(sv), ATTEND_ROWS)
    ys = _merge(xs.reshape(nseq * seqlen, D_MODEL), sza.reshape(nseq * seqlen, D_ATTN), szc, w_out,
                g_post, MERGE_ROWS).reshape(nseq, seqlen, D_MODEL)

    heads = lambda a, lead: a.reshape(lead + (N_KV_HEADS, HEAD_DIM))
    return (yp, ys, heads(k, (nb, t)), heads(v, (nb, t)), jnp.transpose(ik, (0, 2, 1)), cs,
            heads(sk, (nseq, seqlen)), heads(sv, (nseq, seqlen)), per_seq(sik), scs)


def kernel(x_prompt, x_sample, cache_k, cache_v, cache_idx_k, state_conv, g_pre, w_in, w_conv, w_out,
           g_post):
    depth = g_pre.shape[0]
    xp, xs = x_prompt, x_sample
    outs = []
    for l in range(depth):
        res = _layer(xp, xs, cache_k[l], cache_v[l], cache_idx_k[l], state_conv[l], g_pre[l], w_in[l],
                     w_conv[l], w_out[l], g_post[l])
        xp, xs = res[0], res[1]
        outs.append(res[2:])
    stacked = [jnp.stack([o[i] for o in outs]) for i in range(8)]
    return (xp, xs) + tuple(stacked)
```

```python
import functools

import jax
import jax.numpy as jnp
from jax import lax
from jax.experimental import pallas as pl
from jax.experimental.pallas import tpu as pltpu

F32 = jnp.float32
BF16 = jnp.bfloat16
I32 = jnp.int32
I16 = jnp.int16

D_MODEL = 2048
D_ATTN = 1024
D_CONV = 1024
HEAD_DIM = 128
N_KV_HEADS = 2
KV_GROUP = 4
N_HEADS = N_KV_HEADS * KV_GROUP
KV_DIM = N_KV_HEADS * HEAD_DIM
N_IDX_HEADS = 16
IDX_DIM = 64
TOPK_MAX = 256
CHUNK = 64
CONV_WIDTH = 3
RMS_EPS = 1e-6

LANES = 128
SUBLANES = 8
I16_ROWS = 2 * SUBLANES
VMEM_LIMIT_BYTES = 60 * 1000 * 1024

OFF_Q = 0
OFF_K = OFF_Q + D_ATTN
OFF_V = OFF_K + KV_DIM
OFF_GA = OFF_V + KV_DIM
OFF_QI = OFF_GA + D_ATTN
OFF_KW = OFF_QI + N_IDX_HEADS * IDX_DIM
OFF_B = OFF_KW + IDX_DIM + N_IDX_HEADS
OFF_C = OFF_B + D_CONV
OFF_HC = OFF_C + D_CONV
OFF_GB = OFF_HC + D_CONV
D_PROJ = OFF_GB + D_CONV
assert all(o % I16_ROWS == 0 for o in (OFF_K, OFF_V, OFF_GA, OFF_QI, OFF_KW, OFF_B, OFF_C, OFF_HC, OFF_GB))

CONV_COLS = 256
PAD_ROWS = SUBLANES
HEADS_PER_QUAD = 4
QUAD_LANES = HEADS_PER_QUAD * IDX_DIM
N_QUADS = N_IDX_HEADS // HEADS_PER_QUAD
IDX_SCALE = (IDX_DIM ** -0.5) * (N_IDX_HEADS ** -0.5)
LOG2E = 1.4426950408889634
ATT_SCALE_LOG2 = HEAD_DIM ** -0.5 * LOG2E
INT_MAX = 2 ** 31 - 1
HALF_BITS = 16
HALF_MASK = 2 ** HALF_BITS - 1
HALF_MIN = -(2 ** (HALF_BITS - 1))
HALF_MAX = 2 ** (HALF_BITS - 1) - 1
KEY_POS_INF = 0x7F800000
KEY_NEG_INF = -KEY_POS_INF - 1
MAX_SEARCH_STEPS = 36
F32_LOWEST = -3.4028234663852886e38
COUNT_UNROLL = 4
ATT_GROUP = 4
IDX_UNROLL = 4
SAMPLE_STREAMS = 2
NEG_INF = float("-inf")


def _silu(x):
    return x * jax.nn.sigmoid(x)


def _dot_nt(a, b):
    return lax.dot_general(a, b, (((1,), (1,)), ((), ())), preferred_element_type=F32)


def _rmsnorm(x, g):
    ms = jnp.mean(x * x, axis=-1, keepdims=True)
    return x * lax.rsqrt(ms + RMS_EPS) * g


def _project_rows(h, w_ref, q_ref, k_ref, v_ref, ga_ref, qi_ref, ik_ref, aux_ref):
    rows = h.shape[0]

    def mm(r0, n):
        return _dot_nt(h, w_ref[r0:r0 + n, :])

    q_ref[...] = (mm(OFF_Q, D_ATTN) * ATT_SCALE_LOG2).astype(BF16)
    kk = mm(OFF_K, KV_DIM)
    vv = mm(OFF_V, KV_DIM)
    for n in range(N_KV_HEADS):
        k_ref[pl.ds(n, rows, stride=N_KV_HEADS), :] = kk[:, n * HEAD_DIM:(n + 1) * HEAD_DIM]
        v_ref[pl.ds(n, rows, stride=N_KV_HEADS), :] = vv[:, n * HEAD_DIM:(n + 1) * HEAD_DIM]
    ga_ref[...] = mm(OFF_GA, D_ATTN)
    qi_ref[...] = mm(OFF_QI, N_IDX_HEADS * IDX_DIM).astype(BF16)
    kw = mm(OFF_KW, LANES)
    aux_ref[...] = kw
    if ik_ref.shape[0] == IDX_DIM:
        ik_ref[...] = kw.T[:IDX_DIM, :]
    else:
        ik_ref[...] = kw[:, :IDX_DIM]
    return kk, vv, kw


def _conv_chunk(h, w_ref, c):
    def mm(off):
        return _dot_nt(h, w_ref[off + c:off + c + CONV_COLS, :])

    return mm(OFF_B), mm(OFF_C) * mm(OFF_HC), mm(OFF_GB)


def _conv_out(bg, gb, u, um1, um2, wc_ref, c):
    w0 = wc_ref[0:1, c:c + CONV_COLS]
    w1 = wc_ref[1:2, c:c + CONV_COLS]
    w2 = wc_ref[2:3, c:c + CONV_COLS]
    conv = w0 * um2 + w1 * um1 + w2 * u
    return (bg * conv * _silu(gb)).astype(BF16)


def _project_prompt_kernel(x_ref, g_ref, w_ref, wc_ref, q_ref, k_ref, v_ref, ga_ref, qi_ref, ik_ref,
                           aux_ref, zc_ref, kpad_ref, kb_ref, vt_ref, cs_ref, upad_ref, *, tm):
    @pl.when(pl.program_id(1) == 0)
    def _():
        upad_ref[0:PAD_ROWS, :] = jnp.zeros((PAD_ROWS, D_CONV), F32)

    h = _rmsnorm(x_ref[...], g_ref[...]).astype(BF16)
    kk, vv, kw = _project_rows(h, w_ref, q_ref, k_ref, v_ref, ga_ref, qi_ref, ik_ref, aux_ref)
    kb_ref[...] = kk.astype(BF16)
    vt_ref[...] = vv.T.astype(BF16)
    lane = lax.broadcasted_iota(I32, kw.shape, 1)
    lo = jnp.where(lane < IDX_DIM, kw, 0.0)
    hi = pltpu.roll(lo, IDX_DIM, axis=1)
    zero = jnp.zeros_like(lo)
    for i, blk in enumerate((lo, zero, hi, zero, zero, lo, zero, hi)):
        kpad_ref[:, i * LANES:(i + 1) * LANES] = blk.astype(BF16)

    for c in range(0, D_CONV, CONV_COLS):
        bg, u, gb = _conv_chunk(h, w_ref, c)
        upad_ref[PAD_ROWS:PAD_ROWS + tm, c:c + CONV_COLS] = u
        um1 = upad_ref[PAD_ROWS - 1:PAD_ROWS - 1 + tm, c:c + CONV_COLS]
        um2 = upad_ref[PAD_ROWS - 2:PAD_ROWS - 2 + tm, c:c + CONV_COLS]
        zc_ref[:, c:c + CONV_COLS] = _conv_out(bg, gb, u, um1, um2, wc_ref, c)
    last = upad_ref[PAD_ROWS + tm - (CONV_WIDTH - 1):PAD_ROWS + tm, :]
    cs_ref[...] = last
    upad_ref[PAD_ROWS - (CONV_WIDTH - 1):PAD_ROWS, :] = last


def _project_sample_kernel(x_ref, g_ref, w_ref, wc_ref, st_ref, q_ref, k_ref, v_ref, ga_ref, qi_ref,
                           ik_ref, aux_ref, zc_ref, cs_ref, upad_ref, *, nseq, seqlen):
    rows = nseq * seqlen
    upad_ref[:, PAD_ROWS - (CONV_WIDTH - 1):PAD_ROWS, :] = st_ref[...]
    h = _rmsnorm(x_ref[...], g_ref[...]).astype(BF16)
    _project_rows(h, w_ref, q_ref, k_ref, v_ref, ga_ref, qi_ref, ik_ref, aux_ref)
    for c in range(0, D_CONV, CONV_COLS):
        bg, u, gb = _conv_chunk(h, w_ref, c)
        upad_ref[:, PAD_ROWS:PAD_ROWS + seqlen, c:c + CONV_COLS] = u.reshape(nseq, seqlen, CONV_COLS)
        um1 = upad_ref[:, PAD_ROWS - 1:PAD_ROWS - 1 + seqlen, c:c + CONV_COLS].reshape(rows, CONV_COLS)
        um2 = upad_ref[:, PAD_ROWS - 2:PAD_ROWS - 2 + seqlen, c:c + CONV_COLS].reshape(rows, CONV_COLS)
        zc_ref[:, c:c + CONV_COLS] = _conv_out(bg, gb, u, um1, um2, wc_ref, c)
    cs_ref[...] = upad_ref[:, PAD_ROWS + seqlen - (CONV_WIDTH - 1):PAD_ROWS + seqlen, :]


_PROJECT_OUTS = ((1, D_ATTN, BF16), (N_KV_HEADS, HEAD_DIM, F32), (N_KV_HEADS, HEAD_DIM, F32),
                 (1, D_ATTN, F32), (1, N_IDX_HEADS * IDX_DIM, BF16), (1, IDX_DIM, F32), (1, LANES, F32),
                 (1, D_CONV, BF16))
_IK_OUT = 5


def _project_out_shapes(lead, rows):
    return [jax.ShapeDtypeStruct(lead + (m * rows, n), dt) for m, n, dt in _PROJECT_OUTS]


def _resident(shape, ngrid):
    zeros = (0,) * len(shape)
    if ngrid == 1:
        return pl.BlockSpec(shape, lambda i: zeros, pipeline_mode=pl.Buffered(1))
    return pl.BlockSpec(shape, lambda b, i: zeros, pipeline_mode=pl.Buffered(1))


def _project_prompt(x, g_pre, w_t, w_conv, tm):
    nb, t, _ = x.shape
    grid = (nb, t // tm)
    row_spec = lambda n, m=1: pl.BlockSpec((None, m * tm, n), lambda b, i: (b, i, 0))
    out_shapes = _project_out_shapes((nb,), t) + [
        jax.ShapeDtypeStruct((nb, t, N_QUADS * QUAD_LANES), BF16),
        jax.ShapeDtypeStruct((nb, t, KV_DIM), BF16),
        jax.ShapeDtypeStruct((nb, t // tm, KV_DIM, tm), BF16),
        jax.ShapeDtypeStruct((nb, CONV_WIDTH - 1, D_CONV), F32)]
    out_specs = [row_spec(n, m) for m, n, _ in _PROJECT_OUTS]
    out_shapes[_IK_OUT] = jax.ShapeDtypeStruct((nb, IDX_DIM, t), F32)
    out_specs[_IK_OUT] = pl.BlockSpec((None, IDX_DIM, tm), lambda b, i: (b, 0, i))
    out_specs += [row_spec(N_QUADS * QUAD_LANES), row_spec(KV_DIM),
                  pl.BlockSpec((None, None, KV_DIM, tm), lambda b, i: (b, i, 0, 0)),
                  pl.BlockSpec((None, CONV_WIDTH - 1, D_CONV), lambda b, i: (b, 0, 0))]
    return pl.pallas_call(
        functools.partial(_project_prompt_kernel, tm=tm),
        grid=grid,
        in_specs=[row_spec(D_MODEL), _resident((1, D_MODEL), 2), _resident((D_PROJ, D_MODEL), 2),
                  _resident((CONV_WIDTH, D_CONV), 2)],
        out_specs=out_specs,
        out_shape=out_shapes,
        scratch_shapes=[pltpu.VMEM((PAD_ROWS + tm, D_CONV), F32)],
        compiler_params=pltpu.CompilerParams(dimension_semantics=("arbitrary", "arbitrary"),
                                             vmem_limit_bytes=VMEM_LIMIT_BYTES),
        name="project_prompt",
    )(x, g_pre, w_t, w_conv)


def _project_sample(x, g_pre, w_t, w_conv, state):
    nseq, seqlen, _ = x.shape
    rows = nseq * seqlen
    full = lambda shape: pl.BlockSpec(shape, lambda i: (0,) * len(shape))
    out_shapes = _project_out_shapes((), rows) + [jax.ShapeDtypeStruct((nseq, CONV_WIDTH - 1, D_CONV), F32)]
    out_specs = [full((m * rows, n)) for m, n, _ in _PROJECT_OUTS] + [full((nseq, CONV_WIDTH - 1, D_CONV))]
    return pl.pallas_call(
        functools.partial(_project_sample_kernel, nseq=nseq, seqlen=seqlen),
        grid=(1,),
        in_specs=[full((rows, D_MODEL)), full((1, D_MODEL)), _resident((D_PROJ, D_MODEL), 1),
                  full((CONV_WIDTH, D_CONV)), full((nseq, CONV_WIDTH - 1, D_CONV))],
        out_specs=out_specs,
        out_shape=out_shapes,
        scratch_shapes=[pltpu.VMEM((nseq, PAD_ROWS + seqlen, D_CONV), F32)],
        compiler_params=pltpu.CompilerParams(dimension_semantics=("arbitrary",),
                                             vmem_limit_bytes=VMEM_LIMIT_BYTES),
        name="project_sample",
    )(x.reshape(rows, D_MODEL), g_pre, w_t, w_conv, state)


def _key_of_score(x):
    bits = lax.bitcast_convert_type(x, I32)
    return bits ^ ((bits >> 31) & INT_MAX)


def _score_of_key(key):
    return lax.bitcast_convert_type(key ^ ((key >> 31) & INT_MAX), F32)


def _store_scores(score, admissible, sc_scr, hi_scr, lo_scr, kt):
    score = jnp.where(admissible, score, NEG_INF)
    sc_scr[kt] = score
    key = _key_of_score(score)
    hi_scr[kt] = (key >> HALF_BITS).astype(I16)
    lo_scr[kt] = key.astype(I16) ^ jnp.int16(HALF_MIN)


def _vreg_sum(p, rows):
    parts = [p[r:r + rows, :] for r in range(0, p.shape[0], rows)]
    while len(parts) > 1:
        parts = [a + b for a, b in zip(parts[::2], parts[1::2])] + parts[len(parts) & ~1:]
    return parts[0]


def _threshold_key_guess(for_tiles, hi_scr, lo_scr, width):
    def count_half(ref, th):
        def body(kt, c16):
            return c16 + _vreg_sum(jnp.where(ref[kt] >= th, jnp.int16(1), jnp.int16(0)), I16_ROWS)

        c16 = for_tiles(body, jnp.zeros((I16_ROWS, width), I16), COUNT_UNROLL)
        return jnp.sum(c16.astype(I32), axis=0, keepdims=True)

    def bisect_half(ref):
        def step(_, carry):
            lo, hi = carry
            mid = (lo + hi + 1) >> 1
            cnt = count_half(ref, jnp.minimum(mid, HALF_MAX).astype(I16))
            ge = (cnt >= TOPK_MAX) & (mid <= HALF_MAX)
            return jnp.where(ge, mid, lo), jnp.where(ge, hi, mid)

        lo0 = jnp.full((1, width), HALF_MIN, I32)
        hi0 = jnp.full((1, width), HALF_MAX + 1, I32)
        return lax.fori_loop(0, HALF_BITS, step, (lo0, hi0))[0]

    top = bisect_half(hi_scr)
    top16 = top.astype(I16)

    def narrow(kt, carry):
        h = hi_scr[kt]
        lo_scr[kt] = jnp.where(h == top16, lo_scr[kt],
                               jnp.where(h > top16, jnp.int16(HALF_MAX), jnp.int16(HALF_MIN)))
        return carry

    for_tiles(narrow, 0)
    bot = bisect_half(lo_scr)
    return top * (HALF_MASK + 1) + (bot - HALF_MIN)


def _topk_threshold(for_tiles, sc_scr, hi_scr, lo_scr, thr_scr, *, tk, width, n_cols):
    def count(pred_fn):
        def body(kt, c8):
            return c8 + _vreg_sum(pred_fn(kt, sc_scr[kt]).astype(I32), SUBLANES)

        c8 = for_tiles(body, jnp.zeros((SUBLANES, width), I32), COUNT_UNROLL)
        return jnp.sum(c8, axis=0, keepdims=True)

    hint_lo = _threshold_key_guess(for_tiles, hi_scr, lo_scr, width)
    hint_hi = hint_lo + 1

    def open_lanes(lo, hi):
        return jnp.max((hi > lo + 1).astype(F32)) > 0.0

    def cond(state):
        i, lo, hi, _ = state
        return (i < MAX_SEARCH_STEPS) & open_lanes(lo, hi)

    def step(state):
        i, lo, hi, n_lo = state
        mid = (lo >> 1) + (hi >> 1) + ((lo | hi) & 1)
        cand = jnp.where(i == 0, hint_lo, jnp.where(i == 1, hint_hi, mid))
        cand = jnp.minimum(jnp.maximum(cand, lo + 1), hi)
        cand_score = _score_of_key(cand)
        cnt = count(lambda kt, t: t >= cand_score)
        ge = cnt >= TOPK_MAX
        hit = cnt == TOPK_MAX
        lo_new = jnp.where(ge, cand, lo)
        hi_new = jnp.where(hit, cand + 1, jnp.where(ge, hi, cand))
        return i + 1, lo_new, hi_new, jnp.where(ge, cnt, n_lo)

    lo0 = jnp.full((1, width), KEY_NEG_INF, I32)
    hi0 = jnp.full((1, width), KEY_POS_INF + 1, I32)
    n0 = jnp.full((1, width), INT_MAX, I32)
    _, lo, _, n_lo = lax.while_loop(cond, step, (jnp.int32(0), lo0, hi0, n0))
    thr = _score_of_key(lo)
    thr_scr[...] = jnp.broadcast_to(thr, (SUBLANES, width))

    @pl.when(jnp.max((n_lo > TOPK_MAX).astype(F32)) > 0.0)
    def _():
        need = TOPK_MAX - count(lambda kt, t: t > thr)

        def pos_of(kt):
            return kt * tk + lax.broadcasted_iota(I32, (tk, width), 0)

        def pos_bisect(_, carry):
            plo, phi = carry
            mid = (plo + phi) >> 1
            ok = count(lambda kt, t: (t == thr) & (pos_of(kt) <= mid)) >= need
            return jnp.where(ok, plo, mid), jnp.where(ok, mid, phi)

        plo0 = jnp.full((1, width), -1, I32)
        phi0 = jnp.full((1, width), n_cols - 1, I32)
        steps = max(1, (n_cols - 1).bit_length()) + 1
        _, pos = lax.fori_loop(0, steps, pos_bisect, (plo0, phi0))

        def drop(kt, carry):
            t = sc_scr[kt]
            sc_scr[kt] = jnp.where((t == thr) & (pos_of(kt) > pos), NEG_INF, t)
            return carry

        for_tiles(drop, 0)


def _softmax_step(lg, m_ref, l_ref, acc_ref, pv_fn):
    m_prev = m_ref[0:1, :]
    m_new = jnp.maximum(m_prev, jnp.max(lg, axis=0, keepdims=True))
    m_safe = jnp.where(m_new == NEG_INF, 0.0, m_new)
    alpha = jnp.exp2(m_prev - m_safe)
    p = jnp.exp2(lg - m_safe)
    l_new = alpha * l_ref[0:1, :] + jnp.sum(p, axis=0, keepdims=True)
    acc_ref[...] = acc_ref[...] * alpha + pv_fn(p.astype(BF16))
    m_ref[...] = jnp.broadcast_to(m_new, m_ref.shape)
    l_ref[...] = jnp.broadcast_to(l_new, l_ref.shape)


def _attend_prompt_kernel(qi_ref, aux_ref, q_ref, ga_ref, kpad_ref, k_ref, vt_ref, z_ref, sc_scr,
                          hi_scr, lo_scr, thr_scr, m_scr, l_scr, acc_scr, lg_scr, *, tq, tk, n_cols):
    j = pl.program_id(1)
    n_tiles = j + 1

    def for_tiles(fn, init, unroll=2, halve=False):
        shift = unroll.bit_length() - 1
        assert unroll == 1 << shift
        done, carry = 0, init
        for size in ([unroll >> s for s in range(shift + 1)] if halve else [unroll, 1]):
            n_groups = (n_tiles - done) // size

            def body(g, carry, size=size, done=done):
                for u in range(size):
                    carry = fn(done + g * size + u, carry)
                return carry

            carry = lax.fori_loop(0, n_groups, body, carry)
            done = done + n_groups * size
        return carry

    assert tq == tk == TOPK_MAX and tq % CHUNK == 0
    qi = qi_ref[...]
    qstack = jnp.concatenate([qi[:, u * QUAD_LANES:(u + 1) * QUAD_LANES] for u in range(N_QUADS)], axis=0)
    w_t = aux_ref[...].T
    w_rows = [w_t[IDX_DIM + h:IDX_DIM + h + 1, :] for h in range(N_IDX_HEADS)]
    key_chunk = lax.broadcasted_iota(I32, (tk, tq), 0) // CHUNK
    qry_chunk = lax.broadcasted_iota(I32, (tk, tq), 1) // CHUNK
    diag_adm = key_chunk <= qry_chunk

    def idx_tile(kt, carry):
        start = pl.multiple_of(kt * tk, tk)
        acc = jnp.zeros((tk, tq), F32)
        for c in range(HEADS_PER_QUAD):
            kp = kpad_ref[pl.ds(start, tk), c * QUAD_LANES:(c + 1) * QUAD_LANES]
            s = _dot_nt(kp, qstack)
            for u in range(N_QUADS):
                acc = acc + jnp.maximum(s[:, u * tq:(u + 1) * tq], 0.0) * w_rows[HEADS_PER_QUAD * u + c]
        _store_scores(acc * IDX_SCALE, (diag_adm & (kt == j)) | (kt < j), sc_scr, hi_scr, lo_scr, kt)
        return carry

    for_tiles(idx_tile, 0, IDX_UNROLL, halve=True)

    thr_scr[...] = jnp.full((SUBLANES, tq), F32_LOWEST, F32)

    @pl.when(j >= 1)
    def _():
        _topk_threshold(for_tiles, sc_scr, hi_scr, lo_scr, thr_scr, tk=tk, width=tq, n_cols=n_cols)

    thr = thr_scr[0:1, :]

    q = q_ref[...]
    qn = [jnp.concatenate([q[:, (KV_GROUP * n + g) * HEAD_DIM:(KV_GROUP * n + g + 1) * HEAD_DIM]
                           for g in range(KV_GROUP)], axis=0) for n in range(N_KV_HEADS)]
    m_scr[...] = jnp.full(m_scr.shape, NEG_INF, F32)
    l_scr[...] = jnp.zeros(l_scr.shape, F32)
    acc_scr[...] = jnp.zeros(acc_scr.shape, F32)

    def att_tiles(kts):
        for slot, kt in enumerate(kts):
            start = pl.multiple_of(kt * tk, tk)
            for n in range(N_KV_HEADS):
                lg_scr[slot, n] = _dot_nt(k_ref[pl.ds(start, tk), n * HEAD_DIM:(n + 1) * HEAD_DIM], qn[n])
        for slot, kt in enumerate(kts):
            sel = sc_scr[kt] >= thr
            for n in range(N_KV_HEADS):
                lg = jnp.concatenate([jnp.where(sel, lg_scr[slot, n, :, g * tq:(g + 1) * tq], NEG_INF)
                                      for g in range(KV_GROUP)], axis=1)
                vt_n = vt_ref[kt, n * HEAD_DIM:(n + 1) * HEAD_DIM, :]
                _softmax_step(lg, m_scr.at[n], l_scr.at[n], acc_scr.at[n],
                              lambda p: jnp.dot(vt_n, p, preferred_element_type=F32))

    def att_group(size):
        def body(i, carry):
            att_tiles(tuple(i + u for u in range(size)))
            return carry

        return body

    done = 0
    size = ATT_GROUP
    while size >= 1:
        n_groups = (n_tiles - done) // size
        lax.fori_loop(0, n_groups, lambda g, c, size=size, done=done: att_group(size)(done + g * size, c), 0)
        done = done + n_groups * size
        size //= 2

    outs = []
    for n in range(N_KV_HEADS):
        o = acc_scr[n] / l_scr[n][0:1, :]
        outs.extend(o[:, g * tq:(g + 1) * tq].T for g in range(KV_GROUP))
    z_ref[...] = (jnp.concatenate(outs, axis=1) * _silu(ga_ref[...])).astype(BF16)


def _attend_prompt(qi, aux, q, ga, kpad, kb, vt, tq):
    nb, t, _ = q.shape
    tk = tq
    n_tiles = t // tk
    lanes_q = KV_GROUP * tq
    blk = lambda n: pl.BlockSpec((None, tq, n), lambda b, j: (b, j, 0))
    seq = lambda n: pl.BlockSpec((None, t, n), lambda b, j: (b, 0, 0))
    scratch = [pltpu.VMEM((n_tiles, tk, tq), F32), pltpu.VMEM((n_tiles, tk, tq), I16),
               pltpu.VMEM((n_tiles, tk, tq), I16), pltpu.VMEM((SUBLANES, tq), F32),
               pltpu.VMEM((N_KV_HEADS, SUBLANES, lanes_q), F32),
               pltpu.VMEM((N_KV_HEADS, SUBLANES, lanes_q), F32),
               pltpu.VMEM((N_KV_HEADS, HEAD_DIM, lanes_q), F32),
               pltpu.VMEM((ATT_GROUP, N_KV_HEADS, tk, lanes_q), F32)]
    return pl.pallas_call(
        functools.partial(_attend_prompt_kernel, tq=tq, tk=tk, n_cols=t),
        grid=(nb, t // tq),
        in_specs=[blk(N_IDX_HEADS * IDX_DIM), blk(LANES), blk(D_ATTN), blk(D_ATTN),
                  seq(N_QUADS * QUAD_LANES), seq(KV_DIM),
                  pl.BlockSpec((None, n_tiles, KV_DIM, tk), lambda b, j: (b, 0, 0, 0))],
        out_specs=blk(D_ATTN),
        out_shape=jax.ShapeDtypeStruct((nb, t, D_ATTN), BF16),
        scratch_shapes=scratch,
        compiler_params=pltpu.CompilerParams(dimension_semantics=("arbitrary", "arbitrary"),
                                             vmem_limit_bytes=VMEM_LIMIT_BYTES),
        name="attend_prompt",
    )(qi, aux, q, ga, kpad, kb, vt)


def _attend_sample_kernel(qi_ref, aux_ref, q_ref, ga_ref, cikt_ref, ck_ref, cv_ref, nk_ref, nv_ref,
                          z_ref, kit_s, k_s, vt_s, sc_scr, hi_scr, lo_scr, thr_scr, m_scr, l_scr, acc_scr,
                          *, tq, tk, past, n_tiles):
    n_keys = past + tq
    hl = N_IDX_HEADS * tq
    streams = range(SAMPLE_STREAMS)
    width = SAMPLE_STREAMS * LANES
    assert N_HEADS * tq == LANES and hl == 2 * LANES and past % tk == 0 and tq <= tk
    assert n_keys > TOPK_MAX and past % CHUNK == 0 and tq <= CHUNK

    def for_tiles(fn, init, unroll=None):
        for kt in range(n_tiles):
            init = fn(kt, init)
        return init

    for s in streams:
        lanes = slice(s * LANES, (s + 1) * LANES)
        aux = aux_ref[s]
        kit_s[s, :, 0:past] = cikt_ref[s].astype(BF16)
        kit_s[s, :, past:] = jnp.zeros((IDX_DIM, n_tiles * tk - past), BF16)
        kit_s[s, :, past:past + tq] = aux.T[:IDX_DIM, :].astype(BF16)
        k_s[s, past + tq:, :] = jnp.zeros((n_tiles * tk - n_keys, KV_DIM), BF16)
        vt_s[s, n_tiles - 1] = jnp.zeros((KV_DIM, tk), BF16)
        for n in range(N_KV_HEADS):
            cols = slice(n * HEAD_DIM, (n + 1) * HEAD_DIM)
            k_s[s, 0:past, cols] = ck_ref[s, pl.ds(n, past, stride=N_KV_HEADS), :].astype(BF16)
            k_s[s, past:past + tq, cols] = nk_ref[s, pl.ds(n, tq, stride=N_KV_HEADS), :].astype(BF16)
            for kt in range(past // tk):
                v_tile = cv_ref[s, pl.ds(N_KV_HEADS * kt * tk + n, tk, stride=N_KV_HEADS), :]
                vt_s[s, kt, cols, :] = v_tile.T.astype(BF16)
            vt_s[s, n_tiles - 1, cols, 0:tq] = nv_ref[s, pl.ds(n, tq, stride=N_KV_HEADS), :].T.astype(BF16)

        qi = qi_ref[s]
        qrows = jnp.concatenate([qi[:, h * IDX_DIM:(h + 1) * IDX_DIM] for h in range(N_IDX_HEADS)], axis=0)
        w_rows = jnp.concatenate([aux[:, IDX_DIM + h:IDX_DIM + h + 1] for h in range(N_IDX_HEADS)], axis=0)
        w_rows = jnp.broadcast_to(w_rows, (hl, LANES))

        def idx_tile(kt, carry):
            d = jnp.dot(qrows, kit_s[s, :, kt * tk:(kt + 1) * tk], preferred_element_type=F32)
            y = jnp.concatenate([jnp.maximum(d[:, c:c + LANES], 0.0) * w_rows for c in range(0, tk, LANES)],
                                axis=1)
            per_query = _vreg_sum(y, tq) * IDX_SCALE
            score = jnp.concatenate([per_query] * N_HEADS, axis=0).T
            pos = kt * tk + lax.broadcasted_iota(I32, (tk, LANES), 0)
            _store_scores(score, pos < n_keys, sc_scr.at[:, :, lanes], hi_scr.at[:, :, lanes],
                          lo_scr.at[:, :, lanes], kt)
            return carry

        for_tiles(idx_tile, 0)

    _topk_threshold(for_tiles, sc_scr, hi_scr, lo_scr, thr_scr, tk=tk, width=width, n_cols=n_tiles * tk)
    thr = thr_scr[0:1, :]

    zeros = jnp.zeros((tq, HEAD_DIM), BF16)
    m_scr[...] = jnp.full(m_scr.shape, NEG_INF, F32)
    l_scr[...] = jnp.zeros(l_scr.shape, F32)
    acc_scr[...] = jnp.zeros(acc_scr.shape, F32)
    for s in streams:
        lanes = slice(s * LANES, (s + 1) * LANES)
        q = q_ref[s]
        qblk = jnp.concatenate(
            [jnp.concatenate([q[:, (KV_GROUP * n + g) * HEAD_DIM:(KV_GROUP * n + g + 1) * HEAD_DIM]
                              if m == n else zeros for m in range(N_KV_HEADS)], axis=1)
             for n in range(N_KV_HEADS) for g in range(KV_GROUP)], axis=0)

        def att_tile(kt, carry):
            lg = jnp.where(sc_scr[kt, :, lanes] >= thr[:, lanes],
                           _dot_nt(k_s[s, kt * tk:(kt + 1) * tk, :], qblk), NEG_INF)
            _softmax_step(lg, m_scr.at[:, lanes], l_scr.at[:, lanes], acc_scr.at[:, lanes],
                          lambda p: jnp.dot(vt_s[s, kt], p, preferred_element_type=F32))
            return carry

        for_tiles(att_tile, 0)

        o_t = (acc_scr[:, lanes] / l_scr[0:1, lanes]).T
        outs = [o_t[(KV_GROUP * n + g) * tq:(KV_GROUP * n + g + 1) * tq, n * HEAD_DIM:(n + 1) * HEAD_DIM]
                for n in range(N_KV_HEADS) for g in range(KV_GROUP)]
        z_ref[s] = (jnp.concatenate(outs, axis=1) * _silu(ga_ref[s])).astype(BF16)


def _attend_sample(qi, aux, q, ga, cache_ik, cache_k, cache_v, new_k, new_v, tk):
    nseq, tq, _ = q.shape
    past = cache_ik.shape[1]
    assert nseq % SAMPLE_STREAMS == 0
    cache_ikt = jnp.transpose(cache_ik, (0, 2, 1))
    n_tiles = -(-(past + tq) // tk)
    width = SAMPLE_STREAMS * LANES
    blk = lambda n, m=1: pl.BlockSpec((SAMPLE_STREAMS, m * tq, n), lambda b: (b, 0, 0))
    cache = lambda n, m=1: pl.BlockSpec((SAMPLE_STREAMS, m * past, n), lambda b: (b, 0, 0))
    scratch = [pltpu.VMEM((SAMPLE_STREAMS, IDX_DIM, n_tiles * tk), BF16),
               pltpu.VMEM((SAMPLE_STREAMS, n_tiles * tk, KV_DIM), BF16),
               pltpu.VMEM((SAMPLE_STREAMS, n_tiles, KV_DIM, tk), BF16),
               pltpu.VMEM((n_tiles, tk, width), F32), pltpu.VMEM((n_tiles, tk, width), I16),
               pltpu.VMEM((n_tiles, tk, width), I16),
               pltpu.VMEM((SUBLANES, width), F32),
               pltpu.VMEM((SUBLANES, width), F32), pltpu.VMEM((SUBLANES, width), F32),
               pltpu.VMEM((KV_DIM, width), F32)]
    return pl.pallas_call(
        functools.partial(_attend_sample_kernel, tq=tq, tk=tk, past=past, n_tiles=n_tiles),
        grid=(nseq // SAMPLE_STREAMS,),
        in_specs=[blk(N_IDX_HEADS * IDX_DIM), blk(LANES), blk(D_ATTN), blk(D_ATTN),
                  pl.BlockSpec((SAMPLE_STREAMS, IDX_DIM, past), lambda b: (b, 0, 0)),
                  cache(HEAD_DIM, N_KV_HEADS), cache(HEAD_DIM, N_KV_HEADS), blk(HEAD_DIM, N_KV_HEADS),
                  blk(HEAD_DIM, N_KV_HEADS)],
        out_specs=blk(D_ATTN),
        out_shape=jax.ShapeDtypeStruct((nseq, tq, D_ATTN), BF16),
        scratch_shapes=scratch,
        compiler_params=pltpu.CompilerParams(dimension_semantics=("arbitrary",),
                                             vmem_limit_bytes=VMEM_LIMIT_BYTES),
        name="attend_sample",
    )(qi, aux, q, ga, cache_ikt, cache_k, cache_v, new_k, new_v)


def _merge_kernel(x_ref, za_ref, zc_ref, w_ref, g_ref, y_ref):
    z = jnp.concatenate([za_ref[...], zc_ref[...]], axis=1)
    y = jnp.dot(z, w_ref[...], preferred_element_type=F32)
    y_ref[...] = x_ref[...] + _rmsnorm(y, g_ref[...])


def _merge(x, za, zc, w_out, g_post, tm):
    rows = x.shape[0]
    row_spec = lambda n: pl.BlockSpec((tm, n), lambda i: (i, 0))
    return pl.pallas_call(
        _merge_kernel,
        grid=(rows // tm,),
        in_specs=[row_spec(D_MODEL), row_spec(D_ATTN), row_spec(D_CONV), _resident((D_MODEL, D_MODEL), 1),
                  _resident((1, D_MODEL), 1)],
        out_specs=row_spec(D_MODEL),
        out_shape=jax.ShapeDtypeStruct((rows, D_MODEL), F32),
        compiler_params=pltpu.CompilerParams(dimension_semantics=("arbitrary",),
                                             vmem_limit_bytes=VMEM_LIMIT_BYTES),
        name="merge",
    )(x, za, zc, w_out, g_post)


PROJECT_ROWS = 256
ATTEND_ROWS = 256
MERGE_ROWS = 512


def _layer(xp, xs, cache_k, cache_v, cache_ik, state, g_pre, w_in, w_conv, w_out, g_post):
    nb, t, _ = xp.shape
    nseq, seqlen, _ = xs.shape
    g_pre = g_pre.reshape(1, D_MODEL)
    g_post = g_post.reshape(1, D_MODEL)
    w_t = w_in.T.astype(BF16)
    w_out = w_out.astype(BF16)

    assert PROJECT_ROWS == ATTEND_ROWS
    q, k, v, ga, qi, ik, aux, zc, kpad, kb, vt, cs = _project_prompt(xp, g_pre, w_t, w_conv, PROJECT_ROWS)
    za = _attend_prompt(qi, aux, q, ga, kpad, kb, vt, ATTEND_ROWS)
    yp = _merge(xp.reshape(nb * t, D_MODEL), za.reshape(nb * t, D_ATTN), zc.reshape(nb * t, D_CONV),
                w_out, g_post, MERGE_ROWS).reshape(nb, t, D_MODEL)

    sq, sk, sv, sga, sqi, sik, saux, szc, scs = _project_sample(xs, g_pre, w_t, w_conv, state)
    per_seq = lambda a: a.reshape(nseq, -1, a.shape[-1])
    kv_rows = lambda a: a.reshape(nseq, -1, HEAD_DIM)
    sza = _attend_sample(per_seq(sqi), per_seq(saux), per_seq(sq), per_seq(sga), cache_ik,
                         kv_rows(cache_k), kv_rows(cache_v), per_seq(sk), per_seq(sv), ATTEND_ROWS)
    ys = _merge(xs.reshape(nseq * seqlen, D_MODEL), sza.reshape(nseq * seqlen, D_ATTN), szc, w_out,
                g_post, MERGE_ROWS).reshape(nseq, seqlen, D_MODEL)

    heads = lambda a, lead: a.reshape(lead + (N_KV_HEADS, HEAD_DIM))
    return (yp, ys, heads(k, (nb, t)), heads(v, (nb, t)), jnp.transpose(ik, (0, 2, 1)), cs,
            heads(sk, (nseq, seqlen)), heads(sv, (nseq, seqlen)), per_seq(sik), scs)


def kernel(x_prompt, x_sample, cache_k, cache_v, cache_idx_k, state_conv, g_pre, w_in, w_conv, w_out,
           g_post):
    depth = g_pre.shape[0]
    xp, xs = x_prompt, x_sample
    outs = []
    for l in range(depth):
        res = _layer(xp, xs, cache_k[l], cache_v[l], cache_idx_k[l], state_conv[l], g_pre[l], w_in[l],
                     w_conv[l], w_out[l], g_post[l])
        xp, xs = res[0], res[1]
        outs.append(res[2:])
    stacked = [jnp.stack([o[i] for o in outs]) for i in range(8)]
    return (xp, xs) + tuple(stacked)
```

```python
import functools

import jax
import jax.numpy as jnp
from jax import lax
from jax.experimental import pallas as pl
from jax.experimental.pallas import tpu as pltpu

F32 = jnp.float32
BF16 = jnp.bfloat16
I32 = jnp.int32
I16 = jnp.int16

D_MODEL = 2048
D_ATTN = 1024
D_CONV = 1024
HEAD_DIM = 128
N_KV_HEADS = 2
KV_GROUP = 4
N_HEADS = N_KV_HEADS * KV_GROUP
KV_DIM = N_KV_HEADS * HEAD_DIM
N_IDX_HEADS = 16
IDX_DIM = 64
TOPK_MAX = 256
CHUNK = 64
CONV_WIDTH = 3
RMS_EPS = 1e-6

LANES = 128
SUBLANES = 8
I16_ROWS = 2 * SUBLANES
VMEM_LIMIT_BYTES = 60 * 1000 * 1024

OFF_Q = 0
OFF_K = OFF_Q + D_ATTN
OFF_V = OFF_K + KV_DIM
OFF_GA = OFF_V + KV_DIM
OFF_QI = OFF_GA + D_ATTN
OFF_KW = OFF_QI + N_IDX_HEADS * IDX_DIM
OFF_B = OFF_KW + IDX_DIM + N_IDX_HEADS
OFF_C = OFF_B + D_CONV
OFF_HC = OFF_C + D_CONV
OFF_GB = OFF_HC + D_CONV
D_PROJ = OFF_GB + D_CONV
assert all(o % I16_ROWS == 0 for o in (OFF_K, OFF_V, OFF_GA, OFF_QI, OFF_KW, OFF_B, OFF_C, OFF_HC, OFF_GB))

CONV_COLS = 256
PAD_ROWS = SUBLANES
HEADS_PER_QUAD = 4
QUAD_LANES = HEADS_PER_QUAD * IDX_DIM
N_QUADS = N_IDX_HEADS // HEADS_PER_QUAD
IDX_SCALE = (IDX_DIM ** -0.5) * (N_IDX_HEADS ** -0.5)
LOG2E = 1.4426950408889634
ATT_SCALE_LOG2 = HEAD_DIM ** -0.5 * LOG2E
INT_MAX = 2 ** 31 - 1
HALF_BITS = 16
HALF_MASK = 2 ** HALF_BITS - 1
HALF_MIN = -(2 ** (HALF_BITS - 1))
HALF_MAX = 2 ** (HALF_BITS - 1) - 1
KEY_POS_INF = 0x7F800000
KEY_NEG_INF = -KEY_POS_INF - 1
MAX_SEARCH_STEPS = 36
F32_LOWEST = -3.4028234663852886e38
COUNT_UNROLL = 4
ATT_GROUP = 4
IDX_UNROLL = 8
SAMPLE_STREAMS = 2
NEG_INF = float("-inf")


def _silu(x):
    return x * jax.nn.sigmoid(x)


def _dot_nt(a, b):
    return lax.dot_general(a, b, (((1,), (1,)), ((), ())), preferred_element_type=F32)


def _rmsnorm(x, g):
    ms = jnp.mean(x * x, axis=-1, keepdims=True)
    return x * lax.rsqrt(ms + RMS_EPS) * g


def _project_rows(h, w_ref, q_ref, k_ref, v_ref, ga_ref, qi_ref, ik_ref, aux_ref):
    rows = h.shape[0]

    def mm(r0, n):
        return _dot_nt(h, w_ref[r0:r0 + n, :])

    q_ref[...] = (mm(OFF_Q, D_ATTN) * ATT_SCALE_LOG2).astype(BF16)
    kk = mm(OFF_K, KV_DIM)
    vv = mm(OFF_V, KV_DIM)
    for n in range(N_KV_HEADS):
        k_ref[pl.ds(n, rows, stride=N_KV_HEADS), :] = kk[:, n * HEAD_DIM:(n + 1) * HEAD_DIM]
        v_ref[pl.ds(n, rows, stride=N_KV_HEADS), :] = vv[:, n * HEAD_DIM:(n + 1) * HEAD_DIM]
    ga_ref[...] = mm(OFF_GA, D_ATTN)
    qi_ref[...] = mm(OFF_QI, N_IDX_HEADS * IDX_DIM).astype(BF16)
    kw = mm(OFF_KW, LANES)
    aux_ref[...] = kw
    if ik_ref.shape[0] == IDX_DIM:
        ik_ref[...] = kw.T[:IDX_DIM, :]
    else:
        ik_ref[...] = kw[:, :IDX_DIM]
    return kk, vv, kw


def _conv_chunk(h, w_ref, c):
    def mm(off):
        return _dot_nt(h, w_ref[off + c:off + c + CONV_COLS, :])

    return mm(OFF_B), mm(OFF_C) * mm(OFF_HC), mm(OFF_GB)


def _conv_out(bg, gb, u, um1, um2, wc_ref, c):
    w0 = wc_ref[0:1, c:c + CONV_COLS]
    w1 = wc_ref[1:2, c:c + CONV_COLS]
    w2 = wc_ref[2:3, c:c + CONV_COLS]
    conv = w0 * um2 + w1 * um1 + w2 * u
    return (bg * conv * _silu(gb)).astype(BF16)


def _project_prompt_kernel(x_ref, g_ref, w_ref, wc_ref, q_ref, k_ref, v_ref, ga_ref, qi_ref, ik_ref,
                           aux_ref, zc_ref, kpad_ref, kb_ref, vt_ref, cs_ref, upad_ref, *, tm):
    @pl.when(pl.program_id(1) == 0)
    def _():
        upad_ref[0:PAD_ROWS, :] = jnp.zeros((PAD_ROWS, D_CONV), F32)

    h = _rmsnorm(x_ref[...], g_ref[...]).astype(BF16)
    kk, vv, kw = _project_rows(h, w_ref, q_ref, k_ref, v_ref, ga_ref, qi_ref, ik_ref, aux_ref)
    kb_ref[...] = kk.astype(BF16)
    vt_ref[...] = vv.T.astype(BF16)
    lane = lax.broadcasted_iota(I32, kw.shape, 1)
    lo = jnp.where(lane < IDX_DIM, kw, 0.0)
    hi = pltpu.roll(lo, IDX_DIM, axis=1)
    zero = jnp.zeros_like(lo)
    for i, blk in enumerate((lo, zero, hi, zero, zero, lo, zero, hi)):
        kpad_ref[:, i * LANES:(i + 1) * LANES] = blk.astype(BF16)

    for c in range(0, D_CONV, CONV_COLS):
        bg, u, gb = _conv_chunk(h, w_ref, c)
        upad_ref[PAD_ROWS:PAD_ROWS + tm, c:c + CONV_COLS] = u
        um1 = upad_ref[PAD_ROWS - 1:PAD_ROWS - 1 + tm, c:c + CONV_COLS]
        um2 = upad_ref[PAD_ROWS - 2:PAD_ROWS - 2 + tm, c:c + CONV_COLS]
        zc_ref[:, c:c + CONV_COLS] = _conv_out(bg, gb, u, um1, um2, wc_ref, c)
    last = upad_ref[PAD_ROWS + tm - (CONV_WIDTH - 1):PAD_ROWS + tm, :]
    cs_ref[...] = last
    upad_ref[PAD_ROWS - (CONV_WIDTH - 1):PAD_ROWS, :] = last


def _project_sample_kernel(x_ref, g_ref, w_ref, wc_ref, st_ref, q_ref, k_ref, v_ref, ga_ref, qi_ref,
                           ik_ref, aux_ref, zc_ref, cs_ref, upad_ref, *, nseq, seqlen):
    rows = nseq * seqlen
    upad_ref[:, PAD_ROWS - (CONV_WIDTH - 1):PAD_ROWS, :] = st_ref[...]
    h = _rmsnorm(x_ref[...], g_ref[...]).astype(BF16)
    _project_rows(h, w_ref, q_ref, k_ref, v_ref, ga_ref, qi_ref, ik_ref, aux_ref)
    for c in range(0, D_CONV, CONV_COLS):
        bg, u, gb = _conv_chunk(h, w_ref, c)
        upad_ref[:, PAD_ROWS:PAD_ROWS + seqlen, c:c + CONV_COLS] = u.reshape(nseq, seqlen, CONV_COLS)
        um1 = upad_ref[:, PAD_ROWS - 1:PAD_ROWS - 1 + seqlen, c:c + CONV_COLS].reshape(rows, CONV_COLS)
        um2 = upad_ref[:, PAD_ROWS - 2:PAD_ROWS - 2 + seqlen, c:c + CONV_COLS].reshape(rows, CONV_COLS)
        zc_ref[:, c:c + CONV_COLS] = _conv_out(bg, gb, u, um1, um2, wc_ref, c)
    cs_ref[...] = upad_ref[:, PAD_ROWS + seqlen - (CONV_WIDTH - 1):PAD_ROWS + seqlen, :]


_PROJECT_OUTS = ((1, D_ATTN, BF16), (N_KV_HEADS, HEAD_DIM, F32), (N_KV_HEADS, HEAD_DIM, F32),
                 (1, D_ATTN, F32), (1, N_IDX_HEADS * IDX_DIM, BF16), (1, IDX_DIM, F32), (1, LANES, F32),
                 (1, D_CONV, BF16))
_IK_OUT = 5


def _project_out_shapes(lead, rows):
    return [jax.ShapeDtypeStruct(lead + (m * rows, n), dt) for m, n, dt in _PROJECT_OUTS]


def _resident(shape, ngrid):
    zeros = (0,) * len(shape)
    if ngrid == 1:
        return pl.BlockSpec(shape, lambda i: zeros, pipeline_mode=pl.Buffered(1))
    return pl.BlockSpec(shape, lambda b, i: zeros, pipeline_mode=pl.Buffered(1))


def _project_prompt(x, g_pre, w_t, w_conv, tm):
    nb, t, _ = x.shape
    grid = (nb, t // tm)
    row_spec = lambda n, m=1: pl.BlockSpec((None, m * tm, n), lambda b, i: (b, i, 0))
    out_shapes = _project_out_shapes((nb,), t) + [
        jax.ShapeDtypeStruct((nb, t, N_QUADS * QUAD_LANES), BF16),
        jax.ShapeDtypeStruct((nb, t, KV_DIM), BF16),
        jax.ShapeDtypeStruct((nb, t // tm, KV_DIM, tm), BF16),
        jax.ShapeDtypeStruct((nb, CONV_WIDTH - 1, D_CONV), F32)]
    out_specs = [row_spec(n, m) for m, n, _ in _PROJECT_OUTS]
    out_shapes[_IK_OUT] = jax.ShapeDtypeStruct((nb, IDX_DIM, t), F32)
    out_specs[_IK_OUT] = pl.BlockSpec((None, IDX_DIM, tm), lambda b, i: (b, 0, i))
    out_specs += [row_spec(N_QUADS * QUAD_LANES), row_spec(KV_DIM),
                  pl.BlockSpec((None, None, KV_DIM, tm), lambda b, i: (b, i, 0, 0)),
                  pl.BlockSpec((None, CONV_WIDTH - 1, D_CONV), lambda b, i: (b, 0, 0))]
    return pl.pallas_call(
        functools.partial(_project_prompt_kernel, tm=tm),
        grid=grid,
        in_specs=[row_spec(D_MODEL), _resident((1, D_MODEL), 2), _resident((D_PROJ, D_MODEL), 2),
                  _resident((CONV_WIDTH, D_CONV), 2)],
        out_specs=out_specs,
        out_shape=out_shapes,
        scratch_shapes=[pltpu.VMEM((PAD_ROWS + tm, D_CONV), F32)],
        compiler_params=pltpu.CompilerParams(dimension_semantics=("arbitrary", "arbitrary"),
                                             vmem_limit_bytes=VMEM_LIMIT_BYTES),
        name="project_prompt",
    )(x, g_pre, w_t, w_conv)


def _project_sample(x, g_pre, w_t, w_conv, state):
    nseq, seqlen, _ = x.shape
    rows = nseq * seqlen
    full = lambda shape: pl.BlockSpec(shape, lambda i: (0,) * len(shape))
    out_shapes = _project_out_shapes((), rows) + [jax.ShapeDtypeStruct((nseq, CONV_WIDTH - 1, D_CONV), F32)]
    out_specs = [full((m * rows, n)) for m, n, _ in _PROJECT_OUTS] + [full((nseq, CONV_WIDTH - 1, D_CONV))]
    return pl.pallas_call(
        functools.partial(_project_sample_kernel, nseq=nseq, seqlen=seqlen),
        grid=(1,),
        in_specs=[full((rows, D_MODEL)), full((1, D_MODEL)), _resident((D_PROJ, D_MODEL), 1),
                  full((CONV_WIDTH, D_CONV)), full((nseq, CONV_WIDTH - 1, D_CONV))],
        out_specs=out_specs,
        out_shape=out_shapes,
        scratch_shapes=[pltpu.VMEM((nseq, PAD_ROWS + seqlen, D_CONV), F32)],
        compiler_params=pltpu.CompilerParams(dimension_semantics=("arbitrary",),
                                             vmem_limit_bytes=VMEM_LIMIT_BYTES),
        name="project_sample",
    )(x.reshape(rows, D_MODEL), g_pre, w_t, w_conv, state)


def _key_of_score(x):
    bits = lax.bitcast_convert_type(x, I32)
    return bits ^ ((bits >> 31) & INT_MAX)


def _score_of_key(key):
    return lax.bitcast_convert_type(key ^ ((key >> 31) & INT_MAX), F32)


def _store_scores(score, admissible, sc_scr, hi_scr, lo_scr, kt):
    score = jnp.where(admissible, score, NEG_INF)
    sc_scr[kt] = score
    key = _key_of_score(score)
    hi_scr[kt] = (key >> HALF_BITS).astype(I16)
    lo_scr[kt] = key.astype(I16) ^ jnp.int16(HALF_MIN)


def _vreg_sum(p, rows):
    parts = [p[r:r + rows, :] for r in range(0, p.shape[0], rows)]
    while len(parts) > 1:
        parts = [a + b for a, b in zip(parts[::2], parts[1::2])] + parts[len(parts) & ~1:]
    return parts[0]


def _threshold_key_guess(for_tiles, hi_scr, lo_scr, width):
    def count_half(ref, th):
        def body(kt, c16):
            return c16 + _vreg_sum(jnp.where(ref[kt] >= th, jnp.int16(1), jnp.int16(0)), I16_ROWS)

        c16 = for_tiles(body, jnp.zeros((I16_ROWS, width), I16), COUNT_UNROLL)
        return jnp.sum(c16.astype(I32), axis=0, keepdims=True)

    def bisect_half(ref):
        def step(_, carry):
            lo, hi = carry
            mid = (lo + hi + 1) >> 1
            cnt = count_half(ref, jnp.minimum(mid, HALF_MAX).astype(I16))
            ge = (cnt >= TOPK_MAX) & (mid <= HALF_MAX)
            return jnp.where(ge, mid, lo), jnp.where(ge, hi, mid)

        lo0 = jnp.full((1, width), HALF_MIN, I32)
        hi0 = jnp.full((1, width), HALF_MAX + 1, I32)
        return lax.fori_loop(0, HALF_BITS, step, (lo0, hi0))[0]

    top = bisect_half(hi_scr)
    top16 = top.astype(I16)

    def narrow(kt, carry):
        h = hi_scr[kt]
        lo_scr[kt] = jnp.where(h == top16, lo_scr[kt],
                               jnp.where(h > top16, jnp.int16(HALF_MAX), jnp.int16(HALF_MIN)))
        return carry

    for_tiles(narrow, 0)
    bot = bisect_half(lo_scr)
    return top * (HALF_MASK + 1) + (bot - HALF_MIN)


def _topk_threshold(for_tiles, sc_scr, hi_scr, lo_scr, thr_scr, *, tk, width, n_cols):
    def count(pred_fn):
        def body(kt, c8):
            return c8 + _vreg_sum(pred_fn(kt, sc_scr[kt]).astype(I32), SUBLANES)

        c8 = for_tiles(body, jnp.zeros((SUBLANES, width), I32), COUNT_UNROLL)
        return jnp.sum(c8, axis=0, keepdims=True)

    hint_lo = _threshold_key_guess(for_tiles, hi_scr, lo_scr, width)
    hint_hi = hint_lo + 1

    def open_lanes(lo, hi):
        return jnp.max((hi > lo + 1).astype(F32)) > 0.0

    def cond(state):
        i, lo, hi, _ = state
        return (i < MAX_SEARCH_STEPS) & open_lanes(lo, hi)

    def step(state):
        i, lo, hi, n_lo = state
        mid = (lo >> 1) + (hi >> 1) + ((lo | hi) & 1)
        cand = jnp.where(i == 0, hint_lo, jnp.where(i == 1, hint_hi, mid))
        cand = jnp.minimum(jnp.maximum(cand, lo + 1), hi)
        cand_score = _score_of_key(cand)
        cnt = count(lambda kt, t: t >= cand_score)
        ge = cnt >= TOPK_MAX
        hit = cnt == TOPK_MAX
        lo_new = jnp.where(ge, cand, lo)
        hi_new = jnp.where(hit, cand + 1, jnp.where(ge, hi, cand))
        return i + 1, lo_new, hi_new, jnp.where(ge, cnt, n_lo)

    lo0 = jnp.full((1, width), KEY_NEG_INF, I32)
    hi0 = jnp.full((1, width), KEY_POS_INF + 1, I32)
    n0 = jnp.full((1, width), INT_MAX, I32)
    _, lo, _, n_lo = lax.while_loop(cond, step, (jnp.int32(0), lo0, hi0, n0))
    thr = _score_of_key(lo)
    thr_scr[...] = jnp.broadcast_to(thr, (SUBLANES, width))

    @pl.when(jnp.max((n_lo > TOPK_MAX).astype(F32)) > 0.0)
    def _():
        need = TOPK_MAX - count(lambda kt, t: t > thr)

        def pos_of(kt):
            return kt * tk + lax.broadcasted_iota(I32, (tk, width), 0)

        def pos_bisect(_, carry):
            plo, phi = carry
            mid = (plo + phi) >> 1
            ok = count(lambda kt, t: (t == thr) & (pos_of(kt) <= mid)) >= need
            return jnp.where(ok, plo, mid), jnp.where(ok, mid, phi)

        plo0 = jnp.full((1, width), -1, I32)
        phi0 = jnp.full((1, width), n_cols - 1, I32)
        steps = max(1, (n_cols - 1).bit_length()) + 1
        _, pos = lax.fori_loop(0, steps, pos_bisect, (plo0, phi0))

        def drop(kt, carry):
            t = sc_scr[kt]
            sc_scr[kt] = jnp.where((t == thr) & (pos_of(kt) > pos), NEG_INF, t)
            return carry

        for_tiles(drop, 0)


def _softmax_step(lg, m_ref, l_ref, acc_ref, pv_fn):
    m_prev = m_ref[0:1, :]
    m_new = jnp.maximum(m_prev, jnp.max(lg, axis=0, keepdims=True))
    m_safe = jnp.where(m_new == NEG_INF, 0.0, m_new)
    alpha = jnp.exp2(m_prev - m_safe)
    p = jnp.exp2(lg - m_safe)
    l_new = alpha * l_ref[0:1, :] + jnp.sum(p, axis=0, keepdims=True)
    acc_ref[...] = acc_ref[...] * alpha + pv_fn(p.astype(BF16))
    m_ref[...] = jnp.broadcast_to(m_new, m_ref.shape)
    l_ref[...] = jnp.broadcast_to(l_new, l_ref.shape)


def _attend_prompt_kernel(qi_ref, aux_ref, q_ref, ga_ref, kpad_ref, k_ref, vt_ref, z_ref, sc_scr,
                          hi_scr, lo_scr, thr_scr, m_scr, l_scr, acc_scr, lg_scr, *, tq, tk, n_cols):
    j = pl.program_id(1)
    n_tiles = j + 1

    def for_tiles(fn, init, unroll=2, halve=False):
        shift = unroll.bit_length() - 1
        assert unroll == 1 << shift
        done, carry = 0, init
        for size in ([unroll >> s for s in range(shift + 1)] if halve else [unroll, 1]):
            n_groups = (n_tiles - done) // size

            def body(g, carry, size=size, done=done):
                for u in range(size):
                    carry = fn(done + g * size + u, carry)
                return carry

            carry = lax.fori_loop(0, n_groups, body, carry)
            done = done + n_groups * size
        return carry

    assert tq == tk == TOPK_MAX and tq % CHUNK == 0
    qi = qi_ref[...]
    qstack = jnp.concatenate([qi[:, u * QUAD_LANES:(u + 1) * QUAD_LANES] for u in range(N_QUADS)], axis=0)
    w_t = aux_ref[...].T
    w_rows = [w_t[IDX_DIM + h:IDX_DIM + h + 1, :] for h in range(N_IDX_HEADS)]
    key_chunk = lax.broadcasted_iota(I32, (tk, tq), 0) // CHUNK
    qry_chunk = lax.broadcasted_iota(I32, (tk, tq), 1) // CHUNK
    diag_adm = key_chunk <= qry_chunk

    def idx_tile(kt, carry):
        start = pl.multiple_of(kt * tk, tk)
        acc = jnp.zeros((tk, tq), F32)
        for c in range(HEADS_PER_QUAD):
            kp = kpad_ref[pl.ds(start, tk), c * QUAD_LANES:(c + 1) * QUAD_LANES]
            s = _dot_nt(kp, qstack)
            for u in range(N_QUADS):
                acc = acc + jnp.maximum(s[:, u * tq:(u + 1) * tq], 0.0) * w_rows[HEADS_PER_QUAD * u + c]
        _store_scores(acc * IDX_SCALE, (diag_adm & (kt == j)) | (kt < j), sc_scr, hi_scr, lo_scr, kt)
        return carry

    for_tiles(idx_tile, 0, IDX_UNROLL, halve=True)

    thr_scr[...] = jnp.full((SUBLANES, tq), F32_LOWEST, F32)

    @pl.when(j >= 1)
    def _():
        _topk_threshold(for_tiles, sc_scr, hi_scr, lo_scr, thr_scr, tk=tk, width=tq, n_cols=n_cols)

    thr = thr_scr[0:1, :]

    q = q_ref[...]
    qn = [jnp.concatenate([q[:, (KV_GROUP * n + g) * HEAD_DIM:(KV_GROUP * n + g + 1) * HEAD_DIM]
                           for g in range(KV_GROUP)], axis=0) for n in range(N_KV_HEADS)]
    m_scr[...] = jnp.full(m_scr.shape, NEG_INF, F32)
    l_scr[...] = jnp.zeros(l_scr.shape, F32)
    acc_scr[...] = jnp.zeros(acc_scr.shape, F32)

    def att_tiles(kts):
        for slot, kt in enumerate(kts):
            start = pl.multiple_of(kt * tk, tk)
            for n in range(N_KV_HEADS):
                lg_scr[slot, n] = _dot_nt(k_ref[pl.ds(start, tk), n * HEAD_DIM:(n + 1) * HEAD_DIM], qn[n])
        for slot, kt in enumerate(kts):
            sel = sc_scr[kt] >= thr
            for n in range(N_KV_HEADS):
                lg = jnp.concatenate([jnp.where(sel, lg_scr[slot, n, :, g * tq:(g + 1) * tq], NEG_INF)
                                      for g in range(KV_GROUP)], axis=1)
                vt_n = vt_ref[kt, n * HEAD_DIM:(n + 1) * HEAD_DIM, :]
                _softmax_step(lg, m_scr.at[n], l_scr.at[n], acc_scr.at[n],
                              lambda p: jnp.dot(vt_n, p, preferred_element_type=F32))

    def att_group(size):
        def body(i, carry):
            att_tiles(tuple(i + u for u in range(size)))
            return carry

        return body

    done = 0
    size = ATT_GROUP
    while size >= 1:
        n_groups = (n_tiles - done) // size
        lax.fori_loop(0, n_groups, lambda g, c, size=size, done=done: att_group(size)(done + g * size, c), 0)
        done = done + n_groups * size
        size //= 2

    outs = []
    for n in range(N_KV_HEADS):
        o = acc_scr[n] / l_scr[n][0:1, :]
        outs.extend(o[:, g * tq:(g + 1) * tq].T for g in range(KV_GROUP))
    z_ref[...] = (jnp.concatenate(outs, axis=1) * _silu(ga_ref[...])).astype(BF16)


def _attend_prompt(qi, aux, q, ga, kpad, kb, vt, tq):
    nb, t, _ = q.shape
    tk = tq
    n_tiles = t // tk
    lanes_q = KV_GROUP * tq
    blk = lambda n: pl.BlockSpec((None, tq, n), lambda b, j: (b, j, 0))
    seq = lambda n: pl.BlockSpec((None, t, n), lambda b, j: (b, 0, 0))
    scratch = [pltpu.VMEM((n_tiles, tk, tq), F32), pltpu.VMEM((n_tiles, tk, tq), I16),
               pltpu.VMEM((n_tiles, tk, tq), I16), pltpu.VMEM((SUBLANES, tq), F32),
               pltpu.VMEM((N_KV_HEADS, SUBLANES, lanes_q), F32),
               pltpu.VMEM((N_KV_HEADS, SUBLANES, lanes_q), F32),
               pltpu.VMEM((N_KV_HEADS, HEAD_DIM, lanes_q), F32),
               pltpu.VMEM((ATT_GROUP, N_KV_HEADS, tk, lanes_q), F32)]
    return pl.pallas_call(
        functools.partial(_attend_prompt_kernel, tq=tq, tk=tk, n_cols=t),
        grid=(nb, t // tq),
        in_specs=[blk(N_IDX_HEADS * IDX_DIM), blk(LANES), blk(D_ATTN), blk(D_ATTN),
                  seq(N_QUADS * QUAD_LANES), seq(KV_DIM),
                  pl.BlockSpec((None, n_tiles, KV_DIM, tk), lambda b, j: (b, 0, 0, 0))],
        out_specs=blk(D_ATTN),
        out_shape=jax.ShapeDtypeStruct((nb, t, D_ATTN), BF16),
        scratch_shapes=scratch,
        compiler_params=pltpu.CompilerParams(dimension_semantics=("arbitrary", "arbitrary"),
                                             vmem_limit_bytes=VMEM_LIMIT_BYTES),
        name="attend_prompt",
    )(qi, aux, q, ga, kpad, kb, vt)


def _attend_sample_kernel(qi_ref, aux_ref, q_ref, ga_ref, cikt_ref, ck_ref, cv_ref, nk_ref, nv_ref,
                          z_ref, kit_s, k_s, vt_s, sc_scr, hi_scr, lo_scr, thr_scr, m_scr, l_scr, acc_scr,
                          *, tq, tk, past, n_tiles):
    n_keys = past + tq
    hl = N_IDX_HEADS * tq
    streams = range(SAMPLE_STREAMS)
    width = SAMPLE_STREAMS * LANES
    assert N_HEADS * tq == LANES and hl == 2 * LANES and past % tk == 0 and tq <= tk
    assert n_keys > TOPK_MAX and past % CHUNK == 0 and tq <= CHUNK

    def for_tiles(fn, init, unroll=None):
        for kt in range(n_tiles):
            init = fn(kt, init)
        return init

    for s in streams:
        lanes = slice(s * LANES, (s + 1) * LANES)
        aux = aux_ref[s]
        kit_s[s, :, 0:past] = cikt_ref[s].astype(BF16)
        kit_s[s, :, past:] = jnp.zeros((IDX_DIM, n_tiles * tk - past), BF16)
        kit_s[s, :, past:past + tq] = aux.T[:IDX_DIM, :].astype(BF16)
        k_s[s, past + tq:, :] = jnp.zeros((n_tiles * tk - n_keys, KV_DIM), BF16)
        vt_s[s, n_tiles - 1] = jnp.zeros((KV_DIM, tk), BF16)
        for n in range(N_KV_HEADS):
            cols = slice(n * HEAD_DIM, (n + 1) * HEAD_DIM)
            k_s[s, 0:past, cols] = ck_ref[s, pl.ds(n, past, stride=N_KV_HEADS), :].astype(BF16)
            k_s[s, past:past + tq, cols] = nk_ref[s, pl.ds(n, tq, stride=N_KV_HEADS), :].astype(BF16)
            for kt in range(past // tk):
                v_tile = cv_ref[s, pl.ds(N_KV_HEADS * kt * tk + n, tk, stride=N_KV_HEADS), :]
                vt_s[s, kt, cols, :] = v_tile.T.astype(BF16)
            vt_s[s, n_tiles - 1, cols, 0:tq] = nv_ref[s, pl.ds(n, tq, stride=N_KV_HEADS), :].T.astype(BF16)

        qi = qi_ref[s]
        qrows = jnp.concatenate([qi[:, h * IDX_DIM:(h + 1) * IDX_DIM] for h in range(N_IDX_HEADS)], axis=0)
        w_rows = jnp.concatenate([aux[:, IDX_DIM + h:IDX_DIM + h + 1] for h in range(N_IDX_HEADS)], axis=0)
        w_rows = jnp.broadcast_to(w_rows, (hl, LANES))

        def idx_tile(kt, carry):
            d = jnp.dot(qrows, kit_s[s, :, kt * tk:(kt + 1) * tk], preferred_element_type=F32)
            y = jnp.concatenate([jnp.maximum(d[:, c:c + LANES], 0.0) * w_rows for c in range(0, tk, LANES)],
                                axis=1)
            per_query = _vreg_sum(y, tq) * IDX_SCALE
            score = jnp.concatenate([per_query] * N_HEADS, axis=0).T
            pos = kt * tk + lax.broadcasted_iota(I32, (tk, LANES), 0)
            _store_scores(score, pos < n_keys, sc_scr.at[:, :, lanes], hi_scr.at[:, :, lanes],
                          lo_scr.at[:, :, lanes], kt)
            return carry

        for_tiles(idx_tile, 0)

    _topk_threshold(for_tiles, sc_scr, hi_scr, lo_scr, thr_scr, tk=tk, width=width, n_cols=n_tiles * tk)
    thr = thr_scr[0:1, :]

    zeros = jnp.zeros((tq, HEAD_DIM), BF16)
    m_scr[...] = jnp.full(m_scr.shape, NEG_INF, F32)
    l_scr[...] = jnp.zeros(l_scr.shape, F32)
    acc_scr[...] = jnp.zeros(acc_scr.shape, F32)
    for s in streams:
        lanes = slice(s * LANES, (s + 1) * LANES)
        q = q_ref[s]
        qblk = jnp.concatenate(
            [jnp.concatenate([q[:, (KV_GROUP * n + g) * HEAD_DIM:(KV_GROUP * n + g + 1) * HEAD_DIM]
                              if m == n else zeros for m in range(N_KV_HEADS)], axis=1)
             for n in range(N_KV_HEADS) for g in range(KV_GROUP)], axis=0)

        def att_tile(kt, carry):
            lg = jnp.where(sc_scr[kt, :, lanes] >= thr[:, lanes],
                           _dot_nt(k_s[s, kt * tk:(kt + 1) * tk, :], qblk), NEG_INF)
            _softmax_step(lg, m_scr.at[:, lanes], l_scr.at[:, lanes], acc_scr.at[:, lanes],
                          lambda p: jnp.dot(vt_s[s, kt], p, preferred_element_type=F32))
            return carry

        for_tiles(att_tile, 0)

        o_t = (acc_scr[:, lanes] / l_scr[0:1, lanes]).T
        outs = [o_t[(KV_GROUP * n + g) * tq:(KV_GROUP * n + g + 1) * tq, n * HEAD_DIM:(n + 1) * HEAD_DIM]
                for n in range(N_KV_HEADS) for g in range(KV_GROUP)]
        z_ref[s] = (jnp.concatenate(outs, axis=1) * _silu(ga_ref[s])).astype(BF16)


def _attend_sample(qi, aux, q, ga, cache_ik, cache_k, cache_v, new_k, new_v, tk):
    nseq, tq, _ = q.shape
    past = cache_ik.shape[1]
    assert nseq % SAMPLE_STREAMS == 0
    cache_ikt = jnp.transpose(cache_ik, (0, 2, 1))
    n_tiles = -(-(past + tq) // tk)
    width = SAMPLE_STREAMS * LANES
    blk = lambda n, m=1: pl.BlockSpec((SAMPLE_STREAMS, m * tq, n), lambda b: (b, 0, 0))
    cache = lambda n, m=1: pl.BlockSpec((SAMPLE_STREAMS, m * past, n), lambda b: (b, 0, 0))
    scratch = [pltpu.VMEM((SAMPLE_STREAMS, IDX_DIM, n_tiles * tk), BF16),
               pltpu.VMEM((SAMPLE_STREAMS, n_tiles * tk, KV_DIM), BF16),
               pltpu.VMEM((SAMPLE_STREAMS, n_tiles, KV_DIM, tk), BF16),
               pltpu.VMEM((n_tiles, tk, width), F32), pltpu.VMEM((n_tiles, tk, width), I16),
               pltpu.VMEM((n_tiles, tk, width), I16),
               pltpu.VMEM((SUBLANES, width), F32),
               pltpu.VMEM((SUBLANES, width), F32), pltpu.VMEM((SUBLANES, width), F32),
               pltpu.VMEM((KV_DIM, width), F32)]
    return pl.pallas_call(
        functools.partial(_attend_sample_kernel, tq=tq, tk=tk, past=past, n_tiles=n_tiles),
        grid=(nseq // SAMPLE_STREAMS,),
        in_specs=[blk(N_IDX_HEADS * IDX_DIM), blk(LANES), blk(D_ATTN), blk(D_ATTN),
                  pl.BlockSpec((SAMPLE_STREAMS, IDX_DIM, past), lambda b: (b, 0, 0)),
                  cache(HEAD_DIM, N_KV_HEADS), cache(HEAD_DIM, N_KV_HEADS), blk(HEAD_DIM, N_KV_HEADS),
                  blk(HEAD_DIM, N_KV_HEADS)],
        out_specs=blk(D_ATTN),
        out_shape=jax.ShapeDtypeStruct((nseq, tq, D_ATTN), BF16),
        scratch_shapes=scratch,
        compiler_params=pltpu.CompilerParams(dimension_semantics=("arbitrary",),
                                             vmem_limit_bytes=VMEM_LIMIT_BYTES),
        name="attend_sample",
    )(qi, aux, q, ga, cache_ikt, cache_k, cache_v, new_k, new_v)


def _merge_kernel(x_ref, za_ref, zc_ref, w_ref, g_ref, y_ref):
    z = jnp.concatenate([za_ref[...], zc_ref[...]], axis=1)
    y = jnp.dot(z, w_ref[...], preferred_element_type=F32)
    y_ref[...] = x_ref[...] + _rmsnorm(y, g_ref[...])


def _merge(x, za, zc, w_out, g_post, tm):
    rows = x.shape[0]
    row_spec = lambda n: pl.BlockSpec((tm, n), lambda i: (i, 0))
    return pl.pallas_call(
        _merge_kernel,
        grid=(rows // tm,),
        in_specs=[row_spec(D_MODEL), row_spec(D_ATTN), row_spec(D_CONV), _resident((D_MODEL, D_MODEL), 1),
                  _resident((1, D_MODEL), 1)],
        out_specs=row_spec(D_MODEL),
        out_shape=jax.ShapeDtypeStruct((rows, D_MODEL), F32),
        compiler_params=pltpu.CompilerParams(dimension_semantics=("arbitrary",),
                                             vmem_limit_bytes=VMEM_LIMIT_BYTES),
        name="merge",
    )(x, za, zc, w_out, g_post)


PROJECT_ROWS = 256
ATTEND_ROWS = 256
MERGE_ROWS = 512


def _layer(xp, xs, cache_k, cache_v, cache_ik, state, g_pre, w_in, w_conv, w_out, g_post):
    nb, t, _ = xp.shape
    nseq, seqlen, _ = xs.shape
    g_pre = g_pre.reshape(1, D_MODEL)
    g_post = g_post.reshape(1, D_MODEL)
    w_t = w_in.T.astype(BF16)
    w_out = w_out.astype(BF16)

    assert PROJECT_ROWS == ATTEND_ROWS
    q, k, v, ga, qi, ik, aux, zc, kpad, kb, vt, cs = _project_prompt(xp, g_pre, w_t, w_conv, PROJECT_ROWS)
    za = _attend_prompt(qi, aux, q, ga, kpad, kb, vt, ATTEND_ROWS)
    yp = _merge(xp.reshape(nb * t, D_MODEL), za.reshape(nb * t, D_ATTN), zc.reshape(nb * t, D_CONV),
                w_out, g_post, MERGE_ROWS).reshape(nb, t, D_MODEL)

    sq, sk, sv, sga, sqi, sik, saux, szc, scs = _project_sample(xs, g_pre, w_t, w_conv, state)
    per_seq = lambda a: a.reshape(nseq, -1, a.shape[-1])
    kv_rows = lambda a: a.reshape(nseq, -1, HEAD_DIM)
    sza = _attend_sample(per_seq(sqi), per_seq(saux), per_seq(sq), per_seq(sga), cache_ik,
                         kv_rows(cache_k), kv_rows(cache_v), per_seq(sk), per_seq(sv), ATTEND_ROWS)
    ys = _merge(xs.reshape(nseq * seqlen, D_MODEL), sza.reshape(nseq * seqlen, D_ATTN), szc, w_out,
                g_post, MERGE_ROWS).reshape(nseq, seqlen, D_MODEL)

    heads = lambda a, lead: a.reshape(lead + (N_KV_HEADS, HEAD_DIM))
    return (yp, ys, heads(k, (nb, t)), heads(v, (nb, t)), jnp.transpose(ik, (0, 2, 1)), cs,
            heads(sk, (nseq, seqlen)), heads(sv, (nseq, seqlen)), per_seq(sik), scs)


def kernel(x_prompt, x_sample, cache_k, cache_v, cache_idx_k, state_conv, g_pre, w_in, w_conv, w_out,
           g_post):
    depth = g_pre.shape[0]
    xp, xs = x_prompt, x_sample
    outs = []
    for l in range(depth):
        res = _layer(xp, xs, cache_k[l], cache_v[l], cache_idx_k[l], state_conv[l], g_pre[l], w_in[l],
                     w_conv[l], w_out[l], g_post[l])
        xp, xs = res[0], res[1]
        outs.append(res[2:])
    stacked = [jnp.stack([o[i] for o in outs]) for i in range(8)]
    return (xp, xs) + tuple(stacked)
```

```python
import functools

import jax
import jax.numpy as jnp
from jax import lax
from jax.experimental import pallas as pl
from jax.experimental.pallas import tpu as pltpu

F32 = jnp.float32
BF16 = jnp.bfloat16
I32 = jnp.int32
I16 = jnp.int16

D_MODEL = 2048
D_ATTN = 1024
D_CONV = 1024
HEAD_DIM = 128
N_KV_HEADS = 2
KV_GROUP = 4
N_HEADS = N_KV_HEADS * KV_GROUP
KV_DIM = N_KV_HEADS * HEAD_DIM
N_IDX_HEADS = 16
IDX_DIM = 64
TOPK_MAX = 256
CHUNK = 64
CONV_WIDTH = 3
RMS_EPS = 1e-6

LANES = 128
SUBLANES = 8
I16_ROWS = 2 * SUBLANES
VMEM_LIMIT_BYTES = 60 * 1000 * 1024

OFF_Q = 0
OFF_K = OFF_Q + D_ATTN
OFF_V = OFF_K + KV_DIM
OFF_GA = OFF_V + KV_DIM
OFF_QI = OFF_GA + D_ATTN
OFF_KW = OFF_QI + N_IDX_HEADS * IDX_DIM
OFF_B = OFF_KW + IDX_DIM + N_IDX_HEADS
OFF_C = OFF_B + D_CONV
OFF_HC = OFF_C + D_CONV
OFF_GB = OFF_HC + D_CONV
D_PROJ = OFF_GB + D_CONV
assert all(o % I16_ROWS == 0 for o in (OFF_K, OFF_V, OFF_GA, OFF_QI, OFF_KW, OFF_B, OFF_C, OFF_HC, OFF_GB))

CONV_COLS = 256
PAD_ROWS = SUBLANES
HEADS_PER_QUAD = 4
QUAD_LANES = HEADS_PER_QUAD * IDX_DIM
N_QUADS = N_IDX_HEADS // HEADS_PER_QUAD
IDX_SCALE = (IDX_DIM ** -0.5) * (N_IDX_HEADS ** -0.5)
LOG2E = 1.4426950408889634
ATT_SCALE_LOG2 = HEAD_DIM ** -0.5 * LOG2E
INT_MAX = 2 ** 31 - 1
HALF_BITS = 16
HALF_MASK = 2 ** HALF_BITS - 1
HALF_MIN = -(2 ** (HALF_BITS - 1))
HALF_MAX = 2 ** (HALF_BITS - 1) - 1
KEY_POS_INF = 0x7F800000
KEY_NEG_INF = -KEY_POS_INF - 1
MAX_SEARCH_STEPS = 36
F32_LOWEST = -3.4028234663852886e38
COUNT_UNROLL = 4
ATT_GROUP = 4
IDX_UNROLL = 8
SAMPLE_STREAMS = 2
NEG_INF = float("-inf")


def _silu(x):
    return x * jax.nn.sigmoid(x)


def _dot_nt(a, b):
    return lax.dot_general(a, b, (((1,), (1,)), ((), ())), preferred_element_type=F32)


def _rmsnorm(x, g):
    ms = jnp.mean(x * x, axis=-1, keepdims=True)
    return x * lax.rsqrt(ms + RMS_EPS) * g


def _project_rows(h, w_ref, q_ref, k_ref, v_ref, ga_ref, qi_ref, ik_ref, aux_ref):
    rows = h.shape[0]

    def mm(r0, n):
        return _dot_nt(h, w_ref[r0:r0 + n, :])

    q_ref[...] = (mm(OFF_Q, D_ATTN) * ATT_SCALE_LOG2).astype(BF16)
    kk = mm(OFF_K, KV_DIM)
    vv = mm(OFF_V, KV_DIM)
    for n in range(N_KV_HEADS):
        k_ref[pl.ds(n, rows, stride=N_KV_HEADS), :] = kk[:, n * HEAD_DIM:(n + 1) * HEAD_DIM]
        v_ref[pl.ds(n, rows, stride=N_KV_HEADS), :] = vv[:, n * HEAD_DIM:(n + 1) * HEAD_DIM]
    ga_ref[...] = mm(OFF_GA, D_ATTN)
    qi_ref[...] = mm(OFF_QI, N_IDX_HEADS * IDX_DIM).astype(BF16)
    kw = mm(OFF_KW, LANES)
    aux_ref[...] = kw
    if ik_ref.shape[0] == IDX_DIM:
        ik_ref[...] = kw.T[:IDX_DIM, :]
    else:
        ik_ref[...] = kw[:, :IDX_DIM]
    return kk, vv, kw


def _conv_chunk(h, w_ref, c):
    def mm(off):
        return _dot_nt(h, w_ref[off + c:off + c + CONV_COLS, :])

    return mm(OFF_B), mm(OFF_C) * mm(OFF_HC), mm(OFF_GB)


def _conv_out(bg, gb, u, um1, um2, wc_ref, c):
    w0 = wc_ref[0:1, c:c + CONV_COLS]
    w1 = wc_ref[1:2, c:c + CONV_COLS]
    w2 = wc_ref[2:3, c:c + CONV_COLS]
    conv = w0 * um2 + w1 * um1 + w2 * u
    return (bg * conv * _silu(gb)).astype(BF16)


def _project_prompt_kernel(x_ref, g_ref, w_ref, wc_ref, q_ref, k_ref, v_ref, ga_ref, qi_ref, ik_ref,
                           aux_ref, zc_ref, kpad_ref, kb_ref, vt_ref, cs_ref, upad_ref, *, tm):
    @pl.when(pl.program_id(1) == 0)
    def _():
        upad_ref[0:PAD_ROWS, :] = jnp.zeros((PAD_ROWS, D_CONV), F32)

    h = _rmsnorm(x_ref[...], g_ref[...]).astype(BF16)
    kk, vv, kw = _project_rows(h, w_ref, q_ref, k_ref, v_ref, ga_ref, qi_ref, ik_ref, aux_ref)
    kb_ref[...] = kk.astype(BF16)
    vt_ref[...] = vv.T.astype(BF16)
    lane = lax.broadcasted_iota(I32, kw.shape, 1)
    lo = jnp.where(lane < IDX_DIM, kw, 0.0)
    hi = pltpu.roll(lo, IDX_DIM, axis=1)
    zero = jnp.zeros_like(lo)
    for i, blk in enumerate((lo, zero, hi, zero, zero, lo, zero, hi)):
        kpad_ref[:, i * LANES:(i + 1) * LANES] = blk.astype(BF16)

    for c in range(0, D_CONV, CONV_COLS):
        bg, u, gb = _conv_chunk(h, w_ref, c)
        upad_ref[PAD_ROWS:PAD_ROWS + tm, c:c + CONV_COLS] = u
        um1 = upad_ref[PAD_ROWS - 1:PAD_ROWS - 1 + tm, c:c + CONV_COLS]
        um2 = upad_ref[PAD_ROWS - 2:PAD_ROWS - 2 + tm, c:c + CONV_COLS]
        zc_ref[:, c:c + CONV_COLS] = _conv_out(bg, gb, u, um1, um2, wc_ref, c)
    last = upad_ref[PAD_ROWS + tm - (CONV_WIDTH - 1):PAD_ROWS + tm, :]
    cs_ref[...] = last
    upad_ref[PAD_ROWS - (CONV_WIDTH - 1):PAD_ROWS, :] = last


def _project_sample_kernel(x_ref, g_ref, w_ref, wc_ref, st_ref, q_ref, k_ref, v_ref, ga_ref, qi_ref,
                           ik_ref, aux_ref, zc_ref, cs_ref, upad_ref, *, nseq, seqlen):
    rows = nseq * seqlen
    upad_ref[:, PAD_ROWS - (CONV_WIDTH - 1):PAD_ROWS, :] = st_ref[...]
    h = _rmsnorm(x_ref[...], g_ref[...]).astype(BF16)
    _project_rows(h, w_ref, q_ref, k_ref, v_ref, ga_ref, qi_ref, ik_ref, aux_ref)
    for c in range(0, D_CONV, CONV_COLS):
        bg, u, gb = _conv_chunk(h, w_ref, c)
        upad_ref[:, PAD_ROWS:PAD_ROWS + seqlen, c:c + CONV_COLS] = u.reshape(nseq, seqlen, CONV_COLS)
        um1 = upad_ref[:, PAD_ROWS - 1:PAD_ROWS - 1 + seqlen, c:c + CONV_COLS].reshape(rows, CONV_COLS)
        um2 = upad_ref[:, PAD_ROWS - 2:PAD_ROWS - 2 + seqlen, c:c + CONV_COLS].reshape(rows, CONV_COLS)
        zc_ref[:, c:c + CONV_COLS] = _conv_out(bg, gb, u, um1, um2, wc_ref, c)
    cs_ref[...] = upad_ref[:, PAD_ROWS + seqlen - (CONV_WIDTH - 1):PAD_ROWS + seqlen, :]


_PROJECT_OUTS = ((1, D_ATTN, BF16), (N_KV_HEADS, HEAD_DIM, F32), (N_KV_HEADS, HEAD_DIM, F32),
                 (1, D_ATTN, F32), (1, N_IDX_HEADS * IDX_DIM, BF16), (1, IDX_DIM, F32), (1, LANES, F32),
                 (1, D_CONV, BF16))
_IK_OUT = 5


def _project_out_shapes(lead, rows):
    return [jax.ShapeDtypeStruct(lead + (m * rows, n), dt) for m, n, dt in _PROJECT_OUTS]


def _resident(shape, ngrid):
    zeros = (0,) * len(shape)
    if ngrid == 1:
        return pl.BlockSpec(shape, lambda i: zeros, pipeline_mode=pl.Buffered(1))
    return pl.BlockSpec(shape, lambda b, i: zeros, pipeline_mode=pl.Buffered(1))


def _project_prompt(x, g_pre, w_t, w_conv, tm):
    nb, t, _ = x.shape
    grid = (nb, t // tm)
    row_spec = lambda n, m=1: pl.BlockSpec((None, m * tm, n), lambda b, i: (b, i, 0))
    out_shapes = _project_out_shapes((nb,), t) + [
        jax.ShapeDtypeStruct((nb, t, N_QUADS * QUAD_LANES), BF16),
        jax.ShapeDtypeStruct((nb, t, KV_DIM), BF16),
        jax.ShapeDtypeStruct((nb, t // tm, KV_DIM, tm), BF16),
        jax.ShapeDtypeStruct((nb, CONV_WIDTH - 1, D_CONV), F32)]
    out_specs = [row_spec(n, m) for m, n, _ in _PROJECT_OUTS]
    out_shapes[_IK_OUT] = jax.ShapeDtypeStruct((nb, IDX_DIM, t), F32)
    out_specs[_IK_OUT] = pl.BlockSpec((None, IDX_DIM, tm), lambda b, i: (b, 0, i))
    out_specs += [row_spec(N_QUADS * QUAD_LANES), row_spec(KV_DIM),
                  pl.BlockSpec((None, None, KV_DIM, tm), lambda b, i: (b, i, 0, 0)),
                  pl.BlockSpec((None, CONV_WIDTH - 1, D_CONV), lambda b, i: (b, 0, 0))]
    return pl.pallas_call(
        functools.partial(_project_prompt_kernel, tm=tm),
        grid=grid,
        in_specs=[row_spec(D_MODEL), _resident((1, D_MODEL), 2), _resident((D_PROJ, D_MODEL), 2),
                  _resident((CONV_WIDTH, D_CONV), 2)],
        out_specs=out_specs,
        out_shape=out_shapes,
        scratch_shapes=[pltpu.VMEM((PAD_ROWS + tm, D_CONV), F32)],
        compiler_params=pltpu.CompilerParams(dimension_semantics=("arbitrary", "arbitrary"),
                                             vmem_limit_bytes=VMEM_LIMIT_BYTES),
        name="project_prompt",
    )(x, g_pre, w_t, w_conv)


def _project_sample(x, g_pre, w_t, w_conv, state):
    nseq, seqlen, _ = x.shape
    rows = nseq * seqlen
    full = lambda shape: pl.BlockSpec(shape, lambda i: (0,) * len(shape))
    out_shapes = _project_out_shapes((), rows) + [jax.ShapeDtypeStruct((nseq, CONV_WIDTH - 1, D_CONV), F32)]
    out_specs = [full((m * rows, n)) for m, n, _ in _PROJECT_OUTS] + [full((nseq, CONV_WIDTH - 1, D_CONV))]
    return pl.pallas_call(
        functools.partial(_project_sample_kernel, nseq=nseq, seqlen=seqlen),
        grid=(1,),
        in_specs=[full((rows, D_MODEL)), full((1, D_MODEL)), _resident((D_PROJ, D_MODEL), 1),
                  full((CONV_WIDTH, D_CONV)), full((nseq, CONV_WIDTH - 1, D_CONV))],
        out_specs=out_specs,
        out_shape=out_shapes,
        scratch_shapes=[pltpu.VMEM((nseq, PAD_ROWS + seqlen, D_CONV), F32)],
        compiler_params=pltpu.CompilerParams(dimension_semantics=("arbitrary",),
                                             vmem_limit_bytes=VMEM_LIMIT_BYTES),
        name="project_sample",
    )(x.reshape(rows, D_MODEL), g_pre, w_t, w_conv, state)


def _key_of_score(x):
    bits = lax.bitcast_convert_type(x, I32)
    return bits ^ ((bits >> 31) & INT_MAX)


def _score_of_key(key):
    return lax.bitcast_convert_type(key ^ ((key >> 31) & INT_MAX), F32)


def _store_scores(score, admissible, sc_scr, hi_scr, lo_scr, kt):
    score = jnp.where(admissible, score, NEG_INF)
    sc_scr[kt] = score
    key = _key_of_score(score)
    hi_scr[kt] = (key >> HALF_BITS).astype(I16)
    lo_scr[kt] = key.astype(I16) ^ jnp.int16(HALF_MIN)


def _vreg_sum(p, rows):
    parts = [p[r:r + rows, :] for r in range(0, p.shape[0], rows)]
    while len(parts) > 1:
        parts = [a + b for a, b in zip(parts[::2], parts[1::2])] + parts[len(parts) & ~1:]
    return parts[0]


def _threshold_key_guess(for_tiles, hi_scr, lo_scr, width):
    def count_half(ref, th):
        def body(kt, c16):
            return c16 + _vreg_sum(jnp.where(ref[kt] >= th, jnp.int16(1), jnp.int16(0)), I16_ROWS)

        c16 = for_tiles(body, jnp.zeros((I16_ROWS, width), I16), COUNT_UNROLL)
        return jnp.sum(c16.astype(I32), axis=0, keepdims=True)

    def bisect_half(ref):
        def step(_, carry):
            lo, hi = carry
            mid = (lo + hi + 1) >> 1
            cnt = count_half(ref, jnp.minimum(mid, HALF_MAX).astype(I16))
            ge = (cnt >= TOPK_MAX) & (mid <= HALF_MAX)
            return jnp.where(ge, mid, lo), jnp.where(ge, hi, mid)

        lo0 = jnp.full((1, width), HALF_MIN, I32)
        hi0 = jnp.full((1, width), HALF_MAX + 1, I32)
        return lax.fori_loop(0, HALF_BITS, step, (lo0, hi0))[0]

    top = bisect_half(hi_scr)
    top16 = top.astype(I16)

    def narrow(kt, carry):
        h = hi_scr[kt]
        lo_scr[kt] = jnp.where(h == top16, lo_scr[kt],
                               jnp.where(h > top16, jnp.int16(HALF_MAX), jnp.int16(HALF_MIN)))
        return carry

    for_tiles(narrow, 0)
    bot = bisect_half(lo_scr)
    return top * (HALF_MASK + 1) + (bot - HALF_MIN)


def _topk_threshold(for_tiles, sc_scr, hi_scr, lo_scr, thr_scr, *, tk, width, n_cols):
    def count(pred_fn):
        def body(kt, c8):
            return c8 + _vreg_sum(pred_fn(kt, sc_scr[kt]).astype(I32), SUBLANES)

        c8 = for_tiles(body, jnp.zeros((SUBLANES, width), I32), COUNT_UNROLL)
        return jnp.sum(c8, axis=0, keepdims=True)

    hint_lo = _threshold_key_guess(for_tiles, hi_scr, lo_scr, width)
    hint_hi = hint_lo + 1

    def open_lanes(lo, hi):
        return jnp.max((hi > lo + 1).astype(F32)) > 0.0

    def cond(state):
        i, lo, hi, _ = state
        return (i < MAX_SEARCH_STEPS) & open_lanes(lo, hi)

    def step(state):
        i, lo, hi, n_lo = state
        mid = (lo >> 1) + (hi >> 1) + ((lo | hi) & 1)
        cand = jnp.where(i == 0, hint_lo, jnp.where(i == 1, hint_hi, mid))
        cand = jnp.minimum(jnp.maximum(cand, lo + 1), hi)
        cand_score = _score_of_key(cand)
        cnt = count(lambda kt, t: t >= cand_score)
        ge = cnt >= TOPK_MAX
        hit = cnt == TOPK_MAX
        lo_new = jnp.where(ge, cand, lo)
        hi_new = jnp.where(hit, cand + 1, jnp.where(ge, hi, cand))
        return i + 1, lo_new, hi_new, jnp.where(ge, cnt, n_lo)

    lo0 = jnp.full((1, width), KEY_NEG_INF, I32)
    hi0 = jnp.full((1, width), KEY_POS_INF + 1, I32)
    n0 = jnp.full((1, width), INT_MAX, I32)
    _, lo, _, n_lo = lax.while_loop(cond, step, (jnp.int32(0), lo0, hi0, n0))
    thr = _score_of_key(lo)
    thr_scr[...] = jnp.broadcast_to(thr, (SUBLANES, width))

    @pl.when(jnp.max((n_lo > TOPK_MAX).astype(F32)) > 0.0)
    def _():
        need = TOPK_MAX - count(lambda kt, t: t > thr)

        def pos_of(kt):
            return kt * tk + lax.broadcasted_iota(I32, (tk, width), 0)

        def pos_bisect(_, carry):
            plo, phi = carry
            mid = (plo + phi) >> 1
            ok = count(lambda kt, t: (t == thr) & (pos_of(kt) <= mid)) >= need
            return jnp.where(ok, plo, mid), jnp.where(ok, mid, phi)

        plo0 = jnp.full((1, width), -1, I32)
        phi0 = jnp.full((1, width), n_cols - 1, I32)
        steps = max(1, (n_cols - 1).bit_length()) + 1
        _, pos = lax.fori_loop(0, steps, pos_bisect, (plo0, phi0))

        def drop(kt, carry):
            t = sc_scr[kt]
            sc_scr[kt] = jnp.where((t == thr) & (pos_of(kt) > pos), NEG_INF, t)
            return carry

        for_tiles(drop, 0)


def _softmax_step(lg, m_ref, l_ref, acc_ref, pv_fn):
    m_prev = m_ref[0:1, :]
    m_new = jnp.maximum(m_prev, jnp.max(lg, axis=0, keepdims=True))
    m_safe = jnp.where(m_new == NEG_INF, 0.0, m_new)
    alpha = jnp.exp2(m_prev - m_safe)
    p = jnp.exp2(lg - m_safe)
    l_new = alpha * l_ref[0:1, :] + jnp.sum(p, axis=0, keepdims=True)
    acc_ref[...] = acc_ref[...] * alpha + pv_fn(p.astype(BF16))
    m_ref[...] = jnp.broadcast_to(m_new, m_ref.shape)
    l_ref[...] = jnp.broadcast_to(l_new, l_ref.shape)


def _attend_prompt_kernel(qi_ref, aux_ref, q_ref, ga_ref, kpad_ref, k_ref, vt_ref, z_ref, sc_scr,
                          hi_scr, lo_scr, thr_scr, m_scr, l_scr, acc_scr, lg_scr, *, tq, tk, n_cols):
    j = pl.program_id(1)
    n_tiles = j + 1

    def for_tiles(fn, init, unroll=2, halve=False):
        shift = unroll.bit_length() - 1
        assert unroll == 1 << shift
        done, carry = 0, init
        for size in ([unroll >> s for s in range(shift + 1)] if halve else [unroll, 1]):
            n_groups = (n_tiles - done) // size

            def body(g, carry, size=size, done=done):
                for u in range(size):
                    carry = fn(done + g * size + u, carry)
                return carry

            carry = lax.fori_loop(0, n_groups, body, carry)
            done = done + n_groups * size
        return carry

    assert tq == tk == TOPK_MAX and tq % CHUNK == 0
    qi = qi_ref[...]
    qstack = jnp.concatenate([qi[:, u * QUAD_LANES:(u + 1) * QUAD_LANES] for u in range(N_QUADS)], axis=0)
    w_t = aux_ref[...].T
    w_rows = [w_t[IDX_DIM + h:IDX_DIM + h + 1, :] for h in range(N_IDX_HEADS)]
    key_chunk = lax.broadcasted_iota(I32, (tk, tq), 0) // CHUNK
    qry_chunk = lax.broadcasted_iota(I32, (tk, tq), 1) // CHUNK
    diag_adm = key_chunk <= qry_chunk

    def idx_tile(kt, carry):
        start = pl.multiple_of(kt * tk, tk)
        acc = jnp.zeros((tk, tq), F32)
        for c in range(HEADS_PER_QUAD):
            kp = kpad_ref[pl.ds(start, tk), c * QUAD_LANES:(c + 1) * QUAD_LANES]
            s = _dot_nt(kp, qstack)
            for u in range(N_QUADS):
                acc = acc + jnp.maximum(s[:, u * tq:(u + 1) * tq], 0.0) * w_rows[HEADS_PER_QUAD * u + c]
        _store_scores(acc * IDX_SCALE, (diag_adm & (kt == j)) | (kt < j), sc_scr, hi_scr, lo_scr, kt)
        return carry

    for_tiles(idx_tile, 0, IDX_UNROLL, halve=True)

    thr_scr[...] = jnp.full((SUBLANES, tq), F32_LOWEST, F32)

    @pl.when(j >= 1)
    def _():
        _topk_threshold(for_tiles, sc_scr, hi_scr, lo_scr, thr_scr, tk=tk, width=tq, n_cols=n_cols)

    thr = thr_scr[0:1, :]

    q = q_ref[...]
    qn = [jnp.concatenate([q[:, (KV_GROUP * n + g) * HEAD_DIM:(KV_GROUP * n + g + 1) * HEAD_DIM]
                           for g in range(KV_GROUP)], axis=0) for n in range(N_KV_HEADS)]
    m_scr[...] = jnp.full(m_scr.shape, NEG_INF, F32)
    l_scr[...] = jnp.zeros(l_scr.shape, F32)
    acc_scr[...] = jnp.zeros(acc_scr.shape, F32)

    def att_tiles(kts):
        for slot, kt in enumerate(kts):
            start = pl.multiple_of(kt * tk, tk)
            for n in range(N_KV_HEADS):
                lg_scr[slot, n] = _dot_nt(k_ref[pl.ds(start, tk), n * HEAD_DIM:(n + 1) * HEAD_DIM], qn[n])
        for slot, kt in enumerate(kts):
            sel = sc_scr[kt] >= thr
            for n in range(N_KV_HEADS):
                lg = jnp.concatenate([jnp.where(sel, lg_scr[slot, n, :, g * tq:(g + 1) * tq], NEG_INF)
                                      for g in range(KV_GROUP)], axis=1)
                vt_n = vt_ref[kt, n * HEAD_DIM:(n + 1) * HEAD_DIM, :]
                _softmax_step(lg, m_scr.at[n], l_scr.at[n], acc_scr.at[n],
                              lambda p: jnp.dot(vt_n, p, preferred_element_type=F32))

    def att_group(size):
        def body(i, carry):
            att_tiles(tuple(i + u for u in range(size)))
            return carry

        return body

    done = 0
    size = ATT_GROUP
    while size >= 1:
        n_groups = (n_tiles - done) // size
        lax.fori_loop(0, n_groups, lambda g, c, size=size, done=done: att_group(size)(done + g * size, c), 0)
        done = done + n_groups * size
        size //= 2

    outs = []
    for n in range(N_KV_HEADS):
        o = acc_scr[n] / l_scr[n][0:1, :]
        outs.extend(o[:, g * tq:(g + 1) * tq].T for g in range(KV_GROUP))
    z_ref[...] = (jnp.concatenate(outs, axis=1) * _silu(ga_ref[...])).astype(BF16)


def _attend_prompt(qi, aux, q, ga, kpad, kb, vt, tq):
    nb, t, _ = q.shape
    tk = tq
    n_tiles = t // tk
    lanes_q = KV_GROUP * tq
    blk = lambda n: pl.BlockSpec((None, tq, n), lambda b, j: (b, j, 0))
    seq = lambda n: pl.BlockSpec((None, t, n), lambda b, j: (b, 0, 0))
    scratch = [pltpu.VMEM((n_tiles, tk, tq), F32), pltpu.VMEM((n_tiles, tk, tq), I16),
               pltpu.VMEM((n_tiles, tk, tq), I16), pltpu.VMEM((SUBLANES, tq), F32),
               pltpu.VMEM((N_KV_HEADS, SUBLANES, lanes_q), F32),
               pltpu.VMEM((N_KV_HEADS, SUBLANES, lanes_q), F32),
               pltpu.VMEM((N_KV_HEADS, HEAD_DIM, lanes_q), F32),
               pltpu.VMEM((ATT_GROUP, N_KV_HEADS, tk, lanes_q), F32)]
    return pl.pallas_call(
        functools.partial(_attend_prompt_kernel, tq=tq, tk=tk, n_cols=t),
        grid=(nb, t // tq),
        in_specs=[blk(N_IDX_HEADS * IDX_DIM), blk(LANES), blk(D_ATTN), blk(D_ATTN),
                  seq(N_QUADS * QUAD_LANES), seq(KV_DIM),
                  pl.BlockSpec((None, n_tiles, KV_DIM, tk), lambda b, j: (b, 0, 0, 0))],
        out_specs=blk(D_ATTN),
        out_shape=jax.ShapeDtypeStruct((nb, t, D_ATTN), BF16),
        scratch_shapes=scratch,
        compiler_params=pltpu.CompilerParams(dimension_semantics=("arbitrary", "arbitrary"),
                                             vmem_limit_bytes=VMEM_LIMIT_BYTES),
        name="attend_prompt",
    )(qi, aux, q, ga, kpad, kb, vt)


def _attend_sample_kernel(qi_ref, aux_ref, q_ref, ga_ref, cikt_ref, ck_ref, cv_ref, nk_ref, nv_ref,
                          z_ref, kit_s, k_s, vt_s, sc_scr, hi_scr, lo_scr, thr_scr, lg_scr,
                          *, tq, tk, past, n_tiles):
    n_keys = past + tq
    hl = N_IDX_HEADS * tq
    streams = range(SAMPLE_STREAMS)
    width = SAMPLE_STREAMS * LANES
    assert N_HEADS * tq == LANES and hl == 2 * LANES and past % tk == 0 and tq <= tk
    assert n_keys > TOPK_MAX and past % CHUNK == 0 and tq <= CHUNK

    def for_tiles(fn, init, unroll=None):
        for kt in range(n_tiles):
            init = fn(kt, init)
        return init

    for s in streams:
        lanes = slice(s * LANES, (s + 1) * LANES)
        aux = aux_ref[s]
        kit_s[s, :, 0:past] = cikt_ref[s].astype(BF16)
        kit_s[s, :, past:] = jnp.zeros((IDX_DIM, n_tiles * tk - past), BF16)
        kit_s[s, :, past:past + tq] = aux.T[:IDX_DIM, :].astype(BF16)
        k_s[s, past + tq:, :] = jnp.zeros((n_tiles * tk - n_keys, KV_DIM), BF16)
        vt_s[s, n_tiles - 1] = jnp.zeros((KV_DIM, tk), BF16)
        for n in range(N_KV_HEADS):
            cols = slice(n * HEAD_DIM, (n + 1) * HEAD_DIM)
            k_s[s, 0:past, cols] = ck_ref[s, pl.ds(n, past, stride=N_KV_HEADS), :].astype(BF16)
            k_s[s, past:past + tq, cols] = nk_ref[s, pl.ds(n, tq, stride=N_KV_HEADS), :].astype(BF16)
            for kt in range(past // tk):
                v_tile = cv_ref[s, pl.ds(N_KV_HEADS * kt * tk + n, tk, stride=N_KV_HEADS), :]
                vt_s[s, kt, cols, :] = v_tile.T.astype(BF16)
            vt_s[s, n_tiles - 1, cols, 0:tq] = nv_ref[s, pl.ds(n, tq, stride=N_KV_HEADS), :].T.astype(BF16)

        qi = qi_ref[s]
        qrows = jnp.concatenate([qi[:, h * IDX_DIM:(h + 1) * IDX_DIM] for h in range(N_IDX_HEADS)], axis=0)
        w_rows = jnp.concatenate([aux[:, IDX_DIM + h:IDX_DIM + h + 1] for h in range(N_IDX_HEADS)], axis=0)
        w_rows = jnp.broadcast_to(w_rows, (hl, LANES))

        def idx_tile(kt, carry):
            d = jnp.dot(qrows, kit_s[s, :, kt * tk:(kt + 1) * tk], preferred_element_type=F32)
            y = jnp.concatenate([jnp.maximum(d[:, c:c + LANES], 0.0) * w_rows for c in range(0, tk, LANES)],
                                axis=1)
            per_query = _vreg_sum(y, tq) * IDX_SCALE
            score = jnp.concatenate([per_query] * N_HEADS, axis=0).T
            pos = kt * tk + lax.broadcasted_iota(I32, (tk, LANES), 0)
            _store_scores(score, pos < n_keys, sc_scr.at[:, :, lanes], hi_scr.at[:, :, lanes],
                          lo_scr.at[:, :, lanes], kt)
            return carry

        for_tiles(idx_tile, 0)

    _topk_threshold(for_tiles, sc_scr, hi_scr, lo_scr, thr_scr, tk=tk, width=width, n_cols=n_tiles * tk)
    thr = thr_scr[0:1, :]

    zeros = jnp.zeros((tq, HEAD_DIM), BF16)
    for s in streams:
        lanes = slice(s * LANES, (s + 1) * LANES)
        q = q_ref[s]
        qblk = jnp.concatenate(
            [jnp.concatenate([q[:, (KV_GROUP * n + g) * HEAD_DIM:(KV_GROUP * n + g + 1) * HEAD_DIM]
                              if m == n else zeros for m in range(N_KV_HEADS)], axis=1)
             for n in range(N_KV_HEADS) for g in range(KV_GROUP)], axis=0)

        def logits_tile(kt, m):
            lg = jnp.where(sc_scr[kt, :, lanes] >= thr[:, lanes],
                           _dot_nt(k_s[s, kt * tk:(kt + 1) * tk, :], qblk), NEG_INF)
            lg_scr[kt] = lg
            return jnp.maximum(m, lg)

        m = jnp.max(for_tiles(logits_tile, jnp.full((tk, LANES), NEG_INF, F32)), axis=0, keepdims=True)
        m = jnp.where(m == NEG_INF, 0.0, m)

        def pv_tile(kt, carry):
            l, acc = carry
            p = jnp.exp2(lg_scr[kt] - m)
            return (l + _vreg_sum(p, SUBLANES),
                    acc + jnp.dot(vt_s[s, kt], p.astype(BF16), preferred_element_type=F32))

        l, acc = for_tiles(pv_tile, (jnp.zeros((SUBLANES, LANES), F32), jnp.zeros((KV_DIM, LANES), F32)))
        o_t = (acc / jnp.sum(l, axis=0, keepdims=True)).T
        outs = [o_t[(KV_GROUP * n + g) * tq:(KV_GROUP * n + g + 1) * tq, n * HEAD_DIM:(n + 1) * HEAD_DIM]
                for n in range(N_KV_HEADS) for g in range(KV_GROUP)]
        z_ref[s] = (jnp.concatenate(outs, axis=1) * _silu(ga_ref[s])).astype(BF16)


def _attend_sample(qi, aux, q, ga, cache_ik, cache_k, cache_v, new_k, new_v, tk):
    nseq, tq, _ = q.shape
    past = cache_ik.shape[1]
    assert nseq % SAMPLE_STREAMS == 0
    cache_ikt = jnp.transpose(cache_ik, (0, 2, 1))
    n_tiles = -(-(past + tq) // tk)
    width = SAMPLE_STREAMS * LANES
    blk = lambda n, m=1: pl.BlockSpec((SAMPLE_STREAMS, m * tq, n), lambda b: (b, 0, 0))
    cache = lambda n, m=1: pl.BlockSpec((SAMPLE_STREAMS, m * past, n), lambda b: (b, 0, 0))
    scratch = [pltpu.VMEM((SAMPLE_STREAMS, IDX_DIM, n_tiles * tk), BF16),
               pltpu.VMEM((SAMPLE_STREAMS, n_tiles * tk, KV_DIM), BF16),
               pltpu.VMEM((SAMPLE_STREAMS, n_tiles, KV_DIM, tk), BF16),
               pltpu.VMEM((n_tiles, tk, width), F32), pltpu.VMEM((n_tiles, tk, width), I16),
               pltpu.VMEM((n_tiles, tk, width), I16),
               pltpu.VMEM((SUBLANES, width), F32), pltpu.VMEM((n_tiles, tk, LANES), F32)]
    return pl.pallas_call(
        functools.partial(_attend_sample_kernel, tq=tq, tk=tk, past=past, n_tiles=n_tiles),
        grid=(nseq // SAMPLE_STREAMS,),
        in_specs=[blk(N_IDX_HEADS * IDX_DIM), blk(LANES), blk(D_ATTN), blk(D_ATTN),
                  pl.BlockSpec((SAMPLE_STREAMS, IDX_DIM, past), lambda b: (b, 0, 0)),
                  cache(HEAD_DIM, N_KV_HEADS), cache(HEAD_DIM, N_KV_HEADS), blk(HEAD_DIM, N_KV_HEADS),
                  blk(HEAD_DIM, N_KV_HEADS)],
        out_specs=blk(D_ATTN),
        out_shape=jax.ShapeDtypeStruct((nseq, tq, D_ATTN), BF16),
        scratch_shapes=scratch,
        compiler_params=pltpu.CompilerParams(dimension_semantics=("arbitrary",),
                                             vmem_limit_bytes=VMEM_LIMIT_BYTES),
        name="attend_sample",
    )(qi, aux, q, ga, cache_ikt, cache_k, cache_v, new_k, new_v)


def _merge_kernel(x_ref, za_ref, zc_ref, w_ref, g_ref, y_ref):
    z = jnp.concatenate([za_ref[...], zc_ref[...]], axis=1)
    y = jnp.dot(z, w_ref[...], preferred_element_type=F32)
    y_ref[...] = x_ref[...] + _rmsnorm(y, g_ref[...])


def _merge(x, za, zc, w_out, g_post, tm):
    rows = x.shape[0]
    row_spec = lambda n: pl.BlockSpec((tm, n), lambda i: (i, 0))
    return pl.pallas_call(
        _merge_kernel,
        grid=(rows // tm,),
        in_specs=[row_spec(D_MODEL), row_spec(D_ATTN), row_spec(D_CONV), _resident((D_MODEL, D_MODEL), 1),
                  _resident((1, D_MODEL), 1)],
        out_specs=row_spec(D_MODEL),
        out_shape=jax.ShapeDtypeStruct((rows, D_MODEL), F32),
        compiler_params=pltpu.CompilerParams(dimension_semantics=("arbitrary",),
                                             vmem_limit_bytes=VMEM_LIMIT_BYTES),
        name="merge",
    )(x, za, zc, w_out, g_post)


PROJECT_ROWS = 256
ATTEND_ROWS = 256
MERGE_ROWS = 512


def _layer(xp, xs, cache_k, cache_v, cache_ik, state, g_pre, w_in, w_conv, w_out, g_post):
    nb, t, _ = xp.shape
    nseq, seqlen, _ = xs.shape
    g_pre = g_pre.reshape(1, D_MODEL)
    g_post = g_post.reshape(1, D_MODEL)
    w_t = w_in.T.astype(BF16)
    w_out = w_out.astype(BF16)

    assert PROJECT_ROWS == ATTEND_ROWS
    q, k, v, ga, qi, ik, aux, zc, kpad, kb, vt, cs = _project_prompt(xp, g_pre, w_t, w_conv, PROJECT_ROWS)
    za = _attend_prompt(qi, aux, q, ga, kpad, kb, vt, ATTEND_ROWS)
    yp = _merge(xp.reshape(nb * t, D_MODEL), za.reshape(nb * t, D_ATTN), zc.reshape(nb * t, D_CONV),
                w_out, g_post, MERGE_ROWS).reshape(nb, t, D_MODEL)

    sq, sk, sv, sga, sqi, sik, saux, szc, scs = _project_sample(xs, g_pre, w_t, w_conv, state)
    per_seq = lambda a: a.reshape(nseq, -1, a.shape[-1])
    kv_rows = lambda a: a.reshape(nseq, -1, HEAD_DIM)
    sza = _attend_sample(per_seq(sqi), per_seq(saux), per_seq(sq), per_seq(sga), cache_ik,
                         kv_rows(cache_k), kv_rows(cache_v), per_seq(sk), per_seq(sv), ATTEND_ROWS)
    ys = _merge(xs.reshape(nseq * seqlen, D_MODEL), sza.reshape(nseq * seqlen, D_ATTN), szc, w_out,
                g_post, MERGE_ROWS).reshape(nseq, seqlen, D_MODEL)

    heads = lambda a, lead: a.reshape(lead + (N_KV_HEADS, HEAD_DIM))
    return (yp, ys, heads(k, (nb, t)), heads(v, (nb, t)), jnp.transpose(ik, (0, 2, 1)), cs,
            heads(sk, (nseq, seqlen)), heads(sv, (nseq, seqlen)), per_seq(sik), scs)


def kernel(x_prompt, x_sample, cache_k, cache_v, cache_idx_k, state_conv, g_pre, w_in, w_conv, w_out,
           g_post):
    depth = g_pre.shape[0]
    xp, xs = x_prompt, x_sample
    outs = []
    for l in range(depth):
        res = _layer(xp, xs, cache_k[l], cache_v[l], cache_idx_k[l], state_conv[l], g_pre[l], w_in[l],
                     w_conv[l], w_out[l], g_post[l])
        xp, xs = res[0], res[1]
        outs.append(res[2:])
    stacked = [jnp.stack([o[i] for o in outs]) for i in range(8)]
    return (xp, xs) + tuple(stacked)
```
